```python
import math
import jax, jax.numpy as jnp
from jax import lax
import numpy as np

D_MODEL = 4096
BATCH = 8
SEQ = 4096
DEPTH = 4

N_META = 16
D_MIX = D_MODEL
MLA_HEADS = D_MODEL // 256
MLA_NOPE = 128
MLA_ROPE = 64
MLA_V = 128
Q_LORA = D_MODEL // 4
KV_LORA = D_MODEL // 8
SB_HEADS = D_MODEL // 256
SB_HEAD_DIM = 128
W_MLA = MLA_HEADS * MLA_V
W_SB = SB_HEADS * SB_HEAD_DIM
BLOCK_Q = 128
ROPE_THETA = 10000.0
EPS = 1e-6
MLA_SCALE = 1.0 / math.sqrt(MLA_NOPE + MLA_ROPE)
SB_SCALE = 1.0 / math.sqrt(SB_HEAD_DIM)
IN_SIZES = (Q_LORA, KV_LORA, MLA_ROPE, W_MLA, W_SB, W_SB, W_SB, W_SB)
D_IN = Q_LORA + KV_LORA + MLA_ROPE + W_MLA + 4 * W_SB

kernel_name = "hymba_mla_stickbreaking_hybrid"


def rms_norm(x, g):
    xf = x.astype(jnp.float32)
    y = xf * lax.rsqrt(jnp.mean(xf * xf, axis=-1, keepdims=True) + EPS)
    return (y * g.astype(jnp.float32)).astype(x.dtype)


def rope_tables(pos, dtype):
    inv_freq = ROPE_THETA ** (-jnp.arange(0, MLA_ROPE, 2, dtype=jnp.float32) / MLA_ROPE)
    ang = pos.astype(jnp.float32)[:, None] * inv_freq[None, :]
    return jnp.cos(ang).astype(dtype), jnp.sin(ang).astype(dtype)


def apply_rope(x, cos, sin):
    half = x.shape[-1] // 2
    x1, x2 = x[..., :half], x[..., half:]
    return jnp.concatenate([x1 * cos - x2 * sin, x2 * cos + x1 * sin], axis=-1)


def mla_block(q_nope, q_rope, k_nope, k_rope, v, q_pos, k_pos):
    s = (jnp.einsum('bqhd,bkhd->bhqk', q_nope, k_nope, preferred_element_type=jnp.float32)
         + jnp.einsum('bqhr,bkr->bhqk', q_rope, k_rope, preferred_element_type=jnp.float32))
    s = s * MLA_SCALE
    causal = k_pos[None, :] <= q_pos[:, None]
    s = jnp.where(causal, s, -jnp.inf)
    p = jax.nn.softmax(s, axis=-1)
    return jnp.einsum('bhqk,bkhd->bqhd', p.astype(v.dtype), v)


def sb_block(q, k, v, q_pos, k_pos):
    z = jnp.einsum('bqhd,bkhd->bhqk', q, k, preferred_element_type=jnp.float32) * SB_SCALE
    strict = k_pos[None, :] < q_pos[:, None]
    log_beta = jax.nn.log_sigmoid(z)
    log_one_minus = jnp.where(strict, jax.nn.log_sigmoid(-z), 0.0)
    log_remaining = lax.cumsum(log_one_minus, axis=3, reverse=True) - log_one_minus
    a = jnp.where(strict, jnp.exp(log_beta + log_remaining), 0.0)
    return jnp.einsum('bhqk,bkhd->bqhd', a.astype(v.dtype), v)


def blocked_attention(block_fn, qs, kvs, pos):
    B, L = qs[0].shape[0], qs[0].shape[1]
    meta_out = block_fn(*[a[:, :N_META] for a in qs], *[a[:, :N_META] for a in kvs],
                        pos[:N_META], pos[:N_META])
    n_blk = (L - N_META) // BLOCK_Q

    def to_blocks(a):
        r = a[:, N_META:].reshape((B, n_blk, BLOCK_Q) + a.shape[2:])
        return jnp.moveaxis(r, 1, 0)

    q_blocks = tuple(to_blocks(a) for a in qs)
    pos_blocks = pos[N_META:].reshape(n_blk, BLOCK_Q)

    def body(args):
        *qb, pb = args
        return block_fn(*qb, *kvs, pb, pos)

    out = lax.map(body, (*q_blocks, pos_blocks))
    out = jnp.moveaxis(out, 0, 1).reshape((B, L - N_META) + out.shape[3:])
    return jnp.concatenate([meta_out, out], axis=1)


def hybrid_layer(h, g_norm, w_in, g_q, g_kv, w_uq, w_ukv, g_out_mla, g_out_sb, w_o,
                 cos, sin, pos):
    B, L, _ = h.shape
    u = rms_norm(h, g_norm)
    proj = u @ w_in
    split_points = tuple(int(v) for v in np.cumsum(IN_SIZES)[:-1])
    c_q, c_kv, k_r, z_mla, q_sb, k_sb, v_sb, z_sb = jnp.split(proj, split_points, axis=-1)

    q = (rms_norm(c_q, g_q) @ w_uq).reshape(B, L, MLA_HEADS, MLA_NOPE + MLA_ROPE)
    q_nope = q[..., :MLA_NOPE]
    q_rope = apply_rope(q[..., MLA_NOPE:], cos[:, None, :], sin[:, None, :])
    kv = (rms_norm(c_kv, g_kv) @ w_ukv).reshape(B, L, MLA_HEADS, MLA_NOPE + MLA_V)
    k_nope, v_mla = kv[..., :MLA_NOPE], kv[..., MLA_NOPE:]
    k_rope = apply_rope(k_r, cos, sin)
    y_mla = blocked_attention(mla_block, (q_nope, q_rope), (k_nope, k_rope, v_mla), pos)
    y_mla = y_mla.reshape(B, L, W_MLA)

    q_s = q_sb.reshape(B, L, SB_HEADS, SB_HEAD_DIM)
    k_s = k_sb.reshape(B, L, SB_HEADS, SB_HEAD_DIM)
    v_s = v_sb.reshape(B, L, SB_HEADS, SB_HEAD_DIM)
    y_sb = blocked_attention(sb_block, (q_s,), (k_s, v_s), pos).reshape(B, L, W_SB)

    y = jnp.concatenate([rms_norm(y_mla, g_out_mla) * jax.nn.silu(z_mla),
                         rms_norm(y_sb, g_out_sb) * jax.nn.silu(z_sb)], axis=-1)
    return h + y @ w_o


def _fwd_setup_inputs(seed: int = 0) -> dict:
    key = jax.random.key(seed)
    ks = jax.random.split(key, 13)
    f32 = jnp.float32
    x = jax.random.normal(ks[0], (BATCH, SEQ, D_MODEL), f32)
    meta_tokens = jax.random.normal(ks[1], (N_META, D_MODEL), f32)
    g_norm = 1.0 + 0.02 * jax.random.normal(ks[2], (DEPTH, D_MODEL), f32)
    w_in = jax.random.normal(ks[3], (DEPTH, D_MODEL, D_IN), f32) * D_MODEL ** -0.5
    g_q = 1.0 + 0.02 * jax.random.normal(ks[4], (DEPTH, Q_LORA), f32)
    g_kv = 1.0 + 0.02 * jax.random.normal(ks[5], (DEPTH, KV_LORA), f32)
    w_uq = jax.random.normal(ks[6], (DEPTH, Q_LORA, MLA_HEADS * (MLA_NOPE + MLA_ROPE)), f32) * Q_LORA ** -0.5
    w_ukv = jax.random.normal(ks[7], (DEPTH, KV_LORA, MLA_HEADS * (MLA_NOPE + MLA_V)), f32) * KV_LORA ** -0.5
    g_out_mla = 1.0 + 0.02 * jax.random.normal(ks[8], (DEPTH, W_MLA), f32)
    g_out_sb = 1.0 + 0.02 * jax.random.normal(ks[9], (DEPTH, W_SB), f32)
    w_o = jax.random.normal(ks[10], (DEPTH, D_MIX, D_MODEL), f32) * D_MIX ** -0.5
    g_final = 1.0 + 0.02 * jax.random.normal(ks[11], (D_MODEL,), f32)
    return {"x": x, "meta_tokens": meta_tokens, "g_norm": g_norm, "w_in": w_in,
            "g_q": g_q, "g_kv": g_kv, "w_uq": w_uq, "w_ukv": w_ukv,
            "g_out_mla": g_out_mla, "g_out_sb": g_out_sb, "w_o": w_o,
            "g_final": g_final}


def _fwd_reference(x, meta_tokens, g_norm, w_in, g_q, g_kv, w_uq, w_ukv, g_out_mla, g_out_sb,
              w_o, g_final):
    B = x.shape[0]
    meta = jnp.broadcast_to(meta_tokens[None].astype(x.dtype), (B, N_META, D_MODEL))
    h = jnp.concatenate([meta, x], axis=1)
    L = h.shape[1]
    pos = jnp.arange(L, dtype=jnp.int32)
    cos, sin = rope_tables(pos, x.dtype)
    for i in range(DEPTH):
        h = hybrid_layer(h, g_norm[i], w_in[i], g_q[i], g_kv[i], w_uq[i], w_ukv[i],
                         g_out_mla[i], g_out_sb[i], w_o[i], cos, sin, pos)
    return rms_norm(h[:, N_META:], g_final)


import jax as _jax
import jax.numpy as _jnp

TWIN_FORMAT = 'train_step'
FWD_PARAMS = ['x', 'meta_tokens', 'g_norm', 'w_in', 'g_q', 'g_kv', 'w_uq', 'w_ukv', 'g_out_mla', 'g_out_sb', 'w_o', 'g_final']
TWIN_WEIGHTS = ['meta_tokens', 'g_norm', 'w_in', 'g_q', 'g_kv', 'w_uq', 'w_ukv', 'g_out_mla', 'g_out_sb', 'w_o', 'g_final']
TWIN_DIFF_INPUT = 'x'
TWIN_INPUTS = ['x', 'meta_tokens', 'g_norm', 'w_in', 'g_q', 'g_kv', 'w_uq', 'w_ukv', 'g_out_mla', 'g_out_sb', 'w_o', 'g_final', 'loss_target', 'm_meta_tokens', 'm_g_norm', 'm_w_in', 'm_g_q', 'm_g_kv', 'm_w_uq', 'm_w_ukv', 'm_g_out_mla', 'm_g_out_sb', 'm_w_o', 'm_g_final', 'v_meta_tokens', 'v_g_norm', 'v_w_in', 'v_g_q', 'v_g_kv', 'v_w_uq', 'v_w_ukv', 'v_g_out_mla', 'v_g_out_sb', 'v_w_o', 'v_g_final']
TWIN_OUTPUTS = ['loss', 'grad_x', 'grad_meta_tokens', 'grad_g_norm', 'grad_w_in', 'grad_g_q', 'grad_g_kv', 'grad_w_uq', 'grad_w_ukv', 'grad_g_out_mla', 'grad_g_out_sb', 'grad_w_o', 'grad_g_final', 'delta_meta_tokens', 'delta_g_norm', 'delta_w_in', 'delta_g_q', 'delta_g_kv', 'delta_w_uq', 'delta_w_ukv', 'delta_g_out_mla', 'delta_g_out_sb', 'delta_w_o', 'delta_g_final', 'new_m_meta_tokens', 'new_m_g_norm', 'new_m_w_in', 'new_m_g_q', 'new_m_g_kv', 'new_m_w_uq', 'new_m_w_ukv', 'new_m_g_out_mla', 'new_m_g_out_sb', 'new_m_w_o', 'new_m_g_final', 'new_v_meta_tokens', 'new_v_g_norm', 'new_v_w_in', 'new_v_g_q', 'new_v_g_kv', 'new_v_w_uq', 'new_v_w_ukv', 'new_v_g_out_mla', 'new_v_g_out_sb', 'new_v_w_o', 'new_v_g_final']
TWIN_LEAF_KINDS = {'loss': 'loss', 'grad_x': 'grad_x', 'grad_meta_tokens': 'grad_w', 'grad_g_norm': 'grad_w', 'grad_w_in': 'grad_w', 'grad_g_q': 'grad_w', 'grad_g_kv': 'grad_w', 'grad_w_uq': 'grad_w', 'grad_w_ukv': 'grad_w', 'grad_g_out_mla': 'grad_w', 'grad_g_out_sb': 'grad_w', 'grad_w_o': 'grad_w', 'grad_g_final': 'grad_w', 'delta_meta_tokens': 'delta_w', 'delta_g_norm': 'delta_w', 'delta_w_in': 'delta_w', 'delta_g_q': 'delta_w', 'delta_g_kv': 'delta_w', 'delta_w_uq': 'delta_w', 'delta_w_ukv': 'delta_w', 'delta_g_out_mla': 'delta_w', 'delta_g_out_sb': 'delta_w', 'delta_w_o': 'delta_w', 'delta_g_final': 'delta_w', 'new_m_meta_tokens': 'new_m', 'new_m_g_norm': 'new_m', 'new_m_w_in': 'new_m', 'new_m_g_q': 'new_m', 'new_m_g_kv': 'new_m', 'new_m_w_uq': 'new_m', 'new_m_w_ukv': 'new_m', 'new_m_g_out_mla': 'new_m', 'new_m_g_out_sb': 'new_m', 'new_m_w_o': 'new_m', 'new_m_g_final': 'new_m', 'new_v_meta_tokens': 'new_v', 'new_v_g_norm': 'new_v', 'new_v_w_in': 'new_v', 'new_v_g_q': 'new_v', 'new_v_g_kv': 'new_v', 'new_v_w_uq': 'new_v', 'new_v_w_ukv': 'new_v', 'new_v_g_out_mla': 'new_v', 'new_v_g_out_sb': 'new_v', 'new_v_w_o': 'new_v', 'new_v_g_final': 'new_v'}


def _forward(args):
    return _fwd_reference(*[args[k] for k in FWD_PARAMS])


def _output_shape():
    out = _jax.eval_shape(lambda: _forward(_fwd_setup_inputs(0)))
    return out.shape, out.dtype

N_MICROBATCH = 1
ADAM_LR = 0.001
ADAM_B1 = 0.9
ADAM_B2 = 0.999
ADAM_EPS = 1e-08
ADAM_WD = 0.01
ADAM_STEP = 10
PER_EXAMPLE_BATCH_AXIS = {'x': 0, 'loss_target': 0}
SHARED_INPUTS = []
_WEIGHT_DTYPES = {'meta_tokens': _jnp.float32, 'g_norm': _jnp.float32, 'w_in': _jnp.float32, 'g_q': _jnp.float32, 'g_kv': _jnp.float32, 'w_uq': _jnp.float32, 'w_ukv': _jnp.float32, 'g_out_mla': _jnp.float32, 'g_out_sb': _jnp.float32, 'w_o': _jnp.float32, 'g_final': _jnp.float32}
MOMENT_SCALE = {'meta_tokens': 6.271278e-03, 'g_norm': 4.415056e-02, 'w_in': 2.596483e-02, 'g_q': 3.150631e-02, 'g_kv': 7.452739e-02, 'w_uq': 1.800433e-02, 'w_ukv': 2.389915e-02, 'g_out_mla': 2.909860e-02, 'g_out_sb': 2.496409e-02, 'w_o': 2.627750e-02, 'g_final': 7.985521e+00}


def _to_microbatches(a, axis):
    t = _jnp.moveaxis(a, axis, 0)
    t = t.reshape((N_MICROBATCH, t.shape[0] // N_MICROBATCH) + t.shape[1:])
    return _jnp.moveaxis(t, 1, axis + 1)


def setup_inputs(seed: int = 0) -> dict:
    inp = _fwd_setup_inputs(seed)
    key = _jax.random.fold_in(_jax.random.key(seed), 7919)
    shape, _ = _output_shape()
    out = dict(inp)
    out["loss_target"] = _jax.random.normal(_jax.random.fold_in(key, 0), shape, _jnp.float32)
    for i, name in enumerate(TWIN_WEIGHTS):
        w = inp[name].astype(_jnp.float32)
        if MOMENT_SCALE is None:
            s = _jnp.sqrt(_jnp.mean(_jnp.square(w)) + 1e-30)
        else:
            s = MOMENT_SCALE[name]
        km, kv = _jax.random.split(_jax.random.fold_in(key, i + 1))
        out[name] = w
        out["m_" + name] = s * _jax.random.normal(km, w.shape, _jnp.float32)
        out["v_" + name] = (s * s) * _jax.random.uniform(kv, w.shape, _jnp.float32, 0.5, 1.5)
    if N_MICROBATCH > 1:
        for name, axis in PER_EXAMPLE_BATCH_AXIS.items():
            out[name] = _to_microbatches(out[name], axis)
    return {'x': out['x'], 'meta_tokens': out['meta_tokens'], 'g_norm': out['g_norm'], 'w_in': out['w_in'], 'g_q': out['g_q'], 'g_kv': out['g_kv'], 'w_uq': out['w_uq'], 'w_ukv': out['w_ukv'], 'g_out_mla': out['g_out_mla'], 'g_out_sb': out['g_out_sb'], 'w_o': out['w_o'], 'g_final': out['g_final'], 'loss_target': out['loss_target'], 'm_meta_tokens': out['m_meta_tokens'], 'm_g_norm': out['m_g_norm'], 'm_w_in': out['m_w_in'], 'm_g_q': out['m_g_q'], 'm_g_kv': out['m_g_kv'], 'm_w_uq': out['m_w_uq'], 'm_w_ukv': out['m_w_ukv'], 'm_g_out_mla': out['m_g_out_mla'], 'm_g_out_sb': out['m_g_out_sb'], 'm_w_o': out['m_w_o'], 'm_g_final': out['m_g_final'], 'v_meta_tokens': out['v_meta_tokens'], 'v_g_norm': out['v_g_norm'], 'v_w_in': out['v_w_in'], 'v_g_q': out['v_g_q'], 'v_g_kv': out['v_g_kv'], 'v_w_uq': out['v_w_uq'], 'v_w_ukv': out['v_w_ukv'], 'v_g_out_mla': out['v_g_out_mla'], 'v_g_out_sb': out['v_g_out_sb'], 'v_w_o': out['v_w_o'], 'v_g_final': out['v_g_final']}


def _loss(weights, diff, rest, loss_target):
    with _jax.named_scope("forward"):
        args = {**rest, TWIN_DIFF_INPUT: diff, **{k: w.astype(_WEIGHT_DTYPES[k]) for k, w in weights.items()}}
        y = _forward(args)
    with _jax.named_scope("loss_head"):
        err = _jnp.square(y.astype(_jnp.float32) - loss_target)
        return 0.5 * _jnp.sum(_jnp.mean(err, axis=-1)) if err.ndim else 0.5 * err


def _adamw(w, g, m, v):
    m = ADAM_B1 * m + (1.0 - ADAM_B1) * g
    v = ADAM_B2 * v + (1.0 - ADAM_B2) * _jnp.square(g)
    m_hat = m / (1.0 - ADAM_B1 ** ADAM_STEP)
    v_hat = v / (1.0 - ADAM_B2 ** ADAM_STEP)
    delta = -ADAM_LR * (m_hat / (_jnp.sqrt(v_hat) + ADAM_EPS) + ADAM_WD * w)
    return delta, m, v


def reference(x, meta_tokens, g_norm, w_in, g_q, g_kv, w_uq, w_ukv, g_out_mla, g_out_sb, w_o, g_final, loss_target, m_meta_tokens, m_g_norm, m_w_in, m_g_q, m_g_kv, m_w_uq, m_w_ukv, m_g_out_mla, m_g_out_sb, m_w_o, m_g_final, v_meta_tokens, v_g_norm, v_w_in, v_g_q, v_g_kv, v_w_uq, v_w_ukv, v_g_out_mla, v_g_out_sb, v_w_o, v_g_final):
    given = dict(x=x, meta_tokens=meta_tokens, g_norm=g_norm, w_in=w_in, g_q=g_q, g_kv=g_kv, w_uq=w_uq, w_ukv=w_ukv, g_out_mla=g_out_mla, g_out_sb=g_out_sb, w_o=w_o, g_final=g_final, loss_target=loss_target, m_meta_tokens=m_meta_tokens, m_g_norm=m_g_norm, m_w_in=m_w_in, m_g_q=m_g_q, m_g_kv=m_g_kv, m_w_uq=m_w_uq, m_w_ukv=m_w_ukv, m_g_out_mla=m_g_out_mla, m_g_out_sb=m_g_out_sb, m_w_o=m_w_o, m_g_final=m_g_final, v_meta_tokens=v_meta_tokens, v_g_norm=v_g_norm, v_w_in=v_w_in, v_g_q=v_g_q, v_g_kv=v_g_kv, v_w_uq=v_w_uq, v_w_ukv=v_w_ukv, v_g_out_mla=v_g_out_mla, v_g_out_sb=v_g_out_sb, v_w_o=v_w_o, v_g_final=v_g_final)
    weights = {n: given[n] for n in TWIN_WEIGHTS}
    shared = {n: given[n] for n in SHARED_INPUTS}
    per_example = {n: given[n] for n in ['x']}
    grad_fn = _jax.value_and_grad(_loss, argnums=(0, 1))

    def one_microbatch(ex, loss_target):
        ex = dict(ex)
        diff = ex.pop(TWIN_DIFF_INPUT)
        return grad_fn(weights, diff, {**shared, **ex}, loss_target)

    if N_MICROBATCH == 1:
        loss, (grad_w, grad_x) = one_microbatch(per_example, given["loss_target"])
    else:
        def body(carry, xs):
            loss_sum, grad_sum = carry
            l_k, (gw_k, gx_k) = one_microbatch(xs[0], xs[1])
            with _jax.named_scope("update"):
                return (loss_sum + l_k, _jax.tree.map(_jnp.add, grad_sum, gw_k)), gx_k

        init = (_jnp.zeros((), _jnp.float32), _jax.tree.map(_jnp.zeros_like, weights))
        (loss, grad_w), grad_x = _jax.lax.scan(body, init, (per_example, given["loss_target"]))
    with _jax.named_scope("update"):
        delta_w, new_m, new_v = {}, {}, {}
        for n in TWIN_WEIGHTS:
            delta_w[n], new_m[n], new_v[n] = _adamw(weights[n], grad_w[n], given["m_" + n], given["v_" + n])
    return (loss, grad_x, *[grad_w[n] for n in TWIN_WEIGHTS], *[delta_w[n] for n in TWIN_WEIGHTS],
            *[new_m[n] for n in TWIN_WEIGHTS], *[new_v[n] for n in TWIN_WEIGHTS])
```

```python
import math

import numpy as np
import jax
import jax.numpy as jnp
from jax import lax
from jax.experimental import pallas as pl
from jax.experimental.pallas import tpu as pltpu

CD = jnp.bfloat16
F32 = jnp.float32

LANES = 128
N_DEV = 8
EPS = 1e-6
ROPE = 64
ROPE_THETA = 10000.0
NEG = -1e30
VMEM_LIMIT = 48 * 2**20

ADAM_LR, ADAM_B1, ADAM_B2, ADAM_EPS, ADAM_WD, ADAM_STEP = 0.001, 0.9, 0.999, 1e-08, 0.01, 10
ADAM_BC1 = 1.0 - ADAM_B1**ADAM_STEP
ADAM_BC2 = 1.0 - ADAM_B2**ADAM_STEP

MESH = pl.DeviceIdType.MESH


def _cp(*sem):
    return pltpu.CompilerParams(dimension_semantics=tuple(sem) if sem else None, vmem_limit_bytes=VMEM_LIMIT)


def _pick(dim, prefs):
    for p in prefs:
        if dim % p == 0:
            return p
    raise ValueError(f"no tile for {dim} in {prefs}")


def _dot(a, b):
    return jnp.dot(a, b, preferred_element_type=F32)


def _dot_nt(a, b):
    return lax.dot_general(a, b, (((1,), (1,)), ((), ())), preferred_element_type=F32)


def _dot_tn(a, b):
    return lax.dot_general(a, b, (((0,), (0,)), ((), ())), preferred_element_type=F32)


def _mm(a, b, *, ta=False, tb=False, out_dtype, res=None, name):
    M, K = (a.shape[1], a.shape[0]) if ta else a.shape
    N = b.shape[0] if tb else b.shape[1]
    tm = _pick(M, (1056, 1024, 512, 256, 128))
    tn = _pick(N, (1024, 512, 384, 256, 128))
    tk = _pick(K, (1024, 512, 384, 256, 128))
    nk = K // tk
    a_dims = (((0,), (1 if tb else 0,)), ((), ())) if ta else (((1,), (1 if tb else 0,)), ((), ()))

    def body(*refs):
        if res is None:
            a_ref, b_ref, o_ref, acc = refs
        else:
            a_ref, b_ref, r_ref, o_ref, acc = refs
        k = pl.program_id(2)

        @pl.when(k == 0)
        def _():
            acc[...] = jnp.zeros_like(acc)

        acc[...] += lax.dot_general(a_ref[...].astype(CD), b_ref[...].astype(CD), a_dims, preferred_element_type=F32)

        @pl.when(k == nk - 1)
        def _():
            r = acc[...]
            if res is not None:
                r = r + r_ref[...]
            o_ref[...] = r.astype(out_dtype)

    a_spec = pl.BlockSpec((tk, tm), lambda i, j, k: (k, i)) if ta else pl.BlockSpec((tm, tk), lambda i, j, k: (i, k))
    b_spec = pl.BlockSpec((tn, tk), lambda i, j, k: (j, k)) if tb else pl.BlockSpec((tk, tn), lambda i, j, k: (k, j))
    o_spec = pl.BlockSpec((tm, tn), lambda i, j, k: (i, j))
    in_specs, args = [a_spec, b_spec], [a, b]
    if res is not None:
        in_specs.append(o_spec)
        args.append(res)
    return pl.pallas_call(
        body, grid=(M // tm, N // tn, nk), in_specs=in_specs, out_specs=o_spec,
        out_shape=jax.ShapeDtypeStruct((M, N), out_dtype), scratch_shapes=[pltpu.VMEM((tm, tn), F32)],
        compiler_params=_cp("parallel", "parallel", "arbitrary"), name=name,
    )(*args)


def _rms_fwd(x, g, *, cb, width, tr, name):
    rows = x.shape[0]

    def body(x_ref, g_ref, o_ref):
        xv = x_ref[...].astype(F32)
        r = lax.rsqrt(jnp.mean(xv * xv, axis=-1, keepdims=True) + EPS)
        o_ref[...] = (xv * r * g_ref[...]).astype(o_ref.dtype)

    return pl.pallas_call(
        body, grid=(rows // tr,),
        in_specs=[pl.BlockSpec((tr, width), lambda i: (i, cb)), pl.BlockSpec((1, width), lambda i: (0, 0))],
        out_specs=pl.BlockSpec((tr, width), lambda i: (i, 0)),
        out_shape=jax.ShapeDtypeStruct((rows, width), CD), compiler_params=_cp("parallel"), name=name,
    )(x, g)


def _rms_bwd(x, g, dy, *, cb, width, tr, res=None, out_dtype, name):
    rows = x.shape[0]

    def body(*refs):
        if res is None:
            x_ref, g_ref, dy_ref, dx_ref, dg_ref = refs
        else:
            x_ref, g_ref, dy_ref, r_ref, dx_ref, dg_ref = refs

        @pl.when(pl.program_id(0) == 0)
        def _():
            dg_ref[...] = jnp.zeros_like(dg_ref)

        xv = x_ref[...].astype(F32)
        r = lax.rsqrt(jnp.mean(xv * xv, axis=-1, keepdims=True) + EPS)
        xh = xv * r
        dyv = dy_ref[...].astype(F32)
        dxh = dyv * g_ref[...]
        dx = r * (dxh - xh * jnp.mean(dxh * xh, axis=-1, keepdims=True))
        if res is not None:
            dx = dx + r_ref[...]
        dx_ref[...] = dx.astype(out_dtype)
        dg_ref[...] += jnp.sum(dyv * xh, axis=0, keepdims=True)

    blk = pl.BlockSpec((tr, width), lambda i: (i, 0))
    in_specs = [pl.BlockSpec((tr, width), lambda i: (i, cb)), pl.BlockSpec((1, width), lambda i: (0, 0)), blk]
    args = [x, g, dy]
    if res is not None:
        in_specs.append(blk)
        args.append(res)
    return pl.pallas_call(
        body, grid=(rows // tr,), in_specs=in_specs,
        out_specs=[blk, pl.BlockSpec((1, width), lambda i: (0, 0))],
        out_shape=[jax.ShapeDtypeStruct((rows, width), out_dtype), jax.ShapeDtypeStruct((1, width), F32)],
        compiler_params=_cp("arbitrary"), name=name,
    )(*args)


def _sigmoid(z):
    return 1.0 / (1.0 + jnp.exp(-z))


def _gate_fwd(o_mla, o_sb, proj, g_mla, g_sb, *, zb_mla, zb_sb, tr, name):
    rows, W = o_mla.shape

    def body(om_ref, os_ref, zm_ref, zs_ref, gm_ref, gs_ref, y_ref):
        for half, (o_ref, z_ref, g_ref) in enumerate(((om_ref, zm_ref, gm_ref), (os_ref, zs_ref, gs_ref))):
            o = o_ref[...].astype(F32)
            r = lax.rsqrt(jnp.mean(o * o, axis=-1, keepdims=True) + EPS)
            z = z_ref[...].astype(F32)
            y_ref[:, half * W:(half + 1) * W] = ((o * r * g_ref[...]) * (z * _sigmoid(z))).astype(y_ref.dtype)

    blk = pl.BlockSpec((tr, W), lambda i: (i, 0))
    gsp = pl.BlockSpec((1, W), lambda i: (0, 0))
    return pl.pallas_call(
        body, grid=(rows // tr,),
        in_specs=[blk, blk, pl.BlockSpec((tr, W), lambda i: (i, zb_mla)), pl.BlockSpec((tr, W), lambda i: (i, zb_sb)), gsp, gsp],
        out_specs=pl.BlockSpec((tr, 2 * W), lambda i: (i, 0)),
        out_shape=jax.ShapeDtypeStruct((rows, 2 * W), CD), compiler_params=_cp("parallel"), name=name,
    )(o_mla, o_sb, proj, proj, g_mla, g_sb)


def _gate_bwd(dy, o_mla, o_sb, proj, g_mla, g_sb, *, zb_mla, zb_sb, tr, name):
    rows, W = o_mla.shape

    def body(dy_ref, om_ref, os_ref, zm_ref, zs_ref, gm_ref, gs_ref, dom_ref, dos_ref, dzm_ref, dzs_ref, dgm_ref, dgs_ref):
        @pl.when(pl.program_id(0) == 0)
        def _():
            dgm_ref[...] = jnp.zeros_like(dgm_ref)
            dgs_ref[...] = jnp.zeros_like(dgs_ref)

        halves = ((om_ref, zm_ref, gm_ref, dom_ref, dzm_ref, dgm_ref), (os_ref, zs_ref, gs_ref, dos_ref, dzs_ref, dgs_ref))
        for half, (o_ref, z_ref, g_ref, do_ref, dz_ref, dg_ref) in enumerate(halves):
            dyv = dy_ref[:, half * W:(half + 1) * W].astype(F32)
            o = o_ref[...].astype(F32)
            r = lax.rsqrt(jnp.mean(o * o, axis=-1, keepdims=True) + EPS)
            oh = o * r
            g = g_ref[...]
            z = z_ref[...].astype(F32)
            sg = _sigmoid(z)
            dn = dyv * (z * sg)
            dz_ref[...] = (dyv * (oh * g) * (sg * (1.0 + z * (1.0 - sg)))).astype(dz_ref.dtype)
            dxh = dn * g
            do_ref[...] = (r * (dxh - oh * jnp.mean(dxh * oh, axis=-1, keepdims=True))).astype(do_ref.dtype)
            dg_ref[...] += jnp.sum(dn * oh, axis=0, keepdims=True)

    blk = pl.BlockSpec((tr, W), lambda i: (i, 0))
    gsp = pl.BlockSpec((1, W), lambda i: (0, 0))
    act = jax.ShapeDtypeStruct((rows, W), CD)
    gsh = jax.ShapeDtypeStruct((1, W), F32)
    return pl.pallas_call(
        body, grid=(rows // tr,),
        in_specs=[pl.BlockSpec((tr, 2 * W), lambda i: (i, 0)), blk, blk,
                  pl.BlockSpec((tr, W), lambda i: (i, zb_mla)), pl.BlockSpec((tr, W), lambda i: (i, zb_sb)), gsp, gsp],
        out_specs=[blk, blk, blk, blk, gsp, gsp], out_shape=[act, act, act, act, gsh, gsh],
        compiler_params=_cp("arbitrary"), name=name,
    )(dy, o_mla, o_sb, proj, proj, g_mla, g_sb)


def _rope_fwd(q_raw, proj, cs, sn, *, H, krb, tr, name):
    rows = q_raw.shape[0]
    HW = H * LANES

    def body(q_ref, kr_ref, c_ref, s_ref, qo_ref, kro_ref):
        c, s = c_ref[...], s_ref[...]
        qo_ref[:, :HW] = q_ref[:, :HW].astype(qo_ref.dtype)
        for h in range(H):
            sl = slice(HW + h * LANES, HW + (h + 1) * LANES)
            xh = q_ref[:, sl].astype(F32)
            qo_ref[:, sl] = (xh * c + pltpu.roll(xh, ROPE, 1) * s).astype(qo_ref.dtype)
        xk = kr_ref[...].astype(F32)
        kro_ref[...] = (xk * c + pltpu.roll(xk, ROPE, 1) * s).astype(kro_ref.dtype)

    tab = pl.BlockSpec((tr, LANES), lambda i: (i, 0))
    return pl.pallas_call(
        body, grid=(rows // tr,),
        in_specs=[pl.BlockSpec((tr, 2 * HW), lambda i: (i, 0)), pl.BlockSpec((tr, LANES), lambda i: (i, krb)), tab, tab],
        out_specs=[pl.BlockSpec((tr, 2 * HW), lambda i: (i, 0)), tab],
        out_shape=[jax.ShapeDtypeStruct((rows, 2 * HW), CD), jax.ShapeDtypeStruct((rows, LANES), CD)],
        compiler_params=_cp("parallel"), name=name,
    )(q_raw, proj, cs, sn)


def _rope_bwd(dqn, dqr, dkr, cs, sn, *, H, tr, name):
    rows = dqn.shape[0]
    HW = H * LANES

    def body(dqn_ref, dqr_ref, dkr_ref, c_ref, s_ref, dq_ref, dk_ref):
        c, s = c_ref[...], s_ref[...]
        dq_ref[:, :HW] = dqn_ref[...].astype(dq_ref.dtype)
        for h in range(H):
            d = dqr_ref[:, h * LANES:(h + 1) * LANES].astype(F32)
            dq_ref[:, HW + h * LANES:HW + (h + 1) * LANES] = (d * c + pltpu.roll(d * s, ROPE, 1)).astype(dq_ref.dtype)
        d = dkr_ref[...].astype(F32)
        dk_ref[...] = (d * c + pltpu.roll(d * s, ROPE, 1)).astype(dk_ref.dtype)

    tab = pl.BlockSpec((tr, LANES), lambda i: (i, 0))
    hb = pl.BlockSpec((tr, HW), lambda i: (i, 0))
    return pl.pallas_call(
        body, grid=(rows // tr,), in_specs=[hb, hb, tab, tab, tab],
        out_specs=[pl.BlockSpec((tr, 2 * HW), lambda i: (i, 0)), tab],
        out_shape=[jax.ShapeDtypeStruct((rows, 2 * HW), CD), jax.ShapeDtypeStruct((rows, LANES), CD)],
        compiler_params=_cp("parallel"), name=name,
    )(dqn, dqr, dkr, cs, sn)


def _loss_head(h, g, tgt, *, lo, hi, tr, name):
    rows, D = h.shape

    def body(h_ref, g_ref, t_ref, dh_ref, dg_ref, loss_ref):
        i = pl.program_id(0)

        @pl.when(i == 0)
        def _():
            dg_ref[...] = jnp.zeros_like(dg_ref)
            loss_ref[...] = jnp.zeros_like(loss_ref)

        xv = h_ref[...]
        r = lax.rsqrt(jnp.mean(xv * xv, axis=-1, keepdims=True) + EPS)
        xh = xv * r
        g_ = g_ref[...]
        rowid = i * tr + lax.broadcasted_iota(jnp.int32, (tr, 1), 0)
        valid = jnp.logical_and(rowid >= lo, rowid < hi)
        err = jnp.where(valid, xh * g_ - t_ref[...], 0.0)
        loss_ref[...] += jnp.sum(jnp.sum(err * err, axis=-1, keepdims=True), axis=0, keepdims=True) * (0.5 / D)
        dy = err * (1.0 / D)
        dxh = dy * g_
        dh_ref[...] = r * (dxh - xh * jnp.mean(dxh * xh, axis=-1, keepdims=True))
        dg_ref[...] += jnp.sum(dy * xh, axis=0, keepdims=True)

    blk = pl.BlockSpec((tr, D), lambda i: (i, 0))
    gsp = pl.BlockSpec((1, D), lambda i: (0, 0))
    return pl.pallas_call(
        body, grid=(rows // tr,), in_specs=[blk, gsp, blk],
        out_specs=[blk, gsp, pl.BlockSpec((1, 1), lambda i: (0, 0))],
        out_shape=[jax.ShapeDtypeStruct((rows, D), F32), jax.ShapeDtypeStruct((1, D), F32), jax.ShapeDtypeStruct((1, 1), F32)],
        compiler_params=_cp("arbitrary"), name=name,
    )(h, g, tgt)


def _n_chunks(i, bq, ck):
    return ((i + 1) * bq + ck - 1) // ck


def _mla_fwd(q, kv, kr, *, H, bq, ck, scale, name):
    rows = q.shape[0]

    def body(qn_ref, qr_ref, kn_ref, v_ref, kr_ref, o_ref, lse_ref):
        i = pl.program_id(1)
        qn, qr = qn_ref[...], qr_ref[...]
        row = i * bq + lax.broadcasted_iota(jnp.int32, (bq, ck), 0)
        col = lax.broadcasted_iota(jnp.int32, (bq, ck), 1)

        def step(j, carry):
            m, l, acc = carry
            off = pl.multiple_of(j * ck, ck)
            s = (_dot_nt(qn, kn_ref[pl.ds(off, ck), :]) + _dot_nt(qr, kr_ref[pl.ds(off, ck), :])) * scale
            s = jnp.where(col + j * ck <= row, s, NEG)
            m_new = jnp.maximum(m, jnp.max(s, axis=-1, keepdims=True))
            p = jnp.exp(s - m_new)
            alpha = jnp.exp(m - m_new)
            l = alpha * l + jnp.sum(p, axis=-1, keepdims=True)
            acc = alpha * acc + _dot(p.astype(CD), v_ref[pl.ds(off, ck), :])
            return m_new, l, acc

        init = (jnp.full((bq, 1), NEG, F32), jnp.zeros((bq, 1), F32), jnp.zeros((bq, LANES), F32))
        m, l, acc = lax.fori_loop(0, _n_chunks(i, bq, ck), step, init)
        o_ref[...] = (acc / l).astype(o_ref.dtype)
        lse_ref[0] = m + jnp.log(l)

    qb = lambda off: pl.BlockSpec((bq, LANES), lambda h, i: (i, off + h))
    kb = lambda off: pl.BlockSpec((rows, LANES), lambda h, i: (0, off + h))
    return pl.pallas_call(
        body, grid=(H, rows // bq),
        in_specs=[qb(0), qb(H), kb(0), kb(H), pl.BlockSpec((rows, LANES), lambda h, i: (0, 0))],
        out_specs=[qb(0), pl.BlockSpec((1, bq, 1), lambda h, i: (h, i, 0))],
        out_shape=[jax.ShapeDtypeStruct((rows, H * LANES), CD), jax.ShapeDtypeStruct((H, rows, 1), F32)],
        compiler_params=_cp("arbitrary", "arbitrary"), name=name,
    )(q, q, kv, kv, kr)


def _mla_bwd(q, kv, kr, o, do, lse, *, H, bq, ck, scale, name):
    rows = q.shape[0]

    def body(qn_ref, qr_ref, kn_ref, v_ref, kr_ref, o_ref, do_ref, lse_ref, dqn_ref, dqr_ref, dkn_ref, dv_ref, dkr_ref):
        h, i = pl.program_id(0), pl.program_id(1)

        @pl.when(i == 0)
        def _():
            dkn_ref[...] = jnp.zeros_like(dkn_ref)
            dv_ref[...] = jnp.zeros_like(dv_ref)

        @pl.when(jnp.logical_and(i == 0, h == 0))
        def _():
            dkr_ref[...] = jnp.zeros_like(dkr_ref)

        qn, qr, do_ = qn_ref[...], qr_ref[...], do_ref[...]
        delta = jnp.sum(do_.astype(F32) * o_ref[...].astype(F32), axis=-1, keepdims=True)
        lse_ = lse_ref[0]
        row = i * bq + lax.broadcasted_iota(jnp.int32, (bq, ck), 0)
        col = lax.broadcasted_iota(jnp.int32, (bq, ck), 1)

        def step(j, carry):
            dqn, dqr = carry
            off = pl.multiple_of(j * ck, ck)
            kn, krj, vj = kn_ref[pl.ds(off, ck), :], kr_ref[pl.ds(off, ck), :], v_ref[pl.ds(off, ck), :]
            s = (_dot_nt(qn, kn) + _dot_nt(qr, krj)) * scale
            p = jnp.where(col + j * ck <= row, jnp.exp(s - lse_), 0.0)
            ds = (p * (_dot_nt(do_, vj) - delta) * scale).astype(CD)
            dkn_ref[pl.ds(off, ck), :] += _dot_tn(ds, qn)
            dkr_ref[pl.ds(off, ck), :] += _dot_tn(ds, qr)
            dv_ref[pl.ds(off, ck), :] += _dot_tn(p.astype(CD), do_)
            return dqn + _dot(ds, kn), dqr + _dot(ds, krj)

        init = (jnp.zeros((bq, LANES), F32), jnp.zeros((bq, LANES), F32))
        dqn, dqr = lax.fori_loop(0, _n_chunks(i, bq, ck), step, init)
        dqn_ref[...] = dqn.astype(dqn_ref.dtype)
        dqr_ref[...] = dqr.astype(dqr_ref.dtype)

    qb = lambda off: pl.BlockSpec((bq, LANES), lambda h, i: (i, off + h))
    kb = lambda off: pl.BlockSpec((rows, LANES), lambda h, i: (0, off + h))
    whole = pl.BlockSpec((rows, LANES), lambda h, i: (0, 0))
    act = jax.ShapeDtypeStruct((rows, H * LANES), CD)
    acc = jax.ShapeDtypeStruct((rows, H * LANES), F32)
    return pl.pallas_call(
        body, grid=(H, rows // bq),
        in_specs=[qb(0), qb(H), kb(0), kb(H), whole, qb(0), qb(0), pl.BlockSpec((1, bq, 1), lambda h, i: (h, i, 0))],
        out_specs=[qb(0), qb(0), kb(0), kb(0), whole],
        out_shape=[act, act, acc, acc, jax.ShapeDtypeStruct((rows, LANES), F32)],
        compiler_params=_cp("arbitrary", "arbitrary"), name=name,
    )(q, q, kv, kv, kr, o, do, lse)


def _log_sigmoids(z):
    lb = jnp.minimum(z, 0.0) - jnp.log(1.0 + jnp.exp(-jnp.abs(z)))
    return lb, lb - z


def _split(x):
    hi = x.astype(CD)
    lo = (x - hi.astype(F32)).astype(CD)
    return hi, lo, hi.astype(F32) + lo.astype(F32)


def _tri(ck, rel):
    j, s = np.arange(ck)[:, None], np.arange(ck)[None, :]
    return jnp.asarray({"gt": j > s, "le": j <= s, "lt": j < s}[rel], CD)


def _sb_fwd(proj, *, H, qb0, kb0, vb0, bq, ck, scale, name):
    rows = proj.shape[0]

    def body(q_ref, k_ref, v_ref, u_ref, o_ref, t_ref):
        i = pl.program_id(1)
        q_, u = q_ref[...], u_ref[...]
        row = i * bq + lax.broadcasted_iota(jnp.int32, (bq, ck), 0)
        col = lax.broadcasted_iota(jnp.int32, (bq, ck), 1)
        n = _n_chunks(i, bq, ck)

        def step(jj, carry):
            c, acc = carry
            j = n - 1 - jj
            off = pl.multiple_of(j * ck, ck)
            z = _dot_nt(q_, k_ref[pl.ds(off, ck), :]) * scale
            lb, lom = _log_sigmoids(z)
            strict = col + j * ck < row
            hi, lo, lom_r = _split(jnp.where(strict, lom, 0.0))
            a = jnp.where(strict, jnp.exp(lb + (_dot(hi, u) + _dot(lo, u)) + c), 0.0)
            acc = acc + _dot(a.astype(CD), v_ref[pl.ds(off, ck), :])
            return c + jnp.sum(lom_r, axis=-1, keepdims=True), acc

        c, acc = lax.fori_loop(0, n, step, (jnp.zeros((bq, 1), F32), jnp.zeros((bq, LANES), F32)))
        o_ref[...] = acc.astype(o_ref.dtype)
        t_ref[0] = c

    kb = lambda off: pl.BlockSpec((rows, LANES), lambda h, i: (0, off + h))
    return pl.pallas_call(
        body, grid=(H, rows // bq),
        in_specs=[pl.BlockSpec((bq, LANES), lambda h, i: (i, qb0 + h)), kb(kb0), kb(vb0), pl.BlockSpec((ck, ck), lambda h, i: (0, 0))],
        out_specs=[pl.BlockSpec((bq, LANES), lambda h, i: (i, h)), pl.BlockSpec((1, bq, 1), lambda h, i: (h, i, 0))],
        out_shape=[jax.ShapeDtypeStruct((rows, H * LANES), CD), jax.ShapeDtypeStruct((H, rows, 1), F32)],
        compiler_params=_cp("arbitrary", "arbitrary"), name=name,
    )(proj, proj, proj, _tri(ck, "gt"))


def _sb_bwd(proj, do, tot, *, H, qb0, kb0, vb0, bq, ck, scale, name):
    rows = proj.shape[0]

    def body(q_ref, k_ref, v_ref, do_ref, t_ref, ule_ref, ult_ref, dq_ref, dk_ref, dv_ref):
        i = pl.program_id(1)

        @pl.when(i == 0)
        def _():
            dk_ref[...] = jnp.zeros_like(dk_ref)
            dv_ref[...] = jnp.zeros_like(dv_ref)

        q_, do_, tot_, ule, ult = q_ref[...], do_ref[...], t_ref[0], ule_ref[...], ult_ref[...]
        row = i * bq + lax.broadcasted_iota(jnp.int32, (bq, ck), 0)
        col = lax.broadcasted_iota(jnp.int32, (bq, ck), 1)

        def step(j, carry):
            pc, gc, dq = carry
            off = pl.multiple_of(j * ck, ck)
            kj, vj = k_ref[pl.ds(off, ck), :], v_ref[pl.ds(off, ck), :]
            z = _dot_nt(q_, kj) * scale
            lb, lom = _log_sigmoids(z)
            strict = col + j * ck < row
            hi, lo, lom_r = _split(jnp.where(strict, lom, 0.0))
            rem = tot_ - (pc + (_dot(hi, ule) + _dot(lo, ule)))
            a = jnp.where(strict, jnp.exp(lb + rem), 0.0)
            g = a * _dot_nt(do_, vj)
            big_g = gc + _dot(g.astype(CD), ult)
            sg = jnp.exp(lb)
            dz = (jnp.where(strict, g * (1.0 - sg) - big_g * sg, 0.0) * scale).astype(CD)
            dk_ref[pl.ds(off, ck), :] += _dot_tn(dz, q_)
            dv_ref[pl.ds(off, ck), :] += _dot_tn(a.astype(CD), do_)
            return pc + jnp.sum(lom_r, axis=-1, keepdims=True), gc + jnp.sum(g, axis=-1, keepdims=True), dq + _dot(dz, kj)

        init = (jnp.zeros((bq, 1), F32), jnp.zeros((bq, 1), F32), jnp.zeros((bq, LANES), F32))
        _, _, dq = lax.fori_loop(0, _n_chunks(i, bq, ck), step, init)
        dq_ref[...] = dq.astype(dq_ref.dtype)

    kb = lambda off: pl.BlockSpec((rows, LANES), lambda h, i: (0, off + h))
    ob = pl.BlockSpec((bq, LANES), lambda h, i: (i, h))
    tri = pl.BlockSpec((ck, ck), lambda h, i: (0, 0))
    acc = jax.ShapeDtypeStruct((rows, H * LANES), F32)
    return pl.pallas_call(
        body, grid=(H, rows // bq),
        in_specs=[pl.BlockSpec((bq, LANES), lambda h, i: (i, qb0 + h)), kb(kb0), kb(vb0), ob,
                  pl.BlockSpec((1, bq, 1), lambda h, i: (h, i, 0)), tri, tri],
        out_specs=[ob, kb(0), kb(0)],
        out_shape=[jax.ShapeDtypeStruct((rows, H * LANES), CD), acc, acc],
        compiler_params=_cp("arbitrary", "arbitrary"), name=name,
    )(proj, proj, proj, do, tot, _tri(ck, "le"), _tri(ck, "lt"))


def _dev_index(p):
    return 4 * p[0] + 2 * p[1] + p[2]


def _all_gather(arrays, *, name):
    A = len(arrays)

    def body(*refs):
        ins, outs = refs[:A], refs[A:2 * A]
        send_sems, recv_sems, local_sems = refs[2 * A:]
        x, y, c = lax.axis_index("x"), lax.axis_index("y"), lax.axis_index("c")
        me, sibling = (x, y, c), (x, y, 1 - c)
        chips = [(1 - x, y), (x, 1 - y), (1 - x, 1 - y)]

        def copy(a, k, block, to, src=None):
            dst = outs[a].at[_dev_index(block)]
            return pltpu.make_async_remote_copy(
                src_ref=dst if src is None else src, dst_ref=dst, send_sem=send_sems.at[a * 7 + k],
                recv_sem=recv_sems.at[a * 7 + k], device_id=to, device_id_type=MESH)

        mine = [pltpu.make_async_copy(ins[a], outs[a].at[_dev_index(me)], local_sems.at[a]) for a in range(A)]
        for cp in mine:
            cp.start()
        first = []
        for a in range(A):
            first.append(copy(a, 0, me, sibling, src=ins[a]))
            first += [copy(a, 1 + j, me, (*chip, c), src=ins[a]) for j, chip in enumerate(chips)]
        for cp in first:
            cp.start()
        passed = []
        for j, chip in enumerate(chips):
            for a in range(A):
                copy(a, 1 + j, (*chip, c), me).wait_recv()
                fwd = copy(a, 4 + j, (*chip, c), sibling)
                fwd.start()
                passed.append(fwd)
        for a in range(A):
            copy(a, 0, sibling, me).wait_recv()
            for j, chip in enumerate(chips):
                copy(a, 4 + j, (*chip, 1 - c), me).wait_recv()
        for cp in first + passed:
            cp.wait_send()
        for cp in mine:
            cp.wait()

    any_spec = pl.BlockSpec(memory_space=pl.ANY)
    return pl.pallas_call(
        body, in_specs=[any_spec] * A, out_specs=[any_spec] * A,
        out_shape=[jax.ShapeDtypeStruct((N_DEV,) + a.shape, a.dtype) for a in arrays],
        scratch_shapes=[pltpu.SemaphoreType.DMA((7 * A,)), pltpu.SemaphoreType.DMA((7 * A,)), pltpu.SemaphoreType.DMA((A,))],
        name=name,
    )(*arrays)


def _all_to_all(arrays, *, name):
    A = len(arrays)

    def body(*refs):
        ins, outs = refs[:A], refs[A:2 * A]
        send_sems, recv_sems, local_sems = refs[2 * A:]
        x, y, c = lax.axis_index("x"), lax.axis_index("y"), lax.axis_index("c")
        me_i = _dev_index((x, y, c))
        peers = [((1 - x) if r & 4 else x, (1 - y) if r & 2 else y, (1 - c) if r & 1 else c) for r in range(1, N_DEV)]
        order = [0, 1, 3, 2, 4, 5, 6]

        def copy(a, k):
            peer = peers[k]
            return pltpu.make_async_remote_copy(
                src_ref=ins[a].at[_dev_index(peer)], dst_ref=outs[a].at[me_i], send_sem=send_sems.at[a * 7 + k],
                recv_sem=recv_sems.at[a * 7 + k], device_id=peer, device_id_type=MESH)

        def arrival(a, k):
            peer_i = _dev_index(peers[k])
            return pltpu.make_async_remote_copy(
                src_ref=ins[a].at[peer_i], dst_ref=outs[a].at[peer_i], send_sem=send_sems.at[a * 7 + k],
                recv_sem=recv_sems.at[a * 7 + k], device_id=peers[k], device_id_type=MESH)

        mine = [pltpu.make_async_copy(ins[a].at[me_i], outs[a].at[me_i], local_sems.at[a]) for a in range(A)]
        for cp in mine:
            cp.start()
        sent = [copy(a, k) for k in order for a in range(A)]
        for cp in sent:
            cp.start()
        for k in order:
            for a in range(A):
                arrival(a, k).wait_recv()
        for cp in sent:
            cp.wait_send()
        for cp in mine:
            cp.wait()

    any_spec = pl.BlockSpec(memory_space=pl.ANY)
    return pl.pallas_call(
        body, in_specs=[any_spec] * A, out_specs=[any_spec] * A,
        out_shape=[jax.ShapeDtypeStruct(a.shape, a.dtype) for a in arrays],
        scratch_shapes=[pltpu.SemaphoreType.DMA((7 * A,)), pltpu.SemaphoreType.DMA((7 * A,)), pltpu.SemaphoreType.DMA((A,))],
        name=name,
    )(*arrays)


def _adamw(g, w, m, v):
    m = ADAM_B1 * m + (1.0 - ADAM_B1) * g
    v = ADAM_B2 * v + (1.0 - ADAM_B2) * (g * g)
    delta = -ADAM_LR * ((m / ADAM_BC1) / (jnp.sqrt(v / ADAM_BC2) + ADAM_EPS) + ADAM_WD * w)
    return delta, m, v


def _adam_pieces(pieces, w, m, v, *, name):
    R, C = w.shape
    tr = _pick(R, [t for t in (512, 256, 128, 64, 32, 16) if t * C * 4 <= 2**20])

    def body(p_ref, w_ref, m_ref, v_ref, g_ref, d_ref, mo_ref, vo_ref):
        g = p_ref[0].astype(F32)
        for k in range(1, N_DEV):
            g = g + p_ref[k].astype(F32)
        g_ref[...] = g
        d_ref[...], mo_ref[...], vo_ref[...] = _adamw(g, w_ref[...], m_ref[...], v_ref[...])

    blk = pl.BlockSpec((tr, C), lambda i: (i, 0))
    sh = jax.ShapeDtypeStruct((R, C), F32)
    return pl.pallas_call(
        body, grid=(R // tr,), in_specs=[pl.BlockSpec((N_DEV, tr, C), lambda i: (0, i, 0)), blk, blk, blk],
        out_specs=[blk] * 4, out_shape=[sh] * 4, compiler_params=_cp("parallel"), name=name,
    )(pieces, w, m, v)


def _adam_direct(g, w, m, v, *, name):
    sh = jax.ShapeDtypeStruct(w.shape, F32)

    def body(g_ref, w_ref, m_ref, v_ref, d_ref, mo_ref, vo_ref):
        d_ref[...], mo_ref[...], vo_ref[...] = _adamw(g_ref[...], w_ref[...], m_ref[...], v_ref[...])

    return pl.pallas_call(body, out_shape=[sh] * 3, compiler_params=_cp(), name=name)(g, w, m, v)


def _sum_devices(parts, *, name):
    def body(p_ref, o_ref):
        s = p_ref[0]
        for k in range(1, N_DEV):
            s = s + p_ref[k]
        o_ref[...] = s

    return pl.pallas_call(body, out_shape=jax.ShapeDtypeStruct(parts.shape[1:], F32), compiler_params=_cp(), name=name)(parts)


def _rope_cols_pad(w):
    z = jnp.zeros(w.shape[:-1] + (ROPE // 2,), w.dtype)
    return jnp.concatenate([w[..., :ROPE // 2], z, w[..., ROPE // 2:], z], axis=-1)


def _rope_cols_unpad(w):
    return jnp.concatenate([w[..., :ROPE // 2], w[..., ROPE:ROPE + ROPE // 2]], axis=-1)


class _Dims:
    def __init__(self, D, S, NM, QL, KVL, WM, WS):
        self.D, self.S, self.NM, self.QL, self.KVL, self.WM, self.WS = D, S, NM, QL, KVL, WM, WS
        self.H = WM // LANES
        self.HW = self.H * LANES
        self.L = NM + S
        self.bq = 384 if self.L > 1024 else 128
        self.Lp = -(-self.L // self.bq) * self.bq
        self.ck_mla = self.bq
        self.ck_sb = LANES
        self.tr = 128
        self.sizes = (QL, KVL, ROPE, WM, WS, WS, WS, WS)
        self.d_in = sum(self.sizes)
        front = QL + KVL + LANES
        self.front = -(-front // WM) * WM
        self.pad = self.front - front
        self.dn = self.front + WM + 4 * WS
        assert WM == WS and QL % KVL == 0 and KVL % LANES == 0 and self.d_in % N_DEV == 0 and D % N_DEV == 0
        self.mla_scale = 1.0 / math.sqrt(LANES + ROPE)
        self.sb_scale = 1.0 / math.sqrt(LANES)


def _w_in_nice(d, w):
    parts = jnp.split(w, np.cumsum(d.sizes)[:-1].tolist(), axis=1)
    return jnp.concatenate([parts[0], parts[1], _rope_cols_pad(parts[2]), jnp.zeros((w.shape[0], d.pad), w.dtype)] + parts[3:], axis=1)


def _w_in_orig(d, w):
    a, b = d.QL + d.KVL, d.front
    return jnp.concatenate([w[:, :a], _rope_cols_unpad(w[:, a:a + LANES]), w[:, b:]], axis=1)


def _w_uq_nice(d, w):
    w3 = w.reshape(d.QL, d.H, LANES + ROPE)
    return jnp.concatenate([w3[..., :LANES].reshape(d.QL, d.HW), _rope_cols_pad(w3[..., LANES:]).reshape(d.QL, d.HW)], axis=1)


def _w_uq_orig(d, w):
    nope = w[:, :d.HW].reshape(d.QL, d.H, LANES)
    rope = _rope_cols_unpad(w[:, d.HW:].reshape(d.QL, d.H, LANES))
    return jnp.concatenate([nope, rope], axis=-1).reshape(d.QL, d.H * (LANES + ROPE))


def _w_ukv_nice(d, w):
    return w.reshape(d.KVL, d.H, 2, LANES).transpose(0, 2, 1, 3).reshape(d.KVL, 2 * d.HW)


def _w_ukv_orig(d, w):
    return w.reshape(d.KVL, 2, d.H, LANES).transpose(0, 2, 1, 3).reshape(d.KVL, 2 * d.HW)


def _cols_from_shards(g):
    return g.transpose(1, 0, 2).reshape(g.shape[1], -1)


def _cols_to_shards(w):
    return w.reshape(w.shape[0], N_DEV, -1).transpose(1, 0, 2)


def _rope_tables(rows):
    inv_freq = ROPE_THETA ** (-jnp.arange(0, ROPE, 2, dtype=F32) / ROPE)
    ang = jnp.arange(rows, dtype=jnp.int32).astype(F32)[:, None] * inv_freq[None, :]
    cos, sin, z = jnp.cos(ang), jnp.sin(ang), jnp.zeros_like(ang)
    return jnp.concatenate([cos, z, cos, z], axis=1), jnp.concatenate([-sin, z, sin, z], axis=1)


def _flat_pack(vecs):
    flat = jnp.concatenate([v.reshape(-1).astype(F32) for v in vecs])
    n = flat.shape[0]
    rows = -(-n // (8 * 1024)) * 8
    offs = np.cumsum([0] + [int(np.prod(v.shape)) for v in vecs])
    return jnp.pad(flat, (0, rows * 1024 - n)).reshape(rows, 1024), offs


def kernel(x, meta_tokens, g_norm, w_in, g_q, g_kv, w_uq, w_ukv, g_out_mla, g_out_sb, w_o, g_final, loss_target, m_meta_tokens, m_g_norm, m_w_in, m_g_q, m_g_kv, m_w_uq, m_w_ukv, m_g_out_mla, m_g_out_sb, m_w_o, m_g_final, v_meta_tokens, v_g_norm, v_w_in, v_g_q, v_g_kv, v_w_uq, v_w_ukv, v_g_out_mla, v_g_out_sb, v_w_o, v_g_final):
    depth = g_norm.shape[0]
    d = _Dims(D=x.shape[-1], S=x.shape[1], NM=meta_tokens.shape[0], QL=g_q.shape[1], KVL=g_kv.shape[1],
              WM=g_out_mla.shape[1], WS=g_out_sb.shape[1])
    D, Lp, H, tr = d.D, d.Lp, d.H, d.tr
    my = _dev_index((lax.axis_index("x"), lax.axis_index("y"), lax.axis_index("c")))

    gw_in, gw_uq, gw_ukv, gw_o, g_meta = _all_gather(
        [w_in.astype(CD), w_uq.astype(CD), w_ukv.astype(CD), w_o.astype(CD), meta_tokens], name="gather_weights")
    wn = [_w_in_nice(d, _cols_from_shards(gw_in[:, i])) for i in range(depth)]
    wuq = [_w_uq_nice(d, _cols_from_shards(gw_uq[:, i])) for i in range(depth)]
    wukv = [_w_ukv_nice(d, _cols_from_shards(gw_ukv[:, i])) for i in range(depth)]
    wo = [gw_o[:, i].reshape(D, D) for i in range(depth)]
    meta_full = _cols_from_shards(g_meta)

    cs, sn = _rope_tables(Lp)
    h = jnp.concatenate([meta_full, x[0], jnp.zeros((Lp - d.L, D), F32)], axis=0)
    tgt = jnp.concatenate([jnp.zeros((d.NM, D), F32), loss_target[0], jnp.zeros((Lp - d.L, D), F32)], axis=0)

    b128 = lambda col: col // LANES
    krb = b128(d.QL + d.KVL)
    zb_mla, zb_sb = d.front // d.WM, (d.front + d.WM + 3 * d.WS) // d.WS
    qb0, kb0, vb0 = b128(d.front + d.WM), b128(d.front + d.WM + d.WS), b128(d.front + d.WM + 2 * d.WS)
    row = lambda a, i: a[i][None, :]

    saved = []
    for i in range(depth):
        u = _rms_fwd(h, row(g_norm, i), cb=0, width=D, tr=tr, name=f"rms_in_{i}")
        proj = _mm(u, wn[i], out_dtype=CD, name=f"proj_{i}")
        cqn = _rms_fwd(proj, row(g_q, i), cb=0, width=d.QL, tr=tr, name=f"rms_q_{i}")
        ckvn = _rms_fwd(proj, row(g_kv, i), cb=d.QL // d.KVL, width=d.KVL, tr=tr, name=f"rms_kv_{i}")
        q_raw = _mm(cqn, wuq[i], out_dtype=F32, name=f"up_q_{i}")
        kvn = _mm(ckvn, wukv[i], out_dtype=CD, name=f"up_kv_{i}")
        q, kr = _rope_fwd(q_raw, proj, cs, sn, H=H, krb=krb, tr=tr, name=f"rope_{i}")
        o_mla, lse = _mla_fwd(q, kvn, kr, H=H, bq=d.bq, ck=d.ck_mla, scale=d.mla_scale, name=f"mla_fwd_{i}")
        o_sb, tot = _sb_fwd(proj, H=H, qb0=qb0, kb0=kb0, vb0=vb0, bq=d.bq, ck=d.ck_sb, scale=d.sb_scale, name=f"sb_fwd_{i}")
        y = _gate_fwd(o_mla, o_sb, proj, row(g_out_mla, i), row(g_out_sb, i), zb_mla=zb_mla, zb_sb=zb_sb, tr=tr, name=f"gate_fwd_{i}")
        h_next = _mm(y, wo[i], out_dtype=F32, res=h, name=f"out_proj_{i}")
        saved.append((h, u, proj, cqn, ckvn, q, kvn, kr, o_mla, lse, o_sb, tot, y))
        h = h_next

    dh, dg_final, loss_part = _loss_head(h, g_final[None, :], tgt, lo=d.NM, hi=d.L, tr=tr, name="loss_head")

    dg_norm, dg_q, dg_kv, dg_om, dg_os = [None] * depth, [None] * depth, [None] * depth, [None] * depth, [None] * depth
    p_in, p_uq, p_ukv, p_o = [None] * depth, [None] * depth, [None] * depth, [None] * depth
    zpad = jnp.zeros((Lp, d.pad), CD)
    for i in reversed(range(depth)):
        h_in, u, proj, cqn, ckvn, q, kvn, kr, o_mla, lse, o_sb, tot, y = saved[i]
        dy = _mm(dh, wo[i], tb=True, out_dtype=CD, name=f"d_y_{i}")
        dwo = _mm(y, dh, ta=True, out_dtype=F32, name=f"d_wo_{i}")
        do_mla, do_sb, dz_mla, dz_sb, dg_om[i], dg_os[i] = _gate_bwd(
            dy, o_mla, o_sb, proj, row(g_out_mla, i), row(g_out_sb, i), zb_mla=zb_mla, zb_sb=zb_sb, tr=tr, name=f"gate_bwd_{i}")
        dqn, dqr, dkn, dv, dkr = _mla_bwd(q, kvn, kr, o_mla, do_mla, lse, H=H, bq=d.bq, ck=d.ck_mla, scale=d.mla_scale, name=f"mla_bwd_{i}")
        dq_sb, dk_sb, dv_sb = _sb_bwd(proj, do_sb, tot, H=H, qb0=qb0, kb0=kb0, vb0=vb0, bq=d.bq, ck=d.ck_sb, scale=d.sb_scale, name=f"sb_bwd_{i}")
        dq, dk_r = _rope_bwd(dqn, dqr, dkr, cs, sn, H=H, tr=tr, name=f"rope_bwd_{i}")
        dcqn = _mm(dq, wuq[i], tb=True, out_dtype=CD, name=f"d_cq_{i}")
        dwuq = _mm(cqn, dq, ta=True, out_dtype=F32, name=f"d_wuq_{i}")
        dkv = jnp.concatenate([dkn, dv], axis=1).astype(CD)
        dckvn = _mm(dkv, wukv[i], tb=True, out_dtype=CD, name=f"d_ckv_{i}")
        dwukv = _mm(ckvn, dkv, ta=True, out_dtype=F32, name=f"d_wukv_{i}")
        dc_q, dg_q[i] = _rms_bwd(proj, row(g_q, i), dcqn, cb=0, width=d.QL, tr=tr, out_dtype=CD, name=f"rms_q_bwd_{i}")
        dc_kv, dg_kv[i] = _rms_bwd(proj, row(g_kv, i), dckvn, cb=d.QL // d.KVL, width=d.KVL, tr=tr, out_dtype=CD, name=f"rms_kv_bwd_{i}")
        dproj = jnp.concatenate([dc_q, dc_kv, dk_r, zpad, dz_mla, dq_sb, dk_sb.astype(CD), dv_sb.astype(CD), dz_sb], axis=1)
        du = _mm(dproj, wn[i], tb=True, out_dtype=CD, name=f"d_u_{i}")
        dwn = _mm(u, dproj, ta=True, out_dtype=F32, name=f"d_wn_{i}")
        dh, dg_norm[i] = _rms_bwd(h_in, row(g_norm, i), du, cb=0, width=D, tr=tr, res=dh, out_dtype=F32, name=f"rms_in_bwd_{i}")
        p_in[i] = _cols_to_shards(_w_in_orig(d, dwn)).astype(CD)
        p_uq[i] = _cols_to_shards(_w_uq_orig(d, dwuq)).astype(CD)
        p_ukv[i] = _cols_to_shards(_w_ukv_orig(d, dwukv)).astype(CD)
        p_o[i] = dwo.reshape(N_DEV, D // N_DEV, D).astype(CD)

    r_in, r_uq, r_ukv, r_o = _all_to_all(
        [jnp.stack(p_in, axis=1), jnp.stack(p_uq, axis=1), jnp.stack(p_ukv, axis=1), jnp.stack(p_o, axis=1)], name="exchange_grads")
    cat = lambda parts: jnp.concatenate(parts, axis=0)
    small, offs = _flat_pack([cat(dg_norm), cat(dg_q), cat(dg_kv), cat(dg_om), cat(dg_os), dg_final, dh[:d.NM], loss_part])
    (small_all,) = _all_gather([small], name="gather_small")
    small_sum = _sum_devices(small_all, name="sum_small").reshape(-1)
    seg = lambda k, shape: small_sum[offs[k]:offs[k + 1]].reshape(shape)
    loss = small_sum[offs[7]]
    grad_x = dh[d.NM:d.L][None]

    def big(pieces, w, m, v, name):
        R = w.shape[0] * w.shape[1]
        outs = _adam_pieces(pieces.reshape(N_DEV, R, w.shape[2]), w.reshape(R, -1), m.reshape(R, -1), v.reshape(R, -1), name=name)
        return [o.reshape(w.shape) for o in outs]

    res = {
        "w_in": big(r_in, w_in, m_w_in, v_w_in, "adam_w_in"),
        "w_uq": big(r_uq, w_uq, m_w_uq, v_w_uq, "adam_w_uq"),
        "w_ukv": big(r_ukv, w_ukv, m_w_ukv, v_w_ukv, "adam_w_ukv"),
        "w_o": big(r_o, w_o, m_w_o, v_w_o, "adam_w_o"),
    }
    g_meta_full = seg(6, (d.NM, D))
    ncol = meta_tokens.shape[1]
    g_meta_mine = lax.dynamic_slice(g_meta_full, (0, my * ncol), (d.NM, ncol))
    res["meta_tokens"] = [g_meta_mine] + list(_adam_direct(g_meta_mine, meta_tokens, m_meta_tokens, v_meta_tokens, name="adam_meta"))

    small_names = ["g_norm", "g_q", "g_kv", "g_out_mla", "g_out_sb", "g_final"]
    small_w = [g_norm, g_q, g_kv, g_out_mla, g_out_sb, g_final]
    small_m = [m_g_norm, m_g_q, m_g_kv, m_g_out_mla, m_g_out_sb, m_g_final]
    small_v = [v_g_norm, v_g_q, v_g_kv, v_g_out_mla, v_g_out_sb, v_g_final]
    n_small = int(offs[6])
    rows_s = -(-n_small // (8 * 1024)) * 8
    g_flat = small_sum[:rows_s * 1024].reshape(rows_s, 1024)
    wf, _ = _flat_pack(small_w)
    mf, _ = _flat_pack(small_m)
    vf, _ = _flat_pack(small_v)
    d_f, m_f, v_f = _adam_direct(g_flat, wf, mf, vf, name="adam_small")
    for k, nm in enumerate(small_names):
        sl = lambda f: f.reshape(-1)[offs[k]:offs[k + 1]].reshape(small_w[k].shape)
        res[nm] = [sl(g_flat), sl(d_f), sl(m_f), sl(v_f)]

    names = ["meta_tokens", "g_norm", "w_in", "g_q", "g_kv", "w_uq", "w_ukv", "g_out_mla", "g_out_sb", "w_o", "g_final"]
    return (loss, grad_x, *[res[n][0] for n in names], *[res[n][1] for n in names],
            *[res[n][2] for n in names], *[res[n][3] for n in names])
```

```python
import math

import numpy as np
import jax
import jax.numpy as jnp
from jax import lax
from jax.experimental import pallas as pl
from jax.experimental.pallas import tpu as pltpu

CD = jnp.bfloat16
F32 = jnp.float32

LANES = 128
N_DEV = 8
EPS = 1e-6
ROPE = 64
ROPE_THETA = 10000.0
NEG = -1e30
VMEM_LIMIT = 48 * 2**20

ADAM_LR, ADAM_B1, ADAM_B2, ADAM_EPS, ADAM_WD, ADAM_STEP = 0.001, 0.9, 0.999, 1e-08, 0.01, 10
ADAM_BC1 = 1.0 - ADAM_B1**ADAM_STEP
ADAM_BC2 = 1.0 - ADAM_B2**ADAM_STEP

MESH = pl.DeviceIdType.MESH


def _cp(*sem):
    return pltpu.CompilerParams(dimension_semantics=tuple(sem) if sem else None, vmem_limit_bytes=VMEM_LIMIT)


def _pick(dim, prefs):
    for p in prefs:
        if dim % p == 0:
            return p
    raise ValueError(f"no tile for {dim} in {prefs}")


def _dot(a, b):
    return jnp.dot(a, b, preferred_element_type=F32)


def _dot_nt(a, b):
    return lax.dot_general(a, b, (((1,), (1,)), ((), ())), preferred_element_type=F32)


def _dot_tn(a, b):
    return lax.dot_general(a, b, (((0,), (0,)), ((), ())), preferred_element_type=F32)


def _mm(a, b, *, ta=False, tb=False, out_dtype, res=None, name):
    M, K = (a.shape[1], a.shape[0]) if ta else a.shape
    N = b.shape[0] if tb else b.shape[1]
    tm = _pick(M, (1056, 1024, 512, 256, 128))
    tn = _pick(N, (1024, 512, 384, 256, 128))
    tk = _pick(K, (1024, 512, 384, 256, 128))
    nk = K // tk
    a_dims = (((0,), (1 if tb else 0,)), ((), ())) if ta else (((1,), (1 if tb else 0,)), ((), ()))

    def body(*refs):
        if res is None:
            a_ref, b_ref, o_ref, acc = refs
        else:
            a_ref, b_ref, r_ref, o_ref, acc = refs
        k = pl.program_id(2)

        @pl.when(k == 0)
        def _():
            acc[...] = jnp.zeros_like(acc)

        acc[...] += lax.dot_general(a_ref[...].astype(CD), b_ref[...].astype(CD), a_dims, preferred_element_type=F32)

        @pl.when(k == nk - 1)
        def _():
            r = acc[...]
            if res is not None:
                r = r + r_ref[...]
            o_ref[...] = r.astype(out_dtype)

    a_spec = pl.BlockSpec((tk, tm), lambda i, j, k: (k, i)) if ta else pl.BlockSpec((tm, tk), lambda i, j, k: (i, k))
    b_spec = pl.BlockSpec((tn, tk), lambda i, j, k: (j, k)) if tb else pl.BlockSpec((tk, tn), lambda i, j, k: (k, j))
    o_spec = pl.BlockSpec((tm, tn), lambda i, j, k: (i, j))
    in_specs, args = [a_spec, b_spec], [a, b]
    if res is not None:
        in_specs.append(o_spec)
        args.append(res)
    return pl.pallas_call(
        body, grid=(M // tm, N // tn, nk), in_specs=in_specs, out_specs=o_spec,
        out_shape=jax.ShapeDtypeStruct((M, N), out_dtype), scratch_shapes=[pltpu.VMEM((tm, tn), F32)],
        compiler_params=_cp("parallel", "parallel", "arbitrary"), name=name,
    )(*args)


def _rms_fwd(x, g, *, cb, width, tr, name):
    rows = x.shape[0]

    def body(x_ref, g_ref, o_ref):
        xv = x_ref[...].astype(F32)
        r = lax.rsqrt(jnp.mean(xv * xv, axis=-1, keepdims=True) + EPS)
        o_ref[...] = (xv * r * g_ref[...]).astype(o_ref.dtype)

    return pl.pallas_call(
        body, grid=(rows // tr,),
        in_specs=[pl.BlockSpec((tr, width), lambda i: (i, cb)), pl.BlockSpec((1, width), lambda i: (0, 0))],
        out_specs=pl.BlockSpec((tr, width), lambda i: (i, 0)),
        out_shape=jax.ShapeDtypeStruct((rows, width), CD), compiler_params=_cp("parallel"), name=name,
    )(x, g)


def _rms_bwd(x, g, dy, *, cb, width, tr, res=None, out_dtype, name):
    rows = x.shape[0]

    def body(*refs):
        if res is None:
            x_ref, g_ref, dy_ref, dx_ref, dg_ref = refs
        else:
            x_ref, g_ref, dy_ref, r_ref, dx_ref, dg_ref = refs

        @pl.when(pl.program_id(0) == 0)
        def _():
            dg_ref[...] = jnp.zeros_like(dg_ref)

        xv = x_ref[...].astype(F32)
        r = lax.rsqrt(jnp.mean(xv * xv, axis=-1, keepdims=True) + EPS)
        xh = xv * r
        dyv = dy_ref[...].astype(F32)
        dxh = dyv * g_ref[...]
        dx = r * (dxh - xh * jnp.mean(dxh * xh, axis=-1, keepdims=True))
        if res is not None:
            dx = dx + r_ref[...]
        dx_ref[...] = dx.astype(out_dtype)
        dg_ref[...] += jnp.sum(dyv * xh, axis=0, keepdims=True)

    blk = pl.BlockSpec((tr, width), lambda i: (i, 0))
    in_specs = [pl.BlockSpec((tr, width), lambda i: (i, cb)), pl.BlockSpec((1, width), lambda i: (0, 0)), blk]
    args = [x, g, dy]
    if res is not None:
        in_specs.append(blk)
        args.append(res)
    return pl.pallas_call(
        body, grid=(rows // tr,), in_specs=in_specs,
        out_specs=[blk, pl.BlockSpec((1, width), lambda i: (0, 0))],
        out_shape=[jax.ShapeDtypeStruct((rows, width), out_dtype), jax.ShapeDtypeStruct((1, width), F32)],
        compiler_params=_cp("arbitrary"), name=name,
    )(*args)


def _sigmoid(z):
    return 1.0 / (1.0 + jnp.exp(-z))


def _gate_fwd(o_mla, o_sb, proj, g_mla, g_sb, *, zb_mla, zb_sb, tr, name):
    rows, W = o_mla.shape

    def body(om_ref, os_ref, zm_ref, zs_ref, gm_ref, gs_ref, y_ref):
        for half, (o_ref, z_ref, g_ref) in enumerate(((om_ref, zm_ref, gm_ref), (os_ref, zs_ref, gs_ref))):
            o = o_ref[...].astype(F32)
            r = lax.rsqrt(jnp.mean(o * o, axis=-1, keepdims=True) + EPS)
            z = z_ref[...].astype(F32)
            y_ref[:, half * W:(half + 1) * W] = ((o * r * g_ref[...]) * (z * _sigmoid(z))).astype(y_ref.dtype)

    blk = pl.BlockSpec((tr, W), lambda i: (i, 0))
    gsp = pl.BlockSpec((1, W), lambda i: (0, 0))
    return pl.pallas_call(
        body, grid=(rows // tr,),
        in_specs=[blk, blk, pl.BlockSpec((tr, W), lambda i: (i, zb_mla)), pl.BlockSpec((tr, W), lambda i: (i, zb_sb)), gsp, gsp],
        out_specs=pl.BlockSpec((tr, 2 * W), lambda i: (i, 0)),
        out_shape=jax.ShapeDtypeStruct((rows, 2 * W), CD), compiler_params=_cp("parallel"), name=name,
    )(o_mla, o_sb, proj, proj, g_mla, g_sb)


def _gate_bwd(dy, o_mla, o_sb, proj, g_mla, g_sb, *, zb_mla, zb_sb, tr, name):
    rows, W = o_mla.shape

    def body(dy_ref, om_ref, os_ref, zm_ref, zs_ref, gm_ref, gs_ref, dom_ref, dos_ref, dzm_ref, dzs_ref, dgm_ref, dgs_ref):
        @pl.when(pl.program_id(0) == 0)
        def _():
            dgm_ref[...] = jnp.zeros_like(dgm_ref)
            dgs_ref[...] = jnp.zeros_like(dgs_ref)

        halves = ((om_ref, zm_ref, gm_ref, dom_ref, dzm_ref, dgm_ref), (os_ref, zs_ref, gs_ref, dos_ref, dzs_ref, dgs_ref))
        for half, (o_ref, z_ref, g_ref, do_ref, dz_ref, dg_ref) in enumerate(halves):
            dyv = dy_ref[:, half * W:(half + 1) * W].astype(F32)
            o = o_ref[...].astype(F32)
            r = lax.rsqrt(jnp.mean(o * o, axis=-1, keepdims=True) + EPS)
            oh = o * r
            g = g_ref[...]
            z = z_ref[...].astype(F32)
            sg = _sigmoid(z)
            dn = dyv * (z * sg)
            dz_ref[...] = (dyv * (oh * g) * (sg * (1.0 + z * (1.0 - sg)))).astype(dz_ref.dtype)
            dxh = dn * g
            do_ref[...] = (r * (dxh - oh * jnp.mean(dxh * oh, axis=-1, keepdims=True))).astype(do_ref.dtype)
            dg_ref[...] += jnp.sum(dn * oh, axis=0, keepdims=True)

    blk = pl.BlockSpec((tr, W), lambda i: (i, 0))
    gsp = pl.BlockSpec((1, W), lambda i: (0, 0))
    act = jax.ShapeDtypeStruct((rows, W), CD)
    gsh = jax.ShapeDtypeStruct((1, W), F32)
    return pl.pallas_call(
        body, grid=(rows // tr,),
        in_specs=[pl.BlockSpec((tr, 2 * W), lambda i: (i, 0)), blk, blk,
                  pl.BlockSpec((tr, W), lambda i: (i, zb_mla)), pl.BlockSpec((tr, W), lambda i: (i, zb_sb)), gsp, gsp],
        out_specs=[blk, blk, blk, blk, gsp, gsp], out_shape=[act, act, act, act, gsh, gsh],
        compiler_params=_cp("arbitrary"), name=name,
    )(dy, o_mla, o_sb, proj, proj, g_mla, g_sb)


def _rope_fwd(q_raw, proj, cs, sn, *, H, krb, tr, name):
    rows = q_raw.shape[0]
    HW = H * LANES

    def body(q_ref, kr_ref, c_ref, s_ref, qo_ref, kro_ref):
        c, s = c_ref[...], s_ref[...]
        qo_ref[:, :HW] = q_ref[:, :HW].astype(qo_ref.dtype)
        for h in range(H):
            sl = slice(HW + h * LANES, HW + (h + 1) * LANES)
            xh = q_ref[:, sl].astype(F32)
            qo_ref[:, sl] = (xh * c + pltpu.roll(xh, ROPE, 1) * s).astype(qo_ref.dtype)
        xk = kr_ref[...].astype(F32)
        kro_ref[...] = (xk * c + pltpu.roll(xk, ROPE, 1) * s).astype(kro_ref.dtype)

    tab = pl.BlockSpec((tr, LANES), lambda i: (i, 0))
    return pl.pallas_call(
        body, grid=(rows // tr,),
        in_specs=[pl.BlockSpec((tr, 2 * HW), lambda i: (i, 0)), pl.BlockSpec((tr, LANES), lambda i: (i, krb)), tab, tab],
        out_specs=[pl.BlockSpec((tr, 2 * HW), lambda i: (i, 0)), tab],
        out_shape=[jax.ShapeDtypeStruct((rows, 2 * HW), CD), jax.ShapeDtypeStruct((rows, LANES), CD)],
        compiler_params=_cp("parallel"), name=name,
    )(q_raw, proj, cs, sn)


def _rope_bwd(dqn, dqr, dkr, cs, sn, *, H, tr, name):
    rows = dqn.shape[0]
    HW = H * LANES

    def body(dqn_ref, dqr_ref, dkr_ref, c_ref, s_ref, dq_ref, dk_ref):
        c, s = c_ref[...], s_ref[...]
        dq_ref[:, :HW] = dqn_ref[...].astype(dq_ref.dtype)
        for h in range(H):
            d = dqr_ref[:, h * LANES:(h + 1) * LANES].astype(F32)
            dq_ref[:, HW + h * LANES:HW + (h + 1) * LANES] = (d * c + pltpu.roll(d * s, ROPE, 1)).astype(dq_ref.dtype)
        d = dkr_ref[...].astype(F32)
        dk_ref[...] = (d * c + pltpu.roll(d * s, ROPE, 1)).astype(dk_ref.dtype)

    tab = pl.BlockSpec((tr, LANES), lambda i: (i, 0))
    hb = pl.BlockSpec((tr, HW), lambda i: (i, 0))
    return pl.pallas_call(
        body, grid=(rows // tr,), in_specs=[hb, hb, tab, tab, tab],
        out_specs=[pl.BlockSpec((tr, 2 * HW), lambda i: (i, 0)), tab],
        out_shape=[jax.ShapeDtypeStruct((rows, 2 * HW), CD), jax.ShapeDtypeStruct((rows, LANES), CD)],
        compiler_params=_cp("parallel"), name=name,
    )(dqn, dqr, dkr, cs, sn)


def _loss_head(h, g, tgt, *, lo, hi, tr, name):
    rows, D = h.shape

    def body(h_ref, g_ref, t_ref, dh_ref, dg_ref, loss_ref):
        i = pl.program_id(0)

        @pl.when(i == 0)
        def _():
            dg_ref[...] = jnp.zeros_like(dg_ref)
            loss_ref[...] = jnp.zeros_like(loss_ref)

        xv = h_ref[...]
        r = lax.rsqrt(jnp.mean(xv * xv, axis=-1, keepdims=True) + EPS)
        xh = xv * r
        g_ = g_ref[...]
        rowid = i * tr + lax.broadcasted_iota(jnp.int32, (tr, 1), 0)
        valid = jnp.logical_and(rowid >= lo, rowid < hi)
        err = jnp.where(valid, xh * g_ - t_ref[...], 0.0)
        loss_ref[...] += jnp.sum(jnp.sum(err * err, axis=-1, keepdims=True), axis=0, keepdims=True) * (0.5 / D)
        dy = err * (1.0 / D)
        dxh = dy * g_
        dh_ref[...] = r * (dxh - xh * jnp.mean(dxh * xh, axis=-1, keepdims=True))
        dg_ref[...] += jnp.sum(dy * xh, axis=0, keepdims=True)

    blk = pl.BlockSpec((tr, D), lambda i: (i, 0))
    gsp = pl.BlockSpec((1, D), lambda i: (0, 0))
    return pl.pallas_call(
        body, grid=(rows // tr,), in_specs=[blk, gsp, blk],
        out_specs=[blk, gsp, pl.BlockSpec((1, 1), lambda i: (0, 0))],
        out_shape=[jax.ShapeDtypeStruct((rows, D), F32), jax.ShapeDtypeStruct((1, D), F32), jax.ShapeDtypeStruct((1, 1), F32)],
        compiler_params=_cp("arbitrary"), name=name,
    )(h, g, tgt)


HP = 2
HPW = HP * LANES


def _head(hh):
    return slice(hh * LANES, (hh + 1) * LANES)


def _mla_fwd(q, kv, kr, *, H, bq, scale, name):
    rows = q.shape[0]

    def body(qn_ref, qr_ref, kn_ref, v_ref, kr_ref, o_ref, lse_ref):
        i = pl.program_id(1)
        causal = lax.broadcasted_iota(jnp.int32, (bq, bq), 1) <= lax.broadcasted_iota(jnp.int32, (bq, bq), 0)

        def block(j, carry, diag):
            off = pl.multiple_of(j * bq, bq)
            krj = kr_ref[pl.ds(off, bq), :]
            out = []
            for hh in range(HP):
                m, l, acc = carry[hh]
                s = (_dot_nt(qn_ref[:, _head(hh)], kn_ref[pl.ds(off, bq), _head(hh)]) + _dot_nt(qr_ref[:, _head(hh)], krj)) * scale
                if diag:
                    s = jnp.where(causal, s, NEG)
                m_new = jnp.maximum(m, jnp.max(s, axis=-1, keepdims=True))
                p = jnp.exp(s - m_new)
                alpha = jnp.exp(m - m_new)
                l = alpha * l + jnp.sum(p, axis=-1, keepdims=True)
                acc = alpha * acc + _dot(p.astype(CD), v_ref[pl.ds(off, bq), _head(hh)])
                out.append((m_new, l, acc))
            return tuple(out)

        init = tuple((jnp.full((bq, 1), NEG, F32), jnp.zeros((bq, 1), F32), jnp.zeros((bq, LANES), F32)) for _ in range(HP))
        carry = lax.fori_loop(0, i, lambda j, cr: block(j, cr, False), init)
        carry = block(i, carry, True)
        for hh in range(HP):
            m, l, acc = carry[hh]
            o_ref[:, _head(hh)] = (acc / l).astype(o_ref.dtype)
            lse_ref[hh] = m + jnp.log(l)

    qb = lambda off: pl.BlockSpec((bq, HPW), lambda h, i: (i, off + h))
    kb = lambda off: pl.BlockSpec((rows, HPW), lambda h, i: (0, off + h))
    return pl.pallas_call(
        body, grid=(H // HP, rows // bq),
        in_specs=[qb(0), qb(H // HP), kb(0), kb(H // HP), pl.BlockSpec((rows, LANES), lambda h, i: (0, 0))],
        out_specs=[qb(0), pl.BlockSpec((HP, bq, 1), lambda h, i: (h, i, 0))],
        out_shape=[jax.ShapeDtypeStruct((rows, H * LANES), CD), jax.ShapeDtypeStruct((H, rows, 1), F32)],
        compiler_params=_cp("arbitrary", "arbitrary"), name=name,
    )(q, q, kv, kv, kr)


def _mla_bwd(q, kv, kr, o, do, lse, *, H, bq, scale, name):
    rows = q.shape[0]

    def body(qn_ref, qr_ref, kn_ref, v_ref, kr_ref, o_ref, do_ref, lse_ref, dqn_ref, dqr_ref, dkn_ref, dv_ref, dkr_ref):
        h, i = pl.program_id(0), pl.program_id(1)

        @pl.when(i == 0)
        def _():
            dkn_ref[...] = jnp.zeros_like(dkn_ref)
            dv_ref[...] = jnp.zeros_like(dv_ref)

        @pl.when(jnp.logical_and(i == 0, h == 0))
        def _():
            dkr_ref[...] = jnp.zeros_like(dkr_ref)

        causal = lax.broadcasted_iota(jnp.int32, (bq, bq), 1) <= lax.broadcasted_iota(jnp.int32, (bq, bq), 0)
        delta = [jnp.sum(do_ref[:, _head(hh)].astype(F32) * o_ref[:, _head(hh)].astype(F32), axis=-1, keepdims=True) for hh in range(HP)]

        def block(j, carry, diag):
            off = pl.multiple_of(j * bq, bq)
            krj = kr_ref[pl.ds(off, bq), :]
            out = []
            dkr = jnp.zeros((bq, LANES), F32)
            for hh in range(HP):
                dqn, dqr = carry[hh]
                qn, qr, do_ = qn_ref[:, _head(hh)], qr_ref[:, _head(hh)], do_ref[:, _head(hh)]
                kn, vj = kn_ref[pl.ds(off, bq), _head(hh)], v_ref[pl.ds(off, bq), _head(hh)]
                s = (_dot_nt(qn, kn) + _dot_nt(qr, krj)) * scale
                p = jnp.exp(s - lse_ref[hh])
                if diag:
                    p = jnp.where(causal, p, 0.0)
                ds = (p * (_dot_nt(do_, vj) - delta[hh]) * scale).astype(CD)
                dkn_ref[pl.ds(off, bq), _head(hh)] += _dot_tn(ds, qn)
                dv_ref[pl.ds(off, bq), _head(hh)] += _dot_tn(p.astype(CD), do_)
                dkr = dkr + _dot_tn(ds, qr)
                out.append((dqn + _dot(ds, kn), dqr + _dot(ds, krj)))
            dkr_ref[pl.ds(off, bq), :] += dkr
            return tuple(out)

        init = tuple((jnp.zeros((bq, LANES), F32), jnp.zeros((bq, LANES), F32)) for _ in range(HP))
        carry = lax.fori_loop(0, i, lambda j, cr: block(j, cr, False), init)
        carry = block(i, carry, True)
        for hh in range(HP):
            dqn_ref[:, _head(hh)] = carry[hh][0].astype(dqn_ref.dtype)
            dqr_ref[:, _head(hh)] = carry[hh][1].astype(dqr_ref.dtype)

    qb = lambda off: pl.BlockSpec((bq, HPW), lambda h, i: (i, off + h))
    kb = lambda off: pl.BlockSpec((rows, HPW), lambda h, i: (0, off + h))
    whole = pl.BlockSpec((rows, LANES), lambda h, i: (0, 0))
    act = jax.ShapeDtypeStruct((rows, H * LANES), CD)
    acc = jax.ShapeDtypeStruct((rows, H * LANES), F32)
    return pl.pallas_call(
        body, grid=(H // HP, rows // bq),
        in_specs=[qb(0), qb(H // HP), kb(0), kb(H // HP), whole, qb(0), qb(0), pl.BlockSpec((HP, bq, 1), lambda h, i: (h, i, 0))],
        out_specs=[qb(0), qb(0), kb(0), kb(0), whole],
        out_shape=[act, act, acc, acc, jax.ShapeDtypeStruct((rows, LANES), F32)],
        compiler_params=_cp("arbitrary", "arbitrary"), name=name,
    )(q, q, kv, kv, kr, o, do, lse)


def _log_sigmoids(z):
    lb = jnp.minimum(z, 0.0) - jnp.log(1.0 + jnp.exp(-jnp.abs(z)))
    return lb, lb - z


def _split(x):
    hi = x.astype(CD)
    return hi, (x - hi.astype(F32)).astype(CD)


def _rounded(x):
    hi, lo = _split(x)
    return hi.astype(F32) + lo.astype(F32)


def _tri(ck, rel):
    j, s = np.arange(ck)[:, None], np.arange(ck)[None, :]
    return jnp.asarray({"gt": j > s, "le": j <= s, "lt": j < s}[rel], CD)


def _sb_fwd(proj, *, H, qb0, kb0, vb0, bq, ck, scale, name):
    rows = proj.shape[0]
    nsub = bq // ck

    def body(q_ref, k_ref, v_ref, u_ref, o_ref, t_ref):
        i = pl.program_id(1)
        u = u_ref[...]
        strict = lax.broadcasted_iota(jnp.int32, (bq, bq), 1) < lax.broadcasted_iota(jnp.int32, (bq, bq), 0)

        def block(j, carry, diag):
            off = pl.multiple_of(j * bq, bq)
            out = []
            for hh in range(HP):
                c, acc = carry[hh]
                z = _dot_nt(q_ref[:, _head(hh)], k_ref[pl.ds(off, bq), _head(hh)]) * scale
                lb, lom = _log_sigmoids(z)
                if diag:
                    lom = jnp.where(strict, lom, 0.0)
                hi, lo = _split(lom)
                parts = [None] * nsub
                for s in reversed(range(nsub)):
                    ss = slice(s * ck, (s + 1) * ck)
                    rin = _dot(hi[:, ss], u) + _dot(lo[:, ss], u)
                    parts[s] = jnp.exp(lb[:, ss] + rin + c)
                    c = c + rin[:, :1] + _rounded(lom[:, s * ck:s * ck + 1])
                a = parts[0] if nsub == 1 else jnp.concatenate(parts, axis=1)
                if diag:
                    a = jnp.where(strict, a, 0.0)
                out.append((c, acc + _dot(a.astype(CD), v_ref[pl.ds(off, bq), _head(hh)])))
            return tuple(out)

        init = tuple((jnp.zeros((bq, 1), F32), jnp.zeros((bq, LANES), F32)) for _ in range(HP))
        carry = block(i, init, True)
        carry = lax.fori_loop(0, i, lambda jj, cr: block(i - 1 - jj, cr, False), carry)
        for hh in range(HP):
            o_ref[:, _head(hh)] = carry[hh][1].astype(o_ref.dtype)
            t_ref[hh] = carry[hh][0]

    kb = lambda off: pl.BlockSpec((rows, HPW), lambda h, i: (0, off // HP + h))
    return pl.pallas_call(
        body, grid=(H // HP, rows // bq),
        in_specs=[pl.BlockSpec((bq, HPW), lambda h, i: (i, qb0 // HP + h)), kb(kb0), kb(vb0), pl.BlockSpec((ck, ck), lambda h, i: (0, 0))],
        out_specs=[pl.BlockSpec((bq, HPW), lambda h, i: (i, h)), pl.BlockSpec((HP, bq, 1), lambda h, i: (h, i, 0))],
        out_shape=[jax.ShapeDtypeStruct((rows, H * LANES), CD), jax.ShapeDtypeStruct((H, rows, 1), F32)],
        compiler_params=_cp("arbitrary", "arbitrary"), name=name,
    )(proj, proj, proj, _tri(ck, "gt"))


def _sb_bwd(proj, do, tot, *, H, qb0, kb0, vb0, bq, ck, scale, name):
    rows = proj.shape[0]
    nsub = bq // ck

    def body(q_ref, k_ref, v_ref, do_ref, t_ref, ule_ref, ult_ref, dq_ref, dk_ref, dv_ref):
        i = pl.program_id(1)

        @pl.when(i == 0)
        def _():
            dk_ref[...] = jnp.zeros_like(dk_ref)
            dv_ref[...] = jnp.zeros_like(dv_ref)

        ule, ult = ule_ref[...], ult_ref[...]
        strict = lax.broadcasted_iota(jnp.int32, (bq, bq), 1) < lax.broadcasted_iota(jnp.int32, (bq, bq), 0)

        def block(j, carry, diag):
            off = pl.multiple_of(j * bq, bq)
            out = []
            for hh in range(HP):
                pc, gc, dq = carry[hh]
                q_, do_, tot_ = q_ref[:, _head(hh)], do_ref[:, _head(hh)], t_ref[hh]
                kj, vj = k_ref[pl.ds(off, bq), _head(hh)], v_ref[pl.ds(off, bq), _head(hh)]
                z = _dot_nt(q_, kj) * scale
                lb, lom = _log_sigmoids(z)
                if diag:
                    lom = jnp.where(strict, lom, 0.0)
                hi, lo = _split(lom)
                parts = []
                for s in range(nsub):
                    ss = slice(s * ck, (s + 1) * ck)
                    pin = _dot(hi[:, ss], ule) + _dot(lo[:, ss], ule)
                    parts.append(jnp.exp(lb[:, ss] + (tot_ - (pc + pin))))
                    pc = pc + pin[:, ck - 1:ck]
                a = parts[0] if nsub == 1 else jnp.concatenate(parts, axis=1)
                if diag:
                    a = jnp.where(strict, a, 0.0)
                g = a * _dot_nt(do_, vj)
                gb = g.astype(CD)
                parts = []
                for s in range(nsub):
                    ss = slice(s * ck, (s + 1) * ck)
                    gin = _dot(gb[:, ss], ult)
                    parts.append(gc + gin)
                    gc = gc + gin[:, ck - 1:ck] + g[:, (s + 1) * ck - 1:(s + 1) * ck].astype(CD).astype(F32)
                big_g = parts[0] if nsub == 1 else jnp.concatenate(parts, axis=1)
                sg = jnp.exp(lb)
                dz = (g * (1.0 - sg) - big_g * sg) * scale
                if diag:
                    dz = jnp.where(strict, dz, 0.0)
                dz = dz.astype(CD)
                dk_ref[pl.ds(off, bq), _head(hh)] += _dot_tn(dz, q_)
                dv_ref[pl.ds(off, bq), _head(hh)] += _dot_tn(a.astype(CD), do_)
                out.append((pc, gc, dq + _dot(dz, kj)))
            return tuple(out)

        init = tuple((jnp.zeros((bq, 1), F32), jnp.zeros((bq, 1), F32), jnp.zeros((bq, LANES), F32)) for _ in range(HP))
        carry = lax.fori_loop(0, i, lambda j, cr: block(j, cr, False), init)
        carry = block(i, carry, True)
        for hh in range(HP):
            dq_ref[:, _head(hh)] = carry[hh][2].astype(dq_ref.dtype)

    kb = lambda off: pl.BlockSpec((rows, HPW), lambda h, i: (0, off // HP + h))
    ob = pl.BlockSpec((bq, HPW), lambda h, i: (i, h))
    tri = pl.BlockSpec((ck, ck), lambda h, i: (0, 0))
    acc = jax.ShapeDtypeStruct((rows, H * LANES), F32)
    return pl.pallas_call(
        body, grid=(H // HP, rows // bq),
        in_specs=[pl.BlockSpec((bq, HPW), lambda h, i: (i, qb0 // HP + h)), kb(kb0), kb(vb0), ob,
                  pl.BlockSpec((HP, bq, 1), lambda h, i: (h, i, 0)), tri, tri],
        out_specs=[ob, kb(0), kb(0)],
        out_shape=[jax.ShapeDtypeStruct((rows, H * LANES), CD), acc, acc],
        compiler_params=_cp("arbitrary", "arbitrary"), name=name,
    )(proj, proj, proj, do, tot, _tri(ck, "le"), _tri(ck, "lt"))


def _dev_index(p):
    return 4 * p[0] + 2 * p[1] + p[2]


def _all_gather(arrays, *, name):
    A = len(arrays)

    def body(*refs):
        ins, outs = refs[:A], refs[A:2 * A]
        send_sems, recv_sems, local_sems = refs[2 * A:]
        x, y, c = lax.axis_index("x"), lax.axis_index("y"), lax.axis_index("c")
        me, sibling = (x, y, c), (x, y, 1 - c)
        chips = [(1 - x, y), (x, 1 - y), (1 - x, 1 - y)]

        def copy(a, k, block, to, src=None):
            dst = outs[a].at[_dev_index(block)]
            return pltpu.make_async_remote_copy(
                src_ref=dst if src is None else src, dst_ref=dst, send_sem=send_sems.at[a * 7 + k],
                recv_sem=recv_sems.at[a * 7 + k], device_id=to, device_id_type=MESH)

        mine = [pltpu.make_async_copy(ins[a], outs[a].at[_dev_index(me)], local_sems.at[a]) for a in range(A)]
        for cp in mine:
            cp.start()
        first = []
        for a in range(A):
            first.append(copy(a, 0, me, sibling, src=ins[a]))
            first += [copy(a, 1 + j, me, (*chip, c), src=ins[a]) for j, chip in enumerate(chips)]
        for cp in first:
            cp.start()
        passed = []
        for j, chip in enumerate(chips):
            for a in range(A):
                copy(a, 1 + j, (*chip, c), me).wait_recv()
                fwd = copy(a, 4 + j, (*chip, c), sibling)
                fwd.start()
                passed.append(fwd)
        for a in range(A):
            copy(a, 0, sibling, me).wait_recv()
            for j, chip in enumerate(chips):
                copy(a, 4 + j, (*chip, 1 - c), me).wait_recv()
        for cp in first + passed:
            cp.wait_send()
        for cp in mine:
            cp.wait()

    any_spec = pl.BlockSpec(memory_space=pl.ANY)
    return pl.pallas_call(
        body, in_specs=[any_spec] * A, out_specs=[any_spec] * A,
        out_shape=[jax.ShapeDtypeStruct((N_DEV,) + a.shape, a.dtype) for a in arrays],
        scratch_shapes=[pltpu.SemaphoreType.DMA((7 * A,)), pltpu.SemaphoreType.DMA((7 * A,)), pltpu.SemaphoreType.DMA((A,))],
        name=name,
    )(*arrays)


_HBM = pl.BlockSpec(memory_space=pltpu.HBM)
_SEM = pl.BlockSpec(memory_space=pltpu.SEMAPHORE)
_EFFECT = pltpu.SideEffectType.DATAFLOW_SIDE_EFFECTING
_PEER_ORDER = (0, 1, 3, 2, 4, 5, 6)


def _peers():
    x, y, c = lax.axis_index("x"), lax.axis_index("y"), lax.axis_index("c")
    return _dev_index((x, y, c)), [((1 - x) if r & 4 else x, (1 - y) if r & 2 else y, (1 - c) if r & 1 else c) for r in range(1, N_DEV)]


def _xchg_copy(src, land, sends, recvs, a, k, me_i, peers, scatter, arriving):
    peer_i = _dev_index(peers[k])
    return pltpu.make_async_remote_copy(
        src_ref=src.at[peer_i] if scatter else src, dst_ref=land.at[peer_i if arriving else me_i],
        send_sem=sends.at[a * 7 + k], recv_sem=recvs.at[a * 7 + k], device_id=peers[k], device_id_type=MESH)


def _xchg_start(arrays, *, scatter, name):
    A = len(arrays)
    lands = [lax.empty(a.shape if scatter else (N_DEV,) + a.shape, a.dtype) for a in arrays]

    def body(*refs):
        srcs, land = refs[:A], refs[A:2 * A]
        sends, recvs, token = refs[2 * A], refs[2 * A + 1], refs[-1]
        me_i, peers = _peers()
        for k in _PEER_ORDER:
            for a in range(A):
                _xchg_copy(srcs[a], land[a], sends, recvs, a, k, me_i, peers, scatter, False).start()
        token[...] = jnp.zeros_like(token)

    hbm = lambda a: pltpu.HBM(a.shape, a.dtype)
    outs = pl.pallas_call(
        body, name=name, in_specs=[_HBM] * (2 * A),
        out_shape=(pltpu.SemaphoreType.DMA((7 * A,)), pltpu.SemaphoreType.DMA((7 * A,)), *[hbm(a) for a in arrays],
                   *[hbm(a) for a in lands], jax.ShapeDtypeStruct((8, LANES), F32)),
        out_specs=(_SEM, _SEM, *([_HBM] * (2 * A)), pl.BlockSpec(memory_space=pltpu.VMEM)),
        input_output_aliases={n: 2 + n for n in range(2 * A)},
        compiler_params=pltpu.CompilerParams(has_side_effects=_EFFECT),
    )(*[pltpu.with_memory_space_constraint(a, pltpu.HBM) for a in arrays],
      *[pltpu.with_memory_space_constraint(a, pltpu.HBM) for a in lands])
    return outs[0], outs[1], list(outs[2:2 + A]), list(outs[2 + A:2 + 2 * A]), outs[-1]


def _xchg_wait(handle, after, *, scatter, name):
    sends, recvs, srcs, lands, _ = handle
    A = len(srcs)

    def body(*refs):
        src, land = refs[:A], refs[A:2 * A]
        send_sems, recv_sems = refs[2 * A], refs[2 * A + 1]
        me_i, peers = _peers()
        for k in _PEER_ORDER:
            for a in range(A):
                cp = _xchg_copy(src[a], land[a], send_sems, recv_sems, a, k, me_i, peers, scatter, True)
                cp.wait_send()
                cp.wait_recv()

    outs = pl.pallas_call(
        body, name=name, in_specs=[_HBM] * (2 * A) + [_SEM, _SEM, pl.BlockSpec(memory_space=pl.ANY)],
        out_shape=tuple(pltpu.HBM(a.shape, a.dtype) for a in srcs + lands), out_specs=tuple([_HBM] * (2 * A)),
        input_output_aliases={n: n for n in range(2 * A)},
        compiler_params=pltpu.CompilerParams(has_side_effects=_EFFECT),
    )(*srcs, *lands, sends, recvs, after)
    my = _dev_index((lax.axis_index("x"), lax.axis_index("y"), lax.axis_index("c")))
    got = []
    for src, land in zip(outs[:A], outs[A:]):
        own = lax.dynamic_index_in_dim(src, my, 0, keepdims=True) if scatter else src[None]
        got.append(lax.dynamic_update_index_in_dim(land, own, my, 0))
    return got


def _adamw(g, w, m, v):
    m = ADAM_B1 * m + (1.0 - ADAM_B1) * g
    v = ADAM_B2 * v + (1.0 - ADAM_B2) * (g * g)
    delta = -ADAM_LR * ((m / ADAM_BC1) / (jnp.sqrt(v / ADAM_BC2) + ADAM_EPS) + ADAM_WD * w)
    return delta, m, v


def _adam_pieces(pieces, w, m, v, bufs, *, layer, name):
    depth, R, C = w.shape
    tr = _pick(R, [t for t in (512, 256, 128, 64, 32, 16) if t * C * 4 <= 2**20])

    def body(p_ref, w_ref, m_ref, v_ref, b0, b1, b2, b3, g_ref, d_ref, mo_ref, vo_ref):
        g = p_ref[0].astype(F32)
        for k in range(1, N_DEV):
            g = g + p_ref[k].astype(F32)
        g_ref[0] = g
        d_ref[0], mo_ref[0], vo_ref[0] = _adamw(g, w_ref[0], m_ref[0], v_ref[0])

    blk = pl.BlockSpec((1, tr, C), lambda i: (layer, i, 0))
    anyspec = pl.BlockSpec(memory_space=pl.ANY)
    return pl.pallas_call(
        body, grid=(R // tr,),
        in_specs=[pl.BlockSpec((N_DEV, tr, C), lambda i: (0, i, 0)), blk, blk, blk] + [anyspec] * 4,
        out_specs=[blk] * 4, out_shape=[jax.ShapeDtypeStruct((depth, R, C), F32)] * 4,
        input_output_aliases={4: 0, 5: 1, 6: 2, 7: 3}, compiler_params=_cp("parallel"), name=name,
    )(pieces, w, m, v, *bufs)


def _adam_direct(g, w, m, v, *, name):
    sh = jax.ShapeDtypeStruct(w.shape, F32)

    def body(g_ref, w_ref, m_ref, v_ref, d_ref, mo_ref, vo_ref):
        d_ref[...], mo_ref[...], vo_ref[...] = _adamw(g_ref[...], w_ref[...], m_ref[...], v_ref[...])

    return pl.pallas_call(body, out_shape=[sh] * 3, compiler_params=_cp(), name=name)(g, w, m, v)


def _sum_devices(parts, *, name):
    def body(p_ref, o_ref):
        s = p_ref[0]
        for k in range(1, N_DEV):
            s = s + p_ref[k]
        o_ref[...] = s

    return pl.pallas_call(body, out_shape=jax.ShapeDtypeStruct(parts.shape[1:], F32), compiler_params=_cp(), name=name)(parts)


def _rope_cols_pad(w):
    z = jnp.zeros(w.shape[:-1] + (ROPE // 2,), w.dtype)
    return jnp.concatenate([w[..., :ROPE // 2], z, w[..., ROPE // 2:], z], axis=-1)


def _rope_cols_unpad(w):
    return jnp.concatenate([w[..., :ROPE // 2], w[..., ROPE:ROPE + ROPE // 2]], axis=-1)


class _Dims:
    def __init__(self, D, S, NM, QL, KVL, WM, WS):
        self.D, self.S, self.NM, self.QL, self.KVL, self.WM, self.WS = D, S, NM, QL, KVL, WM, WS
        self.H = WM // LANES
        self.HW = self.H * LANES
        self.L = NM + S
        self.bq = 384 if self.L > 1024 else 128
        self.Lp = -(-self.L // self.bq) * self.bq
        self.ck_sb = LANES
        self.tr = 128
        self.sizes = (QL, KVL, ROPE, WM, WS, WS, WS, WS)
        self.d_in = sum(self.sizes)
        front = QL + KVL + LANES
        self.front = -(-front // WM) * WM
        self.pad = self.front - front
        self.dn = self.front + WM + 4 * WS
        assert WM == WS and QL % KVL == 0 and KVL % LANES == 0 and self.d_in % N_DEV == 0 and D % N_DEV == 0
        self.mla_scale = 1.0 / math.sqrt(LANES + ROPE)
        self.sb_scale = 1.0 / math.sqrt(LANES)


def _w_in_nice(d, w):
    parts = jnp.split(w, np.cumsum(d.sizes)[:-1].tolist(), axis=1)
    return jnp.concatenate([parts[0], parts[1], _rope_cols_pad(parts[2]), jnp.zeros((w.shape[0], d.pad), w.dtype)] + parts[3:], axis=1)


def _w_in_orig(d, w):
    a, b = d.QL + d.KVL, d.front
    return jnp.concatenate([w[:, :a], _rope_cols_unpad(w[:, a:a + LANES]), w[:, b:]], axis=1)


def _w_uq_nice(d, w):
    w3 = w.reshape(d.QL, d.H, LANES + ROPE)
    return jnp.concatenate([w3[..., :LANES].reshape(d.QL, d.HW), _rope_cols_pad(w3[..., LANES:]).reshape(d.QL, d.HW)], axis=1)


def _w_uq_orig(d, w):
    nope = w[:, :d.HW].reshape(d.QL, d.H, LANES)
    rope = _rope_cols_unpad(w[:, d.HW:].reshape(d.QL, d.H, LANES))
    return jnp.concatenate([nope, rope], axis=-1).reshape(d.QL, d.H * (LANES + ROPE))


def _w_ukv_nice(d, w):
    return w.reshape(d.KVL, d.H, 2, LANES).transpose(0, 2, 1, 3).reshape(d.KVL, 2 * d.HW)


def _w_ukv_orig(d, w):
    return w.reshape(d.KVL, 2, d.H, LANES).transpose(0, 2, 1, 3).reshape(d.KVL, 2 * d.HW)


def _cols_from_shards(g):
    return g.transpose(1, 0, 2).reshape(g.shape[1], -1)


def _cols_to_shards(w):
    return w.reshape(w.shape[0], N_DEV, -1).transpose(1, 0, 2)


def _rope_tables(rows):
    inv_freq = ROPE_THETA ** (-jnp.arange(0, ROPE, 2, dtype=F32) / ROPE)
    ang = jnp.arange(rows, dtype=jnp.int32).astype(F32)[:, None] * inv_freq[None, :]
    cos, sin, z = jnp.cos(ang), jnp.sin(ang), jnp.zeros_like(ang)
    return jnp.concatenate([cos, z, cos, z], axis=1), jnp.concatenate([-sin, z, sin, z], axis=1)


def _flat_pack(vecs):
    flat = jnp.concatenate([v.reshape(-1).astype(F32) for v in vecs])
    n = flat.shape[0]
    rows = -(-n // (8 * 1024)) * 8
    offs = np.cumsum([0] + [int(np.prod(v.shape)) for v in vecs])
    return jnp.pad(flat, (0, rows * 1024 - n)).reshape(rows, 1024), offs


def kernel(x, meta_tokens, g_norm, w_in, g_q, g_kv, w_uq, w_ukv, g_out_mla, g_out_sb, w_o, g_final, loss_target, m_meta_tokens, m_g_norm, m_w_in, m_g_q, m_g_kv, m_w_uq, m_w_ukv, m_g_out_mla, m_g_out_sb, m_w_o, m_g_final, v_meta_tokens, v_g_norm, v_w_in, v_g_q, v_g_kv, v_w_uq, v_w_ukv, v_g_out_mla, v_g_out_sb, v_w_o, v_g_final):
    depth = g_norm.shape[0]
    d = _Dims(D=x.shape[-1], S=x.shape[1], NM=meta_tokens.shape[0], QL=g_q.shape[1], KVL=g_kv.shape[1],
              WM=g_out_mla.shape[1], WS=g_out_sb.shape[1])
    D, Lp, H, tr = d.D, d.Lp, d.H, d.tr
    my = _dev_index((lax.axis_index("x"), lax.axis_index("y"), lax.axis_index("c")))

    shards = lambda l: [w_in[l].astype(CD), w_uq[l].astype(CD), w_ukv[l].astype(CD), w_o[l].astype(CD)]
    *gathered0, g_meta = _all_gather(shards(0) + [meta_tokens], name="gather_layer0")
    gathers = {l: _xchg_start(shards(l), scatter=False, name=f"gather_start_{l}") for l in range(1, depth)}
    started = sum(hd[4][0, 0] for hd in gathers.values())

    def relayout(g_in, g_uq, g_ukv, g_o):
        return (_w_in_nice(d, _cols_from_shards(g_in)), _w_uq_nice(d, _cols_from_shards(g_uq)),
                _w_ukv_nice(d, _cols_from_shards(g_ukv)), g_o.reshape(D, D))

    wn, wuq, wukv, wo = [None] * depth, [None] * depth, [None] * depth, [None] * depth
    meta_full = _cols_from_shards(g_meta)

    cs, sn = _rope_tables(Lp)
    h = jnp.concatenate([meta_full, x[0], jnp.zeros((Lp - d.L, D), F32)], axis=0)
    tgt = jnp.concatenate([jnp.zeros((d.NM, D), F32), loss_target[0], jnp.zeros((Lp - d.L, D), F32)], axis=0)

    b128 = lambda col: col // LANES
    krb = b128(d.QL + d.KVL)
    zb_mla, zb_sb = d.front // d.WM, (d.front + d.WM + 3 * d.WS) // d.WS
    qb0, kb0, vb0 = b128(d.front + d.WM), b128(d.front + d.WM + d.WS), b128(d.front + d.WM + 2 * d.WS)
    row = lambda a, i: a[i][None, :]

    saved = []
    for i in range(depth):
        if i == 0:
            wn[i], wuq[i], wukv[i], wo[i] = relayout(*gathered0)
            gain = row(g_norm, i) + started
        else:
            wn[i], wuq[i], wukv[i], wo[i] = relayout(*_xchg_wait(gathers[i], h, scatter=False, name=f"gather_wait_{i}"))
            gain = row(g_norm, i)
        u = _rms_fwd(h, gain, cb=0, width=D, tr=tr, name=f"rms_in_{i}")
        proj = _mm(u, wn[i], out_dtype=CD, name=f"proj_{i}")
        cqn = _rms_fwd(proj, row(g_q, i), cb=0, width=d.QL, tr=tr, name=f"rms_q_{i}")
        ckvn = _rms_fwd(proj, row(g_kv, i), cb=d.QL // d.KVL, width=d.KVL, tr=tr, name=f"rms_kv_{i}")
        q_raw = _mm(cqn, wuq[i], out_dtype=F32, name=f"up_q_{i}")
        kvn = _mm(ckvn, wukv[i], out_dtype=CD, name=f"up_kv_{i}")
        q, kr = _rope_fwd(q_raw, proj, cs, sn, H=H, krb=krb, tr=tr, name=f"rope_{i}")
        o_mla, lse = _mla_fwd(q, kvn, kr, H=H, bq=d.bq, scale=d.mla_scale, name=f"mla_fwd_{i}")
        o_sb, tot = _sb_fwd(proj, H=H, qb0=qb0, kb0=kb0, vb0=vb0, bq=d.bq, ck=d.ck_sb, scale=d.sb_scale, name=f"sb_fwd_{i}")
        y = _gate_fwd(o_mla, o_sb, proj, row(g_out_mla, i), row(g_out_sb, i), zb_mla=zb_mla, zb_sb=zb_sb, tr=tr, name=f"gate_fwd_{i}")
        h_next = _mm(y, wo[i], out_dtype=F32, res=h, name=f"out_proj_{i}")
        saved.append((h, u, proj, cqn, ckvn, q, kvn, kr, o_mla, lse, o_sb, tot, y))
        h = h_next

    dh, dg_final, loss_part = _loss_head(h, g_final[None, :], tgt, lo=d.NM, hi=d.L, tr=tr, name="loss_head")

    dg_norm, dg_q, dg_kv, dg_om, dg_os = [None] * depth, [None] * depth, [None] * depth, [None] * depth, [None] * depth
    big_w = {"w_in": (w_in, m_w_in, v_w_in), "w_uq": (w_uq, m_w_uq, v_w_uq), "w_ukv": (w_ukv, m_w_ukv, v_w_ukv), "w_o": (w_o, m_w_o, v_w_o)}
    res = {n: [lax.empty(w.shape, F32) for _ in range(4)] for n, (w, _, _) in big_w.items()}
    exchanges = [None] * depth

    def finish(l, after):
        pieces = _xchg_wait(exchanges[l], after, scatter=True, name=f"exch_wait_{l}")
        for (n, (w, m, v)), p in zip(big_w.items(), pieces):
            res[n] = _adam_pieces(p, w, m, v, res[n], layer=l, name=f"adam_{n}_{l}")

    zpad = jnp.zeros((Lp, d.pad), CD)
    for i in reversed(range(depth)):
        h_in, u, proj, cqn, ckvn, q, kvn, kr, o_mla, lse, o_sb, tot, y = saved[i]
        dy = _mm(dh, wo[i], tb=True, out_dtype=CD, name=f"d_y_{i}")
        dwo = _mm(y, dh, ta=True, out_dtype=F32, name=f"d_wo_{i}")
        sent = exchanges[i + 1][4][0, 0] if i + 1 < depth else 0.0
        do_mla, do_sb, dz_mla, dz_sb, dg_om[i], dg_os[i] = _gate_bwd(
            dy, o_mla, o_sb, proj, row(g_out_mla, i) + sent, row(g_out_sb, i), zb_mla=zb_mla, zb_sb=zb_sb, tr=tr, name=f"gate_bwd_{i}")
        dqn, dqr, dkn, dv, dkr = _mla_bwd(q, kvn, kr, o_mla, do_mla, lse, H=H, bq=d.bq, scale=d.mla_scale, name=f"mla_bwd_{i}")
        dq_sb, dk_sb, dv_sb = _sb_bwd(proj, do_sb, tot, H=H, qb0=qb0, kb0=kb0, vb0=vb0, bq=d.bq, ck=d.ck_sb, scale=d.sb_scale, name=f"sb_bwd_{i}")
        dq, dk_r = _rope_bwd(dqn, dqr, dkr, cs, sn, H=H, tr=tr, name=f"rope_bwd_{i}")
        dcqn = _mm(dq, wuq[i], tb=True, out_dtype=CD, name=f"d_cq_{i}")
        dwuq = _mm(cqn, dq, ta=True, out_dtype=F32, name=f"d_wuq_{i}")
        dkv = jnp.concatenate([dkn, dv], axis=1).astype(CD)
        dckvn = _mm(dkv, wukv[i], tb=True, out_dtype=CD, name=f"d_ckv_{i}")
        dwukv = _mm(ckvn, dkv, ta=True, out_dtype=F32, name=f"d_wukv_{i}")
        dc_q, dg_q[i] = _rms_bwd(proj, row(g_q, i), dcqn, cb=0, width=d.QL, tr=tr, out_dtype=CD, name=f"rms_q_bwd_{i}")
        dc_kv, dg_kv[i] = _rms_bwd(proj, row(g_kv, i), dckvn, cb=d.QL // d.KVL, width=d.KVL, tr=tr, out_dtype=CD, name=f"rms_kv_bwd_{i}")
        dproj = jnp.concatenate([dc_q, dc_kv, dk_r, zpad, dz_mla, dq_sb, dk_sb.astype(CD), dv_sb.astype(CD), dz_sb], axis=1)
        du = _mm(dproj, wn[i], tb=True, out_dtype=CD, name=f"d_u_{i}")
        dwn = _mm(u, dproj, ta=True, out_dtype=F32, name=f"d_wn_{i}")
        dh, dg_norm[i] = _rms_bwd(h_in, row(g_norm, i), du, cb=0, width=D, tr=tr, res=dh, out_dtype=F32, name=f"rms_in_bwd_{i}")
        exchanges[i] = _xchg_start(
            [_cols_to_shards(_w_in_orig(d, dwn)).astype(CD), _cols_to_shards(_w_uq_orig(d, dwuq)).astype(CD),
             _cols_to_shards(_w_ukv_orig(d, dwukv)).astype(CD), dwo.reshape(N_DEV, D // N_DEV, D).astype(CD)],
            scatter=True, name=f"exch_start_{i}")
        if i + 1 < depth:
            finish(i + 1, dh)
    finish(0, dh)

    cat = lambda parts: jnp.concatenate(parts, axis=0)
    small, offs = _flat_pack([cat(dg_norm), cat(dg_q), cat(dg_kv), cat(dg_om), cat(dg_os), dg_final, dh[:d.NM], loss_part])
    (small_all,) = _all_gather([small], name="gather_small")
    small_sum = _sum_devices(small_all, name="sum_small").reshape(-1)
    seg = lambda k, shape: small_sum[offs[k]:offs[k + 1]].reshape(shape)
    loss = small_sum[offs[7]]
    grad_x = dh[d.NM:d.L][None]

    g_meta_full = seg(6, (d.NM, D))
    ncol = meta_tokens.shape[1]
    g_meta_mine = lax.dynamic_slice(g_meta_full, (0, my * ncol), (d.NM, ncol))
    res["meta_tokens"] = [g_meta_mine] + list(_adam_direct(g_meta_mine, meta_tokens, m_meta_tokens, v_meta_tokens, name="adam_meta"))

    small_names = ["g_norm", "g_q", "g_kv", "g_out_mla", "g_out_sb", "g_final"]
    small_w = [g_norm, g_q, g_kv, g_out_mla, g_out_sb, g_final]
    small_m = [m_g_norm, m_g_q, m_g_kv, m_g_out_mla, m_g_out_sb, m_g_final]
    small_v = [v_g_norm, v_g_q, v_g_kv, v_g_out_mla, v_g_out_sb, v_g_final]
    n_small = int(offs[6])
    rows_s = -(-n_small // (8 * 1024)) * 8
    g_flat = small_sum[:rows_s * 1024].reshape(rows_s, 1024)
    wf, _ = _flat_pack(small_w)
    mf, _ = _flat_pack(small_m)
    vf, _ = _flat_pack(small_v)
    d_f, m_f, v_f = _adam_direct(g_flat, wf, mf, vf, name="adam_small")
    for k, nm in enumerate(small_names):
        sl = lambda f: f.reshape(-1)[offs[k]:offs[k + 1]].reshape(small_w[k].shape)
        res[nm] = [sl(g_flat), sl(d_f), sl(m_f), sl(v_f)]

    names = ["meta_tokens", "g_norm", "w_in", "g_q", "g_kv", "w_uq", "w_ukv", "g_out_mla", "g_out_sb", "w_o", "g_final"]
    return (loss, grad_x, *[res[n][0] for n in names], *[res[n][1] for n in names],
            *[res[n][2] for n in names], *[res[n][3] for n in names])
```

```python
import math

import numpy as np
import jax
import jax.numpy as jnp
from jax import lax
from jax.experimental import pallas as pl
from jax.experimental.pallas import tpu as pltpu

CD = jnp.bfloat16
F32 = jnp.float32

LANES = 128
N_DEV = 8
EPS = 1e-6
ROPE = 64
ROPE_THETA = 10000.0
NEG = -1e30
VMEM_LIMIT = 48 * 2**20
VMEM_LIMIT_WIDE = 56 * 2**20

ADAM_LR, ADAM_B1, ADAM_B2, ADAM_EPS, ADAM_WD, ADAM_STEP = 0.001, 0.9, 0.999, 1e-08, 0.01, 10
ADAM_BC1 = 1.0 - ADAM_B1**ADAM_STEP
ADAM_BC2 = 1.0 - ADAM_B2**ADAM_STEP

MESH = pl.DeviceIdType.MESH


def _cp(*sem):
    return pltpu.CompilerParams(dimension_semantics=tuple(sem) if sem else None, vmem_limit_bytes=VMEM_LIMIT)


def _pick(dim, prefs):
    for p in prefs:
        if dim % p == 0:
            return p
    raise ValueError(f"no tile for {dim} in {prefs}")


def _dot(a, b):
    return jnp.dot(a, b, preferred_element_type=F32)


def _dot_nt(a, b):
    return lax.dot_general(a, b, (((1,), (1,)), ((), ())), preferred_element_type=F32)


def _dot_tn(a, b):
    return lax.dot_general(a, b, (((0,), (0,)), ((), ())), preferred_element_type=F32)


def _mm(a, b, *, ta=False, tb=False, out_dtype, res=None, after=None, name):
    M, K = (a.shape[1], a.shape[0]) if ta else a.shape
    N = b.shape[0] if tb else b.shape[1]
    tm = _pick(M, (1056, 1024, 512, 256, 128))
    tn = _pick(N, (1024, 512, 384, 256, 128))
    tk = _pick(K, (1056, 1024, 512, 384, 256, 128) if ta else (2048, 1024, 512, 384, 256, 128))
    nk = K // tk
    a_dims = (((0,), (1 if tb else 0,)), ((), ())) if ta else (((1,), (1 if tb else 0,)), ((), ()))

    def body(*refs):
        if after is not None:
            refs = refs[:-3] + refs[-2:]
        if res is None:
            a_ref, b_ref, o_ref, acc = refs
        else:
            a_ref, b_ref, r_ref, o_ref, acc = refs
        k = pl.program_id(2)

        @pl.when(k == 0)
        def _():
            acc[...] = jnp.zeros_like(acc)

        acc[...] += lax.dot_general(a_ref[...].astype(CD), b_ref[...].astype(CD), a_dims, preferred_element_type=F32)

        @pl.when(k == nk - 1)
        def _():
            r = acc[...]
            if res is not None:
                r = r + r_ref[...]
            o_ref[...] = r.astype(out_dtype)

    a_spec = pl.BlockSpec((tk, tm), lambda i, j, k: (k, i)) if ta else pl.BlockSpec((tm, tk), lambda i, j, k: (i, k))
    b_spec = pl.BlockSpec((tn, tk), lambda i, j, k: (j, k)) if tb else pl.BlockSpec((tk, tn), lambda i, j, k: (k, j))
    o_spec = pl.BlockSpec((tm, tn), lambda i, j, k: (i, j))
    in_specs, args = [a_spec, b_spec], [a, b]
    if res is not None:
        in_specs.append(o_spec)
        args.append(res)
    if after is not None:
        in_specs.append(pl.BlockSpec(memory_space=pl.ANY))
        args.append(after)
    return pl.pallas_call(
        body, grid=(M // tm, N // tn, nk), in_specs=in_specs, out_specs=o_spec,
        out_shape=jax.ShapeDtypeStruct((M, N), out_dtype), scratch_shapes=[pltpu.VMEM((tm, tn), F32)],
        compiler_params=_cp("parallel", "parallel", "arbitrary"), name=name,
    )(*args)


def _rms_fwd(x, g, *, cb, width, tr, name):
    rows = x.shape[0]

    def body(x_ref, g_ref, o_ref):
        xv = x_ref[...].astype(F32)
        r = lax.rsqrt(jnp.mean(xv * xv, axis=-1, keepdims=True) + EPS)
        o_ref[...] = (xv * r * g_ref[...]).astype(o_ref.dtype)

    return pl.pallas_call(
        body, grid=(rows // tr,),
        in_specs=[pl.BlockSpec((tr, width), lambda i: (i, cb)), pl.BlockSpec((1, width), lambda i: (0, 0))],
        out_specs=pl.BlockSpec((tr, width), lambda i: (i, 0)),
        out_shape=jax.ShapeDtypeStruct((rows, width), CD), compiler_params=_cp("parallel"), name=name,
    )(x, g)


def _rms_bwd(x, g, dy, *, cb, width, tr, res=None, out_dtype, name):
    rows = x.shape[0]

    def body(*refs):
        if res is None:
            x_ref, g_ref, dy_ref, dx_ref, dg_ref = refs
        else:
            x_ref, g_ref, dy_ref, r_ref, dx_ref, dg_ref = refs

        @pl.when(pl.program_id(0) == 0)
        def _():
            dg_ref[...] = jnp.zeros_like(dg_ref)

        xv = x_ref[...].astype(F32)
        r = lax.rsqrt(jnp.mean(xv * xv, axis=-1, keepdims=True) + EPS)
        xh = xv * r
        dyv = dy_ref[...].astype(F32)
        dxh = dyv * g_ref[...]
        dx = r * (dxh - xh * jnp.mean(dxh * xh, axis=-1, keepdims=True))
        if res is not None:
            dx = dx + r_ref[...]
        dx_ref[...] = dx.astype(out_dtype)
        dg_ref[...] += jnp.sum(dyv * xh, axis=0, keepdims=True)

    blk = pl.BlockSpec((tr, width), lambda i: (i, 0))
    in_specs = [pl.BlockSpec((tr, width), lambda i: (i, cb)), pl.BlockSpec((1, width), lambda i: (0, 0)), blk]
    args = [x, g, dy]
    if res is not None:
        in_specs.append(blk)
        args.append(res)
    return pl.pallas_call(
        body, grid=(rows // tr,), in_specs=in_specs,
        out_specs=[blk, pl.BlockSpec((1, width), lambda i: (0, 0))],
        out_shape=[jax.ShapeDtypeStruct((rows, width), out_dtype), jax.ShapeDtypeStruct((1, width), F32)],
        compiler_params=_cp("arbitrary"), name=name,
    )(*args)


def _sigmoid(z):
    return 1.0 / (1.0 + jnp.exp(-z))


def _gate_fwd(o_mla, o_sb, proj, g_mla, g_sb, *, zb_mla, zb_sb, tr, name):
    rows, W = o_mla.shape

    def body(om_ref, os_ref, zm_ref, zs_ref, gm_ref, gs_ref, y_ref):
        for half, (o_ref, z_ref, g_ref) in enumerate(((om_ref, zm_ref, gm_ref), (os_ref, zs_ref, gs_ref))):
            o = o_ref[...].astype(F32)
            r = lax.rsqrt(jnp.mean(o * o, axis=-1, keepdims=True) + EPS)
            z = z_ref[...].astype(F32)
            y_ref[:, half * W:(half + 1) * W] = ((o * r * g_ref[...]) * (z * _sigmoid(z))).astype(y_ref.dtype)

    blk = pl.BlockSpec((tr, W), lambda i: (i, 0))
    gsp = pl.BlockSpec((1, W), lambda i: (0, 0))
    return pl.pallas_call(
        body, grid=(rows // tr,),
        in_specs=[blk, blk, pl.BlockSpec((tr, W), lambda i: (i, zb_mla)), pl.BlockSpec((tr, W), lambda i: (i, zb_sb)), gsp, gsp],
        out_specs=pl.BlockSpec((tr, 2 * W), lambda i: (i, 0)),
        out_shape=jax.ShapeDtypeStruct((rows, 2 * W), CD), compiler_params=_cp("parallel"), name=name,
    )(o_mla, o_sb, proj, proj, g_mla, g_sb)


def _gate_bwd(dy, o_mla, o_sb, proj, g_mla, g_sb, *, zb_mla, zb_sb, tr, name):
    rows, W = o_mla.shape

    def body(dy_ref, om_ref, os_ref, zm_ref, zs_ref, gm_ref, gs_ref, dom_ref, dos_ref, dzm_ref, dzs_ref, dgm_ref, dgs_ref):
        @pl.when(pl.program_id(0) == 0)
        def _():
            dgm_ref[...] = jnp.zeros_like(dgm_ref)
            dgs_ref[...] = jnp.zeros_like(dgs_ref)

        halves = ((om_ref, zm_ref, gm_ref, dom_ref, dzm_ref, dgm_ref), (os_ref, zs_ref, gs_ref, dos_ref, dzs_ref, dgs_ref))
        for half, (o_ref, z_ref, g_ref, do_ref, dz_ref, dg_ref) in enumerate(halves):
            dyv = dy_ref[:, half * W:(half + 1) * W].astype(F32)
            o = o_ref[...].astype(F32)
            r = lax.rsqrt(jnp.mean(o * o, axis=-1, keepdims=True) + EPS)
            oh = o * r
            g = g_ref[...]
            z = z_ref[...].astype(F32)
            sg = _sigmoid(z)
            dn = dyv * (z * sg)
            dz_ref[...] = (dyv * (oh * g) * (sg * (1.0 + z * (1.0 - sg)))).astype(dz_ref.dtype)
            dxh = dn * g
            do_ref[...] = (r * (dxh - oh * jnp.mean(dxh * oh, axis=-1, keepdims=True))).astype(do_ref.dtype)
            dg_ref[...] += jnp.sum(dn * oh, axis=0, keepdims=True)

    blk = pl.BlockSpec((tr, W), lambda i: (i, 0))
    gsp = pl.BlockSpec((1, W), lambda i: (0, 0))
    act = jax.ShapeDtypeStruct((rows, W), CD)
    gsh = jax.ShapeDtypeStruct((1, W), F32)
    return pl.pallas_call(
        body, grid=(rows // tr,),
        in_specs=[pl.BlockSpec((tr, 2 * W), lambda i: (i, 0)), blk, blk,
                  pl.BlockSpec((tr, W), lambda i: (i, zb_mla)), pl.BlockSpec((tr, W), lambda i: (i, zb_sb)), gsp, gsp],
        out_specs=[blk, blk, blk, blk, gsp, gsp], out_shape=[act, act, act, act, gsh, gsh],
        compiler_params=_cp("arbitrary"), name=name,
    )(dy, o_mla, o_sb, proj, proj, g_mla, g_sb)


QKW = 2 * LANES


def _rope_fwd(q_raw, kv, proj, cs, sn, *, H, krb, tr, name):
    rows = q_raw.shape[0]

    def body(q_ref, kn_ref, kr_ref, c_ref, s_ref, qo_ref, ko_ref):
        c, s = c_ref[...], s_ref[...]
        xk = kr_ref[...].astype(F32)
        kr = (xk * c + pltpu.roll(xk, ROPE, 1) * s).astype(ko_ref.dtype)
        for h in range(H):
            nope, rope = slice(h * QKW, h * QKW + LANES), slice(h * QKW + LANES, (h + 1) * QKW)
            qo_ref[:, nope] = q_ref[:, nope].astype(qo_ref.dtype)
            xh = q_ref[:, rope]
            qo_ref[:, rope] = (xh * c + pltpu.roll(xh, ROPE, 1) * s).astype(qo_ref.dtype)
            ko_ref[:, nope] = kn_ref[:, _head(h)]
            ko_ref[:, rope] = kr

    tab = pl.BlockSpec((tr, LANES), lambda i: (i, 0))
    wide = pl.BlockSpec((tr, H * QKW), lambda i: (i, 0))
    sh = jax.ShapeDtypeStruct((rows, H * QKW), CD)
    return pl.pallas_call(
        body, grid=(rows // tr,),
        in_specs=[wide, pl.BlockSpec((tr, H * LANES), lambda i: (i, 0)), pl.BlockSpec((tr, LANES), lambda i: (i, krb)), tab, tab],
        out_specs=[wide, wide], out_shape=[sh, sh], compiler_params=_cp("parallel"), name=name,
    )(q_raw, kv, proj, cs, sn)


def _rope_bwd(dq, dkcat, dv, cs, sn, *, H, tr, name):
    rows = dq.shape[0]
    HW = H * LANES

    def body(dq_ref, dk_ref, dv_ref, c_ref, s_ref, dqo_ref, dkv_ref, dkr_ref):
        c, s = c_ref[...], s_ref[...]
        dkr = jnp.zeros((tr, LANES), F32)
        for h in range(H):
            nope, rope = slice(h * QKW, h * QKW + LANES), slice(h * QKW + LANES, (h + 1) * QKW)
            dqo_ref[:, nope] = dq_ref[:, nope]
            d = dq_ref[:, rope].astype(F32)
            dqo_ref[:, rope] = (d * c + pltpu.roll(d * s, ROPE, 1)).astype(dqo_ref.dtype)
            dkv_ref[:, _head(h)] = dk_ref[:, nope].astype(dkv_ref.dtype)
            dkr = dkr + dk_ref[:, rope]
        dkv_ref[:, HW:] = dv_ref[...].astype(dkv_ref.dtype)
        dkr_ref[...] = (dkr * c + pltpu.roll(dkr * s, ROPE, 1)).astype(dkr_ref.dtype)

    tab = pl.BlockSpec((tr, LANES), lambda i: (i, 0))
    wide = pl.BlockSpec((tr, H * QKW), lambda i: (i, 0))
    return pl.pallas_call(
        body, grid=(rows // tr,), in_specs=[wide, wide, pl.BlockSpec((tr, HW), lambda i: (i, 0)), tab, tab],
        out_specs=[wide, wide, tab],
        out_shape=[jax.ShapeDtypeStruct((rows, H * QKW), CD), jax.ShapeDtypeStruct((rows, 2 * HW), CD), jax.ShapeDtypeStruct((rows, LANES), CD)],
        compiler_params=_cp("parallel"), name=name,
    )(dq, dkcat, dv, cs, sn)


def _loss_head(h, g, tgt, *, lo, hi, tr, name):
    rows, D = h.shape

    def body(h_ref, g_ref, t_ref, dh_ref, dg_ref, loss_ref):
        i = pl.program_id(0)

        @pl.when(i == 0)
        def _():
            dg_ref[...] = jnp.zeros_like(dg_ref)
            loss_ref[...] = jnp.zeros_like(loss_ref)

        xv = h_ref[...]
        r = lax.rsqrt(jnp.mean(xv * xv, axis=-1, keepdims=True) + EPS)
        xh = xv * r
        g_ = g_ref[...]
        rowid = i * tr + lax.broadcasted_iota(jnp.int32, (tr, 1), 0)
        valid = jnp.logical_and(rowid >= lo, rowid < hi)
        err = jnp.where(valid, xh * g_ - t_ref[...], 0.0)
        loss_ref[...] += jnp.sum(jnp.sum(err * err, axis=-1, keepdims=True), axis=0, keepdims=True) * (0.5 / D)
        dy = err * (1.0 / D)
        dxh = dy * g_
        dh_ref[...] = r * (dxh - xh * jnp.mean(dxh * xh, axis=-1, keepdims=True))
        dg_ref[...] += jnp.sum(dy * xh, axis=0, keepdims=True)

    blk = pl.BlockSpec((tr, D), lambda i: (i, 0))
    gsp = pl.BlockSpec((1, D), lambda i: (0, 0))
    return pl.pallas_call(
        body, grid=(rows // tr,), in_specs=[blk, gsp, blk],
        out_specs=[blk, gsp, pl.BlockSpec((1, 1), lambda i: (0, 0))],
        out_shape=[jax.ShapeDtypeStruct((rows, D), F32), jax.ShapeDtypeStruct((1, D), F32), jax.ShapeDtypeStruct((1, 1), F32)],
        compiler_params=_cp("arbitrary"), name=name,
    )(h, g, tgt)


HP = 2
HPW = HP * LANES


def _head(hh):
    return slice(hh * LANES, (hh + 1) * LANES)


def _qk(hh):
    return slice(hh * QKW, (hh + 1) * QKW)


def _mla_fwd(q, kcat, kv, *, H, bq, scale, name):
    rows = q.shape[0]

    def body(q_ref, k_ref, v_ref, o_ref, lse_ref):
        i = pl.program_id(1)
        causal = lax.broadcasted_iota(jnp.int32, (bq, bq), 1) <= lax.broadcasted_iota(jnp.int32, (bq, bq), 0)

        def block(j, carry, diag):
            off = pl.multiple_of(j * bq, bq)
            out = []
            for hh in range(HP):
                m, l, acc = carry[hh]
                s = _dot_nt(q_ref[:, _qk(hh)], k_ref[pl.ds(off, bq), _qk(hh)]) * scale
                if diag:
                    s = jnp.where(causal, s, NEG)
                m_new = jnp.maximum(m, jnp.max(s, axis=-1, keepdims=True))
                p = jnp.exp(s - m_new)
                alpha = jnp.exp(m - m_new)
                l = alpha * l + jnp.sum(p, axis=-1, keepdims=True)
                acc = alpha * acc + _dot(p.astype(CD), v_ref[pl.ds(off, bq), _head(hh)])
                out.append((m_new, l, acc))
            return tuple(out)

        init = tuple((jnp.full((bq, 1), NEG, F32), jnp.zeros((bq, 1), F32), jnp.zeros((bq, LANES), F32)) for _ in range(HP))
        carry = lax.fori_loop(0, i, lambda j, cr: block(j, cr, False), init)
        carry = block(i, carry, True)
        for hh in range(HP):
            m, l, acc = carry[hh]
            o_ref[:, _head(hh)] = (acc / l).astype(o_ref.dtype)
            lse_ref[hh] = m + jnp.log(l)

    return pl.pallas_call(
        body, grid=(H // HP, rows // bq),
        in_specs=[pl.BlockSpec((bq, HP * QKW), lambda h, i: (i, h)), pl.BlockSpec((rows, HP * QKW), lambda h, i: (0, h)),
                  pl.BlockSpec((rows, HPW), lambda h, i: (0, H // HP + h))],
        out_specs=[pl.BlockSpec((bq, HPW), lambda h, i: (i, h)), pl.BlockSpec((HP, bq, 1), lambda h, i: (h, i, 0))],
        out_shape=[jax.ShapeDtypeStruct((rows, H * LANES), CD), jax.ShapeDtypeStruct((H, rows, 1), F32)],
        compiler_params=_cp("arbitrary", "arbitrary"), name=name,
    )(q, kcat, kv)


def _mla_bwd(q, kcat, kv, o, do, lse, *, H, bq, scale, name):
    rows = q.shape[0]

    def body(q_ref, k_ref, v_ref, o_ref, do_ref, lse_ref, dq_ref, dk_ref, dv_ref):
        i = pl.program_id(1)

        @pl.when(i == 0)
        def _():
            dk_ref[...] = jnp.zeros_like(dk_ref)
            dv_ref[...] = jnp.zeros_like(dv_ref)

        causal = lax.broadcasted_iota(jnp.int32, (bq, bq), 1) <= lax.broadcasted_iota(jnp.int32, (bq, bq), 0)
        delta = [jnp.sum(do_ref[:, _head(hh)].astype(F32) * o_ref[:, _head(hh)].astype(F32), axis=-1, keepdims=True) for hh in range(HP)]

        def block(j, carry, diag):
            off = pl.multiple_of(j * bq, bq)
            out = []
            for hh in range(HP):
                q_, do_ = q_ref[:, _qk(hh)], do_ref[:, _head(hh)]
                kj, vj = k_ref[pl.ds(off, bq), _qk(hh)], v_ref[pl.ds(off, bq), _head(hh)]
                p = jnp.exp(_dot_nt(q_, kj) * scale - lse_ref[hh])
                if diag:
                    p = jnp.where(causal, p, 0.0)
                ds = (p * (_dot_nt(do_, vj) - delta[hh]) * scale).astype(CD)
                dk_ref[pl.ds(off, bq), _qk(hh)] += _dot_tn(ds, q_)
                dv_ref[pl.ds(off, bq), _head(hh)] += _dot_tn(p.astype(CD), do_)
                out.append(carry[hh] + _dot(ds, kj))
            return tuple(out)

        init = tuple(jnp.zeros((bq, QKW), F32) for _ in range(HP))
        carry = lax.fori_loop(0, i, lambda j, cr: block(j, cr, False), init)
        carry = block(i, carry, True)
        for hh in range(HP):
            dq_ref[:, _qk(hh)] = carry[hh].astype(dq_ref.dtype)

    qb = pl.BlockSpec((bq, HP * QKW), lambda h, i: (i, h))
    kb = pl.BlockSpec((rows, HP * QKW), lambda h, i: (0, h))
    ob = pl.BlockSpec((bq, HPW), lambda h, i: (i, h))
    return pl.pallas_call(
        body, grid=(H // HP, rows // bq),
        in_specs=[qb, kb, pl.BlockSpec((rows, HPW), lambda h, i: (0, H // HP + h)), ob, ob, pl.BlockSpec((HP, bq, 1), lambda h, i: (h, i, 0))],
        out_specs=[qb, kb, pl.BlockSpec((rows, HPW), lambda h, i: (0, h))],
        out_shape=[jax.ShapeDtypeStruct((rows, H * QKW), CD), jax.ShapeDtypeStruct((rows, H * QKW), F32), jax.ShapeDtypeStruct((rows, H * LANES), F32)],
        compiler_params=pltpu.CompilerParams(dimension_semantics=("arbitrary", "arbitrary"), vmem_limit_bytes=VMEM_LIMIT_WIDE), name=name,
    )(q, kcat, kv, o, do, lse)


def _log_sigmoids(z):
    lb = jnp.minimum(z, 0.0) - jnp.log(1.0 + jnp.exp(-jnp.abs(z)))
    return lb, lb - z


def _split(x):
    hi = x.astype(CD)
    return hi, (x - hi.astype(F32)).astype(CD)


def _rounded(x):
    hi, lo = _split(x)
    return hi.astype(F32) + lo.astype(F32)


def _tri(ck, rel, stack=1):
    j, s = np.arange(ck)[:, None], np.arange(ck)[None, :]
    return jnp.asarray(np.tile({"gt": j > s, "le": j <= s, "lt": j < s}[rel], (stack, 1)), CD)


def _cumsum_mm(hi, lo, tri2):
    return _dot(jnp.concatenate([hi, lo], axis=1), tri2)


def _sb_fwd(proj, *, H, qb0, kb0, vb0, bq, ck, scale, name):
    rows = proj.shape[0]
    nsub = bq // ck

    def body(q_ref, k_ref, v_ref, u_ref, o_ref, t_ref):
        i = pl.program_id(1)
        u = u_ref[...]
        strict = lax.broadcasted_iota(jnp.int32, (bq, bq), 1) < lax.broadcasted_iota(jnp.int32, (bq, bq), 0)

        def block(j, carry, diag):
            off = pl.multiple_of(j * bq, bq)
            out = []
            for hh in range(HP):
                c, acc = carry[hh]
                z = _dot_nt(q_ref[:, _head(hh)], k_ref[pl.ds(off, bq), _head(hh)]) * scale
                lb, lom = _log_sigmoids(z)
                if diag:
                    lom = jnp.where(strict, lom, 0.0)
                hi, lo = _split(lom)
                parts = [None] * nsub
                for s in reversed(range(nsub)):
                    ss = slice(s * ck, (s + 1) * ck)
                    rin = _cumsum_mm(hi[:, ss], lo[:, ss], u)
                    parts[s] = jnp.exp(lb[:, ss] + rin + c)
                    c = c + rin[:, :1] + _rounded(lom[:, s * ck:s * ck + 1])
                a = parts[0] if nsub == 1 else jnp.concatenate(parts, axis=1)
                if diag:
                    a = jnp.where(strict, a, 0.0)
                out.append((c, acc + _dot(a.astype(CD), v_ref[pl.ds(off, bq), _head(hh)])))
            return tuple(out)

        init = tuple((jnp.zeros((bq, 1), F32), jnp.zeros((bq, LANES), F32)) for _ in range(HP))
        carry = block(i, init, True)
        carry = lax.fori_loop(0, i, lambda jj, cr: block(i - 1 - jj, cr, False), carry)
        for hh in range(HP):
            o_ref[:, _head(hh)] = carry[hh][1].astype(o_ref.dtype)
            t_ref[hh] = carry[hh][0]

    kb = lambda off: pl.BlockSpec((rows, HPW), lambda h, i: (0, off // HP + h))
    return pl.pallas_call(
        body, grid=(H // HP, rows // bq),
        in_specs=[pl.BlockSpec((bq, HPW), lambda h, i: (i, qb0 // HP + h)), kb(kb0), kb(vb0), pl.BlockSpec((2 * ck, ck), lambda h, i: (0, 0))],
        out_specs=[pl.BlockSpec((bq, HPW), lambda h, i: (i, h)), pl.BlockSpec((HP, bq, 1), lambda h, i: (h, i, 0))],
        out_shape=[jax.ShapeDtypeStruct((rows, H * LANES), CD), jax.ShapeDtypeStruct((H, rows, 1), F32)],
        compiler_params=_cp("arbitrary", "arbitrary"), name=name,
    )(proj, proj, proj, _tri(ck, "gt", 2))


def _sb_bwd(proj, do, tot, *, H, qb0, kb0, vb0, bq, ck, scale, name):
    rows = proj.shape[0]
    nsub = bq // ck

    def body(q_ref, k_ref, v_ref, do_ref, t_ref, ule_ref, ult_ref, dq_ref, dk_ref, dv_ref):
        i = pl.program_id(1)

        @pl.when(i == 0)
        def _():
            dk_ref[...] = jnp.zeros_like(dk_ref)
            dv_ref[...] = jnp.zeros_like(dv_ref)

        ule, ult = ule_ref[...], ult_ref[...]
        strict = lax.broadcasted_iota(jnp.int32, (bq, bq), 1) < lax.broadcasted_iota(jnp.int32, (bq, bq), 0)

        def block(j, carry, diag):
            off = pl.multiple_of(j * bq, bq)
            out = []
            for hh in range(HP):
                pc, gc, dq = carry[hh]
                q_, do_, tot_ = q_ref[:, _head(hh)], do_ref[:, _head(hh)], t_ref[hh]
                kj, vj = k_ref[pl.ds(off, bq), _head(hh)], v_ref[pl.ds(off, bq), _head(hh)]
                z = _dot_nt(q_, kj) * scale
                lb, lom = _log_sigmoids(z)
                if diag:
                    lom = jnp.where(strict, lom, 0.0)
                hi, lo = _split(lom)
                parts = []
                for s in range(nsub):
                    ss = slice(s * ck, (s + 1) * ck)
                    pin = _cumsum_mm(hi[:, ss], lo[:, ss], ule)
                    parts.append(jnp.exp(lb[:, ss] + (tot_ - (pc + pin))))
                    pc = pc + pin[:, ck - 1:ck]
                a = parts[0] if nsub == 1 else jnp.concatenate(parts, axis=1)
                if diag:
                    a = jnp.where(strict, a, 0.0)
                g = a * _dot_nt(do_, vj)
                gb = g.astype(CD)
                parts = []
                for s in range(nsub):
                    ss = slice(s * ck, (s + 1) * ck)
                    gin = _dot(gb[:, ss], ult)
                    parts.append(gc + gin)
                    gc = gc + gin[:, ck - 1:ck] + g[:, (s + 1) * ck - 1:(s + 1) * ck].astype(CD).astype(F32)
                big_g = parts[0] if nsub == 1 else jnp.concatenate(parts, axis=1)
                sg = jnp.exp(lb)
                dz = (g * (1.0 - sg) - big_g * sg) * scale
                if diag:
                    dz = jnp.where(strict, dz, 0.0)
                dz = dz.astype(CD)
                dk_ref[pl.ds(off, bq), _head(hh)] += _dot_tn(dz, q_)
                dv_ref[pl.ds(off, bq), _head(hh)] += _dot_tn(a.astype(CD), do_)
                out.append((pc, gc, dq + _dot(dz, kj)))
            return tuple(out)

        init = tuple((jnp.zeros((bq, 1), F32), jnp.zeros((bq, 1), F32), jnp.zeros((bq, LANES), F32)) for _ in range(HP))
        carry = lax.fori_loop(0, i, lambda j, cr: block(j, cr, False), init)
        carry = block(i, carry, True)
        for hh in range(HP):
            dq_ref[:, _head(hh)] = carry[hh][2].astype(dq_ref.dtype)

    kb = lambda off: pl.BlockSpec((rows, HPW), lambda h, i: (0, off // HP + h))
    ob = pl.BlockSpec((bq, HPW), lambda h, i: (i, h))
    tri = pl.BlockSpec((ck, ck), lambda h, i: (0, 0))
    acc = jax.ShapeDtypeStruct((rows, H * LANES), F32)
    return pl.pallas_call(
        body, grid=(H // HP, rows // bq),
        in_specs=[pl.BlockSpec((bq, HPW), lambda h, i: (i, qb0 // HP + h)), kb(kb0), kb(vb0), ob,
                  pl.BlockSpec((HP, bq, 1), lambda h, i: (h, i, 0)), pl.BlockSpec((2 * ck, ck), lambda h, i: (0, 0)), tri],
        out_specs=[ob, kb(0), kb(0)],
        out_shape=[jax.ShapeDtypeStruct((rows, H * LANES), CD), acc, acc],
        compiler_params=_cp("arbitrary", "arbitrary"), name=name,
    )(proj, proj, proj, do, tot, _tri(ck, "le", 2), _tri(ck, "lt"))


def _dev_index(p):
    return 4 * p[0] + 2 * p[1] + p[2]


def _all_gather(arrays, *, name):
    A = len(arrays)

    def body(*refs):
        ins, outs = refs[:A], refs[A:2 * A]
        send_sems, recv_sems, local_sems = refs[2 * A:]
        x, y, c = lax.axis_index("x"), lax.axis_index("y"), lax.axis_index("c")
        me, sibling = (x, y, c), (x, y, 1 - c)
        chips = [(1 - x, y), (x, 1 - y), (1 - x, 1 - y)]

        def copy(a, k, block, to, src=None):
            dst = outs[a].at[_dev_index(block)]
            return pltpu.make_async_remote_copy(
                src_ref=dst if src is None else src, dst_ref=dst, send_sem=send_sems.at[a * 7 + k],
                recv_sem=recv_sems.at[a * 7 + k], device_id=to, device_id_type=MESH)

        mine = [pltpu.make_async_copy(ins[a], outs[a].at[_dev_index(me)], local_sems.at[a]) for a in range(A)]
        for cp in mine:
            cp.start()
        first = []
        for a in range(A):
            first.append(copy(a, 0, me, sibling, src=ins[a]))
            first += [copy(a, 1 + j, me, (*chip, c), src=ins[a]) for j, chip in enumerate(chips)]
        for cp in first:
            cp.start()
        passed = []
        for j, chip in enumerate(chips):
            for a in range(A):
                copy(a, 1 + j, (*chip, c), me).wait_recv()
                fwd = copy(a, 4 + j, (*chip, c), sibling)
                fwd.start()
                passed.append(fwd)
        for a in range(A):
            copy(a, 0, sibling, me).wait_recv()
            for j, chip in enumerate(chips):
                copy(a, 4 + j, (*chip, 1 - c), me).wait_recv()
        for cp in first + passed:
            cp.wait_send()
        for cp in mine:
            cp.wait()

    any_spec = pl.BlockSpec(memory_space=pl.ANY)
    return pl.pallas_call(
        body, in_specs=[any_spec] * A, out_specs=[any_spec] * A,
        out_shape=[jax.ShapeDtypeStruct((N_DEV,) + a.shape, a.dtype) for a in arrays],
        scratch_shapes=[pltpu.SemaphoreType.DMA((7 * A,)), pltpu.SemaphoreType.DMA((7 * A,)), pltpu.SemaphoreType.DMA((A,))],
        name=name,
    )(*arrays)


_HBM = pl.BlockSpec(memory_space=pltpu.HBM)
_SEM = pl.BlockSpec(memory_space=pltpu.SEMAPHORE)
_EFFECT = pltpu.SideEffectType.DATAFLOW_SIDE_EFFECTING
_PEER_ORDER = (0, 1, 3, 2, 4, 5, 6)


def _peers():
    x, y, c = lax.axis_index("x"), lax.axis_index("y"), lax.axis_index("c")
    return _dev_index((x, y, c)), [((1 - x) if r & 4 else x, (1 - y) if r & 2 else y, (1 - c) if r & 1 else c) for r in range(1, N_DEV)]


def _xchg_copy(src, land, sends, recvs, a, k, me_i, peers, scatter, arriving):
    peer_i = _dev_index(peers[k])
    return pltpu.make_async_remote_copy(
        src_ref=src.at[peer_i] if scatter else src, dst_ref=land.at[peer_i if arriving else me_i],
        send_sem=sends.at[a * 7 + k], recv_sem=recvs.at[a * 7 + k], device_id=peers[k], device_id_type=MESH)


def _xchg_start(arrays, *, scatter, name):
    A = len(arrays)
    lands = [lax.empty(a.shape if scatter else (N_DEV,) + a.shape, a.dtype) for a in arrays]

    def body(*refs):
        srcs, land = refs[:A], refs[A:2 * A]
        sends, recvs, token = refs[2 * A], refs[2 * A + 1], refs[-1]
        me_i, peers = _peers()
        for k in _PEER_ORDER:
            for a in range(A):
                _xchg_copy(srcs[a], land[a], sends, recvs, a, k, me_i, peers, scatter, False).start()
        token[...] = jnp.zeros_like(token)

    hbm = lambda a: pltpu.HBM(a.shape, a.dtype)
    outs = pl.pallas_call(
        body, name=name, in_specs=[_HBM] * (2 * A),
        out_shape=(pltpu.SemaphoreType.DMA((7 * A,)), pltpu.SemaphoreType.DMA((7 * A,)), *[hbm(a) for a in arrays],
                   *[hbm(a) for a in lands], jax.ShapeDtypeStruct((8, LANES), F32)),
        out_specs=(_SEM, _SEM, *([_HBM] * (2 * A)), pl.BlockSpec(memory_space=pltpu.VMEM)),
        input_output_aliases={n: 2 + n for n in range(2 * A)},
        compiler_params=pltpu.CompilerParams(has_side_effects=_EFFECT),
    )(*[pltpu.with_memory_space_constraint(a, pltpu.HBM) for a in arrays],
      *[pltpu.with_memory_space_constraint(a, pltpu.HBM) for a in lands])
    return outs[0], outs[1], list(outs[2:2 + A]), list(outs[2 + A:2 + 2 * A]), outs[-1]


def _xchg_wait(handle, after, *, scatter, name):
    sends, recvs, srcs, lands, _ = handle
    A = len(srcs)

    def body(*refs):
        src, land = refs[:A], refs[A:2 * A]
        send_sems, recv_sems = refs[2 * A], refs[2 * A + 1]
        me_i, peers = _peers()
        for k in _PEER_ORDER:
            for a in range(A):
                cp = _xchg_copy(src[a], land[a], send_sems, recv_sems, a, k, me_i, peers, scatter, True)
                cp.wait_send()
                cp.wait_recv()

    outs = pl.pallas_call(
        body, name=name, in_specs=[_HBM] * (2 * A) + [_SEM, _SEM, pl.BlockSpec(memory_space=pl.ANY)],
        out_shape=tuple(pltpu.HBM(a.shape, a.dtype) for a in srcs + lands), out_specs=tuple([_HBM] * (2 * A)),
        input_output_aliases={n: n for n in range(2 * A)},
        compiler_params=pltpu.CompilerParams(has_side_effects=_EFFECT),
    )(*srcs, *lands, sends, recvs, after)
    my = _dev_index((lax.axis_index("x"), lax.axis_index("y"), lax.axis_index("c")))
    got = []
    for src, land in zip(outs[:A], outs[A:]):
        own = lax.dynamic_index_in_dim(src, my, 0, keepdims=True) if scatter else src[None]
        got.append(lax.dynamic_update_index_in_dim(land, own, my, 0))
    return got


def _adamw(g, w, m, v):
    m = ADAM_B1 * m + (1.0 - ADAM_B1) * g
    v = ADAM_B2 * v + (1.0 - ADAM_B2) * (g * g)
    delta = -ADAM_LR * ((m / ADAM_BC1) / (jnp.sqrt(v / ADAM_BC2) + ADAM_EPS) + ADAM_WD * w)
    return delta, m, v


def _adam_pieces(pieces, w, m, v, bufs, *, layer, name):
    depth, R, C = w.shape
    tr = _pick(R, [t for t in (512, 256, 128, 64, 32, 16) if t * C * 4 <= 2**20])

    def body(p_ref, w_ref, m_ref, v_ref, b0, b1, b2, b3, g_ref, d_ref, mo_ref, vo_ref):
        g = p_ref[0].astype(F32)
        for k in range(1, N_DEV):
            g = g + p_ref[k].astype(F32)
        g_ref[0] = g
        d_ref[0], mo_ref[0], vo_ref[0] = _adamw(g, w_ref[0], m_ref[0], v_ref[0])

    blk = pl.BlockSpec((1, tr, C), lambda i: (layer, i, 0))
    anyspec = pl.BlockSpec(memory_space=pl.ANY)
    return pl.pallas_call(
        body, grid=(R // tr,),
        in_specs=[pl.BlockSpec((N_DEV, tr, C), lambda i: (0, i, 0)), blk, blk, blk] + [anyspec] * 4,
        out_specs=[blk] * 4, out_shape=[jax.ShapeDtypeStruct((depth, R, C), F32)] * 4,
        input_output_aliases={4: 0, 5: 1, 6: 2, 7: 3}, compiler_params=_cp("parallel"), name=name,
    )(pieces, w, m, v, *bufs)


def _adam_direct(g, w, m, v, *, name):
    sh = jax.ShapeDtypeStruct(w.shape, F32)

    def body(g_ref, w_ref, m_ref, v_ref, d_ref, mo_ref, vo_ref):
        d_ref[...], mo_ref[...], vo_ref[...] = _adamw(g_ref[...], w_ref[...], m_ref[...], v_ref[...])

    return pl.pallas_call(body, out_shape=[sh] * 3, compiler_params=_cp(), name=name)(g, w, m, v)


def _sum_devices(parts, *, name):
    def body(p_ref, o_ref):
        s = p_ref[0]
        for k in range(1, N_DEV):
            s = s + p_ref[k]
        o_ref[...] = s

    return pl.pallas_call(body, out_shape=jax.ShapeDtypeStruct(parts.shape[1:], F32), compiler_params=_cp(), name=name)(parts)


def _rope_cols_pad(w):
    z = jnp.zeros(w.shape[:-1] + (ROPE // 2,), w.dtype)
    return jnp.concatenate([w[..., :ROPE // 2], z, w[..., ROPE // 2:], z], axis=-1)


def _rope_cols_unpad(w):
    return jnp.concatenate([w[..., :ROPE // 2], w[..., ROPE:ROPE + ROPE // 2]], axis=-1)


class _Dims:
    def __init__(self, D, S, NM, QL, KVL, WM, WS):
        self.D, self.S, self.NM, self.QL, self.KVL, self.WM, self.WS = D, S, NM, QL, KVL, WM, WS
        self.H = WM // LANES
        self.HW = self.H * LANES
        self.L = NM + S
        self.bq = 384 if self.L > 1024 else 128
        self.Lp = -(-self.L // self.bq) * self.bq
        self.ck_sb = LANES
        self.tr = 128
        self.sizes = (QL, KVL, ROPE, WM, WS, WS, WS, WS)
        self.d_in = sum(self.sizes)
        front = QL + KVL + LANES
        self.front = -(-front // WM) * WM
        self.pad = self.front - front
        self.dn = self.front + WM + 4 * WS
        assert WM == WS and QL % KVL == 0 and KVL % LANES == 0 and self.d_in % N_DEV == 0 and D % N_DEV == 0
        self.mla_scale = 1.0 / math.sqrt(LANES + ROPE)
        self.sb_scale = 1.0 / math.sqrt(LANES)


def _w_in_nice(d, w):
    parts = jnp.split(w, np.cumsum(d.sizes)[:-1].tolist(), axis=1)
    return jnp.concatenate([parts[0], parts[1], _rope_cols_pad(parts[2]), jnp.zeros((w.shape[0], d.pad), w.dtype)] + parts[3:], axis=1)


def _w_in_orig(d, w):
    a, b = d.QL + d.KVL, d.front
    return jnp.concatenate([w[:, :a], _rope_cols_unpad(w[:, a:a + LANES]), w[:, b:]], axis=1)


def _w_uq_nice(d, w):
    w3 = w.reshape(d.QL, d.H, LANES + ROPE)
    return jnp.concatenate([w3[..., :LANES], _rope_cols_pad(w3[..., LANES:])], axis=-1).reshape(d.QL, d.H * QKW)


def _w_uq_orig(d, w):
    w3 = w.reshape(d.QL, d.H, QKW)
    return jnp.concatenate([w3[..., :LANES], _rope_cols_unpad(w3[..., LANES:])], axis=-1).reshape(d.QL, d.H * (LANES + ROPE))


def _w_ukv_nice(d, w):
    return w.reshape(d.KVL, d.H, 2, LANES).transpose(0, 2, 1, 3).reshape(d.KVL, 2 * d.HW)


def _w_ukv_orig(d, w):
    return w.reshape(d.KVL, 2, d.H, LANES).transpose(0, 2, 1, 3).reshape(d.KVL, 2 * d.HW)


def _cols_from_shards(g):
    return g.transpose(1, 0, 2).reshape(g.shape[1], -1)


def _cols_to_shards(w):
    return w.reshape(w.shape[0], N_DEV, -1).transpose(1, 0, 2)


def _rope_tables(rows):
    inv_freq = ROPE_THETA ** (-jnp.arange(0, ROPE, 2, dtype=F32) / ROPE)
    ang = jnp.arange(rows, dtype=jnp.int32).astype(F32)[:, None] * inv_freq[None, :]
    cos, sin, z = jnp.cos(ang), jnp.sin(ang), jnp.zeros_like(ang)
    return jnp.concatenate([cos, z, cos, z], axis=1), jnp.concatenate([-sin, z, sin, z], axis=1)


def _flat_pack(vecs):
    flat = jnp.concatenate([v.reshape(-1).astype(F32) for v in vecs])
    n = flat.shape[0]
    rows = -(-n // (8 * 1024)) * 8
    offs = np.cumsum([0] + [int(np.prod(v.shape)) for v in vecs])
    return jnp.pad(flat, (0, rows * 1024 - n)).reshape(rows, 1024), offs


def kernel(x, meta_tokens, g_norm, w_in, g_q, g_kv, w_uq, w_ukv, g_out_mla, g_out_sb, w_o, g_final, loss_target, m_meta_tokens, m_g_norm, m_w_in, m_g_q, m_g_kv, m_w_uq, m_w_ukv, m_g_out_mla, m_g_out_sb, m_w_o, m_g_final, v_meta_tokens, v_g_norm, v_w_in, v_g_q, v_g_kv, v_w_uq, v_w_ukv, v_g_out_mla, v_g_out_sb, v_w_o, v_g_final):
    depth = g_norm.shape[0]
    d = _Dims(D=x.shape[-1], S=x.shape[1], NM=meta_tokens.shape[0], QL=g_q.shape[1], KVL=g_kv.shape[1],
              WM=g_out_mla.shape[1], WS=g_out_sb.shape[1])
    D, Lp, H, tr = d.D, d.Lp, d.H, d.tr
    my = _dev_index((lax.axis_index("x"), lax.axis_index("y"), lax.axis_index("c")))

    shards = lambda l: [w_in[l].astype(CD), w_uq[l].astype(CD), w_ukv[l].astype(CD), w_o[l].astype(CD)]
    *gathered0, g_meta = _all_gather(shards(0) + [meta_tokens], name="gather_layer0")
    gathers = {l: _xchg_start(shards(l), scatter=False, name=f"gather_start_{l}") for l in range(1, depth)}
    started = sum(hd[4][0, 0] for hd in gathers.values())

    def relayout(g_in, g_uq, g_ukv, g_o):
        return (_w_in_nice(d, _cols_from_shards(g_in)), _w_uq_nice(d, _cols_from_shards(g_uq)),
                _w_ukv_nice(d, _cols_from_shards(g_ukv)), g_o.reshape(D, D))

    wn, wuq, wukv, wo = [None] * depth, [None] * depth, [None] * depth, [None] * depth
    meta_full = _cols_from_shards(g_meta)

    cs, sn = _rope_tables(Lp)
    h = jnp.concatenate([meta_full, x[0], jnp.zeros((Lp - d.L, D), F32)], axis=0)
    tgt = jnp.concatenate([jnp.zeros((d.NM, D), F32), loss_target[0], jnp.zeros((Lp - d.L, D), F32)], axis=0)

    b128 = lambda col: col // LANES
    krb = b128(d.QL + d.KVL)
    zb_mla, zb_sb = d.front // d.WM, (d.front + d.WM + 3 * d.WS) // d.WS
    qb0, kb0, vb0 = b128(d.front + d.WM), b128(d.front + d.WM + d.WS), b128(d.front + d.WM + 2 * d.WS)
    row = lambda a, i: a[i][None, :]

    saved = []
    for i in range(depth):
        if i == 0:
            wn[i], wuq[i], wukv[i], wo[i] = relayout(*gathered0)
            gain = row(g_norm, i) + started
        else:
            wn[i], wuq[i], wukv[i], wo[i] = relayout(*_xchg_wait(gathers[i], h, scatter=False, name=f"gather_wait_{i}"))
            gain = row(g_norm, i)
        u = _rms_fwd(h, gain, cb=0, width=D, tr=tr, name=f"rms_in_{i}")
        proj = _mm(u, wn[i], out_dtype=CD, name=f"proj_{i}")
        cqn = _rms_fwd(proj, row(g_q, i), cb=0, width=d.QL, tr=tr, name=f"rms_q_{i}")
        ckvn = _rms_fwd(proj, row(g_kv, i), cb=d.QL // d.KVL, width=d.KVL, tr=tr, name=f"rms_kv_{i}")
        q_raw = _mm(cqn, wuq[i], out_dtype=F32, name=f"up_q_{i}")
        kvn = _mm(ckvn, wukv[i], out_dtype=CD, name=f"up_kv_{i}")
        q, kcat = _rope_fwd(q_raw, kvn, proj, cs, sn, H=H, krb=krb, tr=tr, name=f"rope_{i}")
        o_mla, lse = _mla_fwd(q, kcat, kvn, H=H, bq=d.bq, scale=d.mla_scale, name=f"mla_fwd_{i}")
        o_sb, tot = _sb_fwd(proj, H=H, qb0=qb0, kb0=kb0, vb0=vb0, bq=d.bq, ck=d.ck_sb, scale=d.sb_scale, name=f"sb_fwd_{i}")
        y = _gate_fwd(o_mla, o_sb, proj, row(g_out_mla, i), row(g_out_sb, i), zb_mla=zb_mla, zb_sb=zb_sb, tr=tr, name=f"gate_fwd_{i}")
        h_next = _mm(y, wo[i], out_dtype=F32, res=h, name=f"out_proj_{i}")
        saved.append((h, u, proj, cqn, ckvn, q, kvn, kcat, o_mla, lse, o_sb, tot, y))
        h = h_next

    dh, dg_final, loss_part = _loss_head(h, g_final[None, :], tgt, lo=d.NM, hi=d.L, tr=tr, name="loss_head")

    dg_norm, dg_q, dg_kv, dg_om, dg_os = [None] * depth, [None] * depth, [None] * depth, [None] * depth, [None] * depth
    big_w = {"w_in": (w_in, m_w_in, v_w_in), "w_uq": (w_uq, m_w_uq, v_w_uq), "w_ukv": (w_ukv, m_w_ukv, v_w_ukv), "w_o": (w_o, m_w_o, v_w_o)}
    res = {n: [lax.empty(w.shape, F32) for _ in range(4)] for n, (w, _, _) in big_w.items()}
    exchanges = [None] * depth

    def finish(l, after):
        pieces = _xchg_wait(exchanges[l], after, scatter=True, name=f"exch_wait_{l}")
        for (n, (w, m, v)), p in zip(big_w.items(), pieces):
            res[n] = _adam_pieces(p, w, m, v, res[n], layer=l, name=f"adam_{n}_{l}")

    zpad = jnp.zeros((Lp, d.pad), CD)
    for i in reversed(range(depth)):
        h_in, u, proj, cqn, ckvn, q, kvn, kcat, o_mla, lse, o_sb, tot, y = saved[i]
        dy = _mm(dh, wo[i], tb=True, out_dtype=CD, name=f"d_y_{i}")
        dwo = _mm(y, dh, ta=True, out_dtype=F32, name=f"d_wo_{i}")
        do_mla, do_sb, dz_mla, dz_sb, dg_om[i], dg_os[i] = _gate_bwd(
            dy, o_mla, o_sb, proj, row(g_out_mla, i), row(g_out_sb, i), zb_mla=zb_mla, zb_sb=zb_sb, tr=tr, name=f"gate_bwd_{i}")
        dq_mla, dkcat, dv = _mla_bwd(q, kcat, kvn, o_mla, do_mla, lse, H=H, bq=d.bq, scale=d.mla_scale, name=f"mla_bwd_{i}")
        dq_sb, dk_sb, dv_sb = _sb_bwd(proj, do_sb, tot, H=H, qb0=qb0, kb0=kb0, vb0=vb0, bq=d.bq, ck=d.ck_sb, scale=d.sb_scale, name=f"sb_bwd_{i}")
        dq, dkv, dk_r = _rope_bwd(dq_mla, dkcat, dv, cs, sn, H=H, tr=tr, name=f"rope_bwd_{i}")
        dcqn = _mm(dq, wuq[i], tb=True, out_dtype=CD, name=f"d_cq_{i}")
        dwuq = _mm(cqn, dq, ta=True, out_dtype=F32, name=f"d_wuq_{i}")
        dckvn = _mm(dkv, wukv[i], tb=True, out_dtype=CD, name=f"d_ckv_{i}")
        dwukv = _mm(ckvn, dkv, ta=True, out_dtype=F32, name=f"d_wukv_{i}")
        dc_q, dg_q[i] = _rms_bwd(proj, row(g_q, i), dcqn, cb=0, width=d.QL, tr=tr, out_dtype=CD, name=f"rms_q_bwd_{i}")
        dc_kv, dg_kv[i] = _rms_bwd(proj, row(g_kv, i), dckvn, cb=d.QL // d.KVL, width=d.KVL, tr=tr, out_dtype=CD, name=f"rms_kv_bwd_{i}")
        dproj = jnp.concatenate([dc_q, dc_kv, dk_r, zpad, dz_mla, dq_sb, dk_sb.astype(CD), dv_sb.astype(CD), dz_sb], axis=1)
        dwn = _mm(u, dproj, ta=True, out_dtype=F32, name=f"d_wn_{i}")
        exchanges[i] = _xchg_start(
            [_cols_to_shards(_w_in_orig(d, dwn)).astype(CD), _cols_to_shards(_w_uq_orig(d, dwuq)).astype(CD),
             _cols_to_shards(_w_ukv_orig(d, dwukv)).astype(CD), dwo.reshape(N_DEV, D // N_DEV, D).astype(CD)],
            scatter=True, name=f"exch_start_{i}")
        du = _mm(dproj, wn[i], tb=True, out_dtype=CD, after=exchanges[i][4], name=f"d_u_{i}")
        dh, dg_norm[i] = _rms_bwd(h_in, row(g_norm, i), du, cb=0, width=D, tr=tr, res=dh, out_dtype=F32, name=f"rms_in_bwd_{i}")
        if i + 1 < depth:
            finish(i + 1, dh)

    cat = lambda parts: jnp.concatenate(parts, axis=0)
    small, offs = _flat_pack([cat(dg_norm), cat(dg_q), cat(dg_kv), cat(dg_om), cat(dg_os), dg_final, dh[:d.NM], loss_part])
    (small_all,) = _all_gather([small], name="gather_small")
    small_sum = _sum_devices(small_all, name="sum_small").reshape(-1)
    finish(0, small_sum)
    seg = lambda k, shape: small_sum[offs[k]:offs[k + 1]].reshape(shape)
    loss = small_sum[offs[7]]
    grad_x = dh[d.NM:d.L][None]

    g_meta_full = seg(6, (d.NM, D))
    ncol = meta_tokens.shape[1]
    g_meta_mine = lax.dynamic_slice(g_meta_full, (0, my * ncol), (d.NM, ncol))
    res["meta_tokens"] = [g_meta_mine] + list(_adam_direct(g_meta_mine, meta_tokens, m_meta_tokens, v_meta_tokens, name="adam_meta"))

    small_names = ["g_norm", "g_q", "g_kv", "g_out_mla", "g_out_sb", "g_final"]
    small_w = [g_norm, g_q, g_kv, g_out_mla, g_out_sb, g_final]
    small_m = [m_g_norm, m_g_q, m_g_kv, m_g_out_mla, m_g_out_sb, m_g_final]
    small_v = [v_g_norm, v_g_q, v_g_kv, v_g_out_mla, v_g_out_sb, v_g_final]
    n_small = int(offs[6])
    rows_s = -(-n_small // (8 * 1024)) * 8
    g_flat = small_sum[:rows_s * 1024].reshape(rows_s, 1024)
    wf, _ = _flat_pack(small_w)
    mf, _ = _flat_pack(small_m)
    vf, _ = _flat_pack(small_v)
    d_f, m_f, v_f = _adam_direct(g_flat, wf, mf, vf, name="adam_small")
    for k, nm in enumerate(small_names):
        sl = lambda f: f.reshape(-1)[offs[k]:offs[k + 1]].reshape(small_w[k].shape)
        res[nm] = [sl(g_flat), sl(d_f), sl(m_f), sl(v_f)]

    names = ["meta_tokens", "g_norm", "w_in", "g_q", "g_kv", "w_uq", "w_ukv", "g_out_mla", "g_out_sb", "w_o", "g_final"]
    return (loss, grad_x, *[res[n][0] for n in names], *[res[n][1] for n in names],
            *[res[n][2] for n in names], *[res[n][3] for n in names])
```

```python
import math

import numpy as np
import jax
import jax.numpy as jnp
from jax import lax
from jax.experimental import pallas as pl
from jax.experimental.pallas import tpu as pltpu

CD = jnp.bfloat16
F32 = jnp.float32

LANES = 128
N_DEV = 8
EPS = 1e-6
ROPE = 64
ROPE_THETA = 10000.0
NEG = -1e30
VMEM_LIMIT = 48 * 2**20
VMEM_LIMIT_WIDE = 56 * 2**20

ADAM_LR, ADAM_B1, ADAM_B2, ADAM_EPS, ADAM_WD, ADAM_STEP = 0.001, 0.9, 0.999, 1e-08, 0.01, 10
ADAM_BC1 = 1.0 - ADAM_B1**ADAM_STEP
ADAM_BC2 = 1.0 - ADAM_B2**ADAM_STEP

MESH = pl.DeviceIdType.MESH


def _cp(*sem):
    return pltpu.CompilerParams(dimension_semantics=tuple(sem) if sem else None, vmem_limit_bytes=VMEM_LIMIT)


def _pick(dim, prefs):
    for p in prefs:
        if dim % p == 0:
            return p
    raise ValueError(f"no tile for {dim} in {prefs}")


def _dot(a, b):
    return jnp.dot(a, b, preferred_element_type=F32)


def _dot_nt(a, b):
    return lax.dot_general(a, b, (((1,), (1,)), ((), ())), preferred_element_type=F32)


def _dot_tn(a, b):
    return lax.dot_general(a, b, (((0,), (0,)), ((), ())), preferred_element_type=F32)


def _mm(a, b, *, ta=False, tb=False, out_dtype, res=None, after=None, name):
    M, K = (a.shape[1], a.shape[0]) if ta else a.shape
    N = b.shape[0] if tb else b.shape[1]
    tm = _pick(M, (1056, 1024, 512, 256, 128))
    tn = _pick(N, (1024, 512, 384, 256, 128))
    tk = _pick(K, (2112, 1056, 1024, 512, 384, 256, 128) if ta else (2048, 1024, 512, 384, 256, 128))
    nk = K // tk
    a_dims = (((0,), (1 if tb else 0,)), ((), ())) if ta else (((1,), (1 if tb else 0,)), ((), ()))

    def body(*refs):
        if after is not None:
            refs = refs[:-3] + refs[-2:]
        if res is None:
            a_ref, b_ref, o_ref, acc = refs
        else:
            a_ref, b_ref, r_ref, o_ref, acc = refs
        k = pl.program_id(2)

        @pl.when(k == 0)
        def _():
            acc[...] = jnp.zeros_like(acc)

        acc[...] += lax.dot_general(a_ref[...].astype(CD), b_ref[...].astype(CD), a_dims, preferred_element_type=F32)

        @pl.when(k == nk - 1)
        def _():
            r = acc[...]
            if res is not None:
                r = r + r_ref[...]
            o_ref[...] = r.astype(out_dtype)

    a_spec = pl.BlockSpec((tk, tm), lambda i, j, k: (k, i)) if ta else pl.BlockSpec((tm, tk), lambda i, j, k: (i, k))
    b_spec = pl.BlockSpec((tn, tk), lambda i, j, k: (j, k)) if tb else pl.BlockSpec((tk, tn), lambda i, j, k: (k, j))
    o_spec = pl.BlockSpec((tm, tn), lambda i, j, k: (i, j))
    in_specs, args = [a_spec, b_spec], [a, b]
    if res is not None:
        in_specs.append(o_spec)
        args.append(res)
    if after is not None:
        in_specs.append(pl.BlockSpec(memory_space=pl.ANY))
        args.append(after)
    return pl.pallas_call(
        body, grid=(M // tm, N // tn, nk), in_specs=in_specs, out_specs=o_spec,
        out_shape=jax.ShapeDtypeStruct((M, N), out_dtype), scratch_shapes=[pltpu.VMEM((tm, tn), F32)],
        compiler_params=_cp("parallel", "parallel", "arbitrary"), name=name,
    )(*args)


def _rms_fwd(x, g, *, cb, width, tr, name):
    rows = x.shape[0]

    def body(x_ref, g_ref, o_ref):
        xv = x_ref[...].astype(F32)
        r = lax.rsqrt(jnp.mean(xv * xv, axis=-1, keepdims=True) + EPS)
        o_ref[...] = (xv * r * g_ref[...]).astype(o_ref.dtype)

    return pl.pallas_call(
        body, grid=(rows // tr,),
        in_specs=[pl.BlockSpec((tr, width), lambda i: (i, cb)), pl.BlockSpec((1, width), lambda i: (0, 0))],
        out_specs=pl.BlockSpec((tr, width), lambda i: (i, 0)),
        out_shape=jax.ShapeDtypeStruct((rows, width), CD), compiler_params=_cp("parallel"), name=name,
    )(x, g)


def _rms_bwd(x, g, dy, *, cb, width, tr, res=None, out_dtype, name):
    rows = x.shape[0]

    def body(*refs):
        if res is None:
            x_ref, g_ref, dy_ref, dx_ref, dg_ref = refs
        else:
            x_ref, g_ref, dy_ref, r_ref, dx_ref, dg_ref = refs

        @pl.when(pl.program_id(0) == 0)
        def _():
            dg_ref[...] = jnp.zeros_like(dg_ref)

        xv = x_ref[...].astype(F32)
        r = lax.rsqrt(jnp.mean(xv * xv, axis=-1, keepdims=True) + EPS)
        xh = xv * r
        dyv = dy_ref[...].astype(F32)
        dxh = dyv * g_ref[...]
        dx = r * (dxh - xh * jnp.mean(dxh * xh, axis=-1, keepdims=True))
        if res is not None:
            dx = dx + r_ref[...]
        dx_ref[...] = dx.astype(out_dtype)
        dg_ref[...] += jnp.sum(dyv * xh, axis=0, keepdims=True)

    blk = pl.BlockSpec((tr, width), lambda i: (i, 0))
    in_specs = [pl.BlockSpec((tr, width), lambda i: (i, cb)), pl.BlockSpec((1, width), lambda i: (0, 0)), blk]
    args = [x, g, dy]
    if res is not None:
        in_specs.append(blk)
        args.append(res)
    return pl.pallas_call(
        body, grid=(rows // tr,), in_specs=in_specs,
        out_specs=[blk, pl.BlockSpec((1, width), lambda i: (0, 0))],
        out_shape=[jax.ShapeDtypeStruct((rows, width), out_dtype), jax.ShapeDtypeStruct((1, width), F32)],
        compiler_params=_cp("arbitrary"), name=name,
    )(*args)


def _sigmoid(z):
    return 1.0 / (1.0 + jnp.exp(-z))


def _gate_fwd(o_mla, o_sb, proj, g_mla, g_sb, *, zb_mla, zb_sb, tr, name):
    rows, W = o_mla.shape

    def body(om_ref, os_ref, zm_ref, zs_ref, gm_ref, gs_ref, y_ref):
        for half, (o_ref, z_ref, g_ref) in enumerate(((om_ref, zm_ref, gm_ref), (os_ref, zs_ref, gs_ref))):
            o = o_ref[...].astype(F32)
            r = lax.rsqrt(jnp.mean(o * o, axis=-1, keepdims=True) + EPS)
            z = z_ref[...].astype(F32)
            y_ref[:, half * W:(half + 1) * W] = ((o * r * g_ref[...]) * (z * _sigmoid(z))).astype(y_ref.dtype)

    blk = pl.BlockSpec((tr, W), lambda i: (i, 0))
    gsp = pl.BlockSpec((1, W), lambda i: (0, 0))
    return pl.pallas_call(
        body, grid=(rows // tr,),
        in_specs=[blk, blk, pl.BlockSpec((tr, W), lambda i: (i, zb_mla)), pl.BlockSpec((tr, W), lambda i: (i, zb_sb)), gsp, gsp],
        out_specs=pl.BlockSpec((tr, 2 * W), lambda i: (i, 0)),
        out_shape=jax.ShapeDtypeStruct((rows, 2 * W), CD), compiler_params=_cp("parallel"), name=name,
    )(o_mla, o_sb, proj, proj, g_mla, g_sb)


def _gate_bwd(dy, o_mla, o_sb, proj, g_mla, g_sb, *, zb_mla, zb_sb, tr, name):
    rows, W = o_mla.shape

    def body(dy_ref, om_ref, os_ref, zm_ref, zs_ref, gm_ref, gs_ref, dom_ref, dos_ref, dzm_ref, dzs_ref, dgm_ref, dgs_ref):
        @pl.when(pl.program_id(0) == 0)
        def _():
            dgm_ref[...] = jnp.zeros_like(dgm_ref)
            dgs_ref[...] = jnp.zeros_like(dgs_ref)

        halves = ((om_ref, zm_ref, gm_ref, dom_ref, dzm_ref, dgm_ref), (os_ref, zs_ref, gs_ref, dos_ref, dzs_ref, dgs_ref))
        for half, (o_ref, z_ref, g_ref, do_ref, dz_ref, dg_ref) in enumerate(halves):
            dyv = dy_ref[:, half * W:(half + 1) * W].astype(F32)
            o = o_ref[...].astype(F32)
            r = lax.rsqrt(jnp.mean(o * o, axis=-1, keepdims=True) + EPS)
            oh = o * r
            g = g_ref[...]
            z = z_ref[...].astype(F32)
            sg = _sigmoid(z)
            dn = dyv * (z * sg)
            dz_ref[...] = (dyv * (oh * g) * (sg * (1.0 + z * (1.0 - sg)))).astype(dz_ref.dtype)
            dxh = dn * g
            do_ref[...] = (r * (dxh - oh * jnp.mean(dxh * oh, axis=-1, keepdims=True))).astype(do_ref.dtype)
            dg_ref[...] += jnp.sum(dn * oh, axis=0, keepdims=True)

    blk = pl.BlockSpec((tr, W), lambda i: (i, 0))
    gsp = pl.BlockSpec((1, W), lambda i: (0, 0))
    act = jax.ShapeDtypeStruct((rows, W), CD)
    gsh = jax.ShapeDtypeStruct((1, W), F32)
    return pl.pallas_call(
        body, grid=(rows // tr,),
        in_specs=[pl.BlockSpec((tr, 2 * W), lambda i: (i, 0)), blk, blk,
                  pl.BlockSpec((tr, W), lambda i: (i, zb_mla)), pl.BlockSpec((tr, W), lambda i: (i, zb_sb)), gsp, gsp],
        out_specs=[blk, blk, blk, blk, gsp, gsp], out_shape=[act, act, act, act, gsh, gsh],
        compiler_params=_cp("arbitrary"), name=name,
    )(dy, o_mla, o_sb, proj, proj, g_mla, g_sb)


QKW = 2 * LANES


def _rope_fwd(q_raw, kv, proj, cs, sn, *, H, krb, tr, name):
    rows = q_raw.shape[0]

    def body(q_ref, kn_ref, kr_ref, c_ref, s_ref, qo_ref, ko_ref):
        c, s = c_ref[...], s_ref[...]
        xk = kr_ref[...].astype(F32)
        kr = (xk * c + pltpu.roll(xk, ROPE, 1) * s).astype(ko_ref.dtype)
        for h in range(H):
            nope, rope = slice(h * QKW, h * QKW + LANES), slice(h * QKW + LANES, (h + 1) * QKW)
            qo_ref[:, nope] = q_ref[:, nope].astype(qo_ref.dtype)
            xh = q_ref[:, rope]
            qo_ref[:, rope] = (xh * c + pltpu.roll(xh, ROPE, 1) * s).astype(qo_ref.dtype)
            ko_ref[:, nope] = kn_ref[:, _head(h)]
            ko_ref[:, rope] = kr

    tab = pl.BlockSpec((tr, LANES), lambda i: (i, 0))
    wide = pl.BlockSpec((tr, H * QKW), lambda i: (i, 0))
    sh = jax.ShapeDtypeStruct((rows, H * QKW), CD)
    return pl.pallas_call(
        body, grid=(rows // tr,),
        in_specs=[wide, pl.BlockSpec((tr, H * LANES), lambda i: (i, 0)), pl.BlockSpec((tr, LANES), lambda i: (i, krb)), tab, tab],
        out_specs=[wide, wide], out_shape=[sh, sh], compiler_params=_cp("parallel"), name=name,
    )(q_raw, kv, proj, cs, sn)


def _rope_bwd(dq, dkcat, dv, cs, sn, *, H, tr, name):
    rows = dq.shape[0]
    HW = H * LANES

    def body(dq_ref, dk_ref, dv_ref, c_ref, s_ref, dqo_ref, dkv_ref, dkr_ref):
        c, s = c_ref[...], s_ref[...]
        dkr = jnp.zeros((tr, LANES), F32)
        for h in range(H):
            nope, rope = slice(h * QKW, h * QKW + LANES), slice(h * QKW + LANES, (h + 1) * QKW)
            dqo_ref[:, nope] = dq_ref[:, nope]
            d = dq_ref[:, rope].astype(F32)
            dqo_ref[:, rope] = (d * c + pltpu.roll(d * s, ROPE, 1)).astype(dqo_ref.dtype)
            dkv_ref[:, _head(h)] = dk_ref[:, nope].astype(dkv_ref.dtype)
            dkr = dkr + dk_ref[:, rope]
        dkv_ref[:, HW:] = dv_ref[...].astype(dkv_ref.dtype)
        dkr_ref[...] = (dkr * c + pltpu.roll(dkr * s, ROPE, 1)).astype(dkr_ref.dtype)

    tab = pl.BlockSpec((tr, LANES), lambda i: (i, 0))
    wide = pl.BlockSpec((tr, H * QKW), lambda i: (i, 0))
    return pl.pallas_call(
        body, grid=(rows // tr,), in_specs=[wide, wide, pl.BlockSpec((tr, HW), lambda i: (i, 0)), tab, tab],
        out_specs=[wide, wide, tab],
        out_shape=[jax.ShapeDtypeStruct((rows, H * QKW), CD), jax.ShapeDtypeStruct((rows, 2 * HW), CD), jax.ShapeDtypeStruct((rows, LANES), CD)],
        compiler_params=_cp("parallel"), name=name,
    )(dq, dkcat, dv, cs, sn)


def _loss_head(h, g, tgt, *, lo, hi, tr, name):
    rows, D = h.shape

    def body(h_ref, g_ref, t_ref, dh_ref, dg_ref, loss_ref):
        i = pl.program_id(0)

        @pl.when(i == 0)
        def _():
            dg_ref[...] = jnp.zeros_like(dg_ref)
            loss_ref[...] = jnp.zeros_like(loss_ref)

        xv = h_ref[...]
        r = lax.rsqrt(jnp.mean(xv * xv, axis=-1, keepdims=True) + EPS)
        xh = xv * r
        g_ = g_ref[...]
        rowid = i * tr + lax.broadcasted_iota(jnp.int32, (tr, 1), 0)
        valid = jnp.logical_and(rowid >= lo, rowid < hi)
        err = jnp.where(valid, xh * g_ - t_ref[...], 0.0)
        loss_ref[...] += jnp.sum(jnp.sum(err * err, axis=-1, keepdims=True), axis=0, keepdims=True) * (0.5 / D)
        dy = err * (1.0 / D)
        dxh = dy * g_
        dh_ref[...] = r * (dxh - xh * jnp.mean(dxh * xh, axis=-1, keepdims=True))
        dg_ref[...] += jnp.sum(dy * xh, axis=0, keepdims=True)

    blk = pl.BlockSpec((tr, D), lambda i: (i, 0))
    gsp = pl.BlockSpec((1, D), lambda i: (0, 0))
    return pl.pallas_call(
        body, grid=(rows // tr,), in_specs=[blk, gsp, blk],
        out_specs=[blk, gsp, pl.BlockSpec((1, 1), lambda i: (0, 0))],
        out_shape=[jax.ShapeDtypeStruct((rows, D), F32), jax.ShapeDtypeStruct((1, D), F32), jax.ShapeDtypeStruct((1, 1), F32)],
        compiler_params=_cp("arbitrary"), name=name,
    )(h, g, tgt)


HP = 2
HPW = HP * LANES
HP_FWD = 4


def _head(hh):
    return slice(hh * LANES, (hh + 1) * LANES)


def _qk(hh):
    return slice(hh * QKW, (hh + 1) * QKW)


def _mla_fwd(q, kcat, kv, *, H, bq, scale, name):
    rows = q.shape[0]
    HP, HPW = HP_FWD, HP_FWD * LANES

    def body(q_ref, k_ref, v_ref, o_ref, lse_ref):
        i = pl.program_id(1)
        causal = lax.broadcasted_iota(jnp.int32, (bq, bq), 1) <= lax.broadcasted_iota(jnp.int32, (bq, bq), 0)

        def block(j, carry, diag):
            off = pl.multiple_of(j * bq, bq)
            out = []
            for hh in range(HP):
                m, l, acc = carry[hh]
                s = _dot_nt(q_ref[:, _qk(hh)], k_ref[pl.ds(off, bq), _qk(hh)]) * scale
                if diag:
                    s = jnp.where(causal, s, NEG)
                m_new = jnp.maximum(m, jnp.max(s, axis=-1, keepdims=True))
                p = jnp.exp(s - m_new)
                alpha = jnp.exp(m - m_new)
                l = alpha * l + jnp.sum(p, axis=-1, keepdims=True)
                acc = alpha * acc + _dot(p.astype(CD), v_ref[pl.ds(off, bq), _head(hh)])
                out.append((m_new, l, acc))
            return tuple(out)

        init = tuple((jnp.full((bq, 1), NEG, F32), jnp.zeros((bq, 1), F32), jnp.zeros((bq, LANES), F32)) for _ in range(HP))
        carry = lax.fori_loop(0, i, lambda j, cr: block(j, cr, False), init)
        carry = block(i, carry, True)
        for hh in range(HP):
            m, l, acc = carry[hh]
            o_ref[:, _head(hh)] = (acc / l).astype(o_ref.dtype)
            lse_ref[hh] = m + jnp.log(l)

    return pl.pallas_call(
        body, grid=(H // HP, rows // bq),
        in_specs=[pl.BlockSpec((bq, HP * QKW), lambda h, i: (i, h)), pl.BlockSpec((rows, HP * QKW), lambda h, i: (0, h)),
                  pl.BlockSpec((rows, HPW), lambda h, i: (0, H // HP + h))],
        out_specs=[pl.BlockSpec((bq, HPW), lambda h, i: (i, h)), pl.BlockSpec((HP, bq, 1), lambda h, i: (h, i, 0))],
        out_shape=[jax.ShapeDtypeStruct((rows, H * LANES), CD), jax.ShapeDtypeStruct((H, rows, 1), F32)],
        compiler_params=_cp("arbitrary", "arbitrary"), name=name,
    )(q, kcat, kv)


def _mla_bwd(q, kcat, kv, o, do, lse, *, H, bq, scale, name):
    rows = q.shape[0]

    def body(q_ref, k_ref, v_ref, o_ref, do_ref, lse_ref, dq_ref, dk_ref, dv_ref):
        i = pl.program_id(1)

        @pl.when(i == 0)
        def _():
            dk_ref[...] = jnp.zeros_like(dk_ref)
            dv_ref[...] = jnp.zeros_like(dv_ref)

        causal = lax.broadcasted_iota(jnp.int32, (bq, bq), 1) <= lax.broadcasted_iota(jnp.int32, (bq, bq), 0)
        delta = [jnp.sum(do_ref[:, _head(hh)].astype(F32) * o_ref[:, _head(hh)].astype(F32), axis=-1, keepdims=True) for hh in range(HP)]

        def block(j, carry, diag):
            off = pl.multiple_of(j * bq, bq)
            out = []
            for hh in range(HP):
                q_, do_ = q_ref[:, _qk(hh)], do_ref[:, _head(hh)]
                kj, vj = k_ref[pl.ds(off, bq), _qk(hh)], v_ref[pl.ds(off, bq), _head(hh)]
                p = jnp.exp(_dot_nt(q_, kj) * scale - lse_ref[hh])
                if diag:
                    p = jnp.where(causal, p, 0.0)
                ds = (p * (_dot_nt(do_, vj) - delta[hh]) * scale).astype(CD)
                dk_ref[pl.ds(off, bq), _qk(hh)] += _dot_tn(ds, q_)
                dv_ref[pl.ds(off, bq), _head(hh)] += _dot_tn(p.astype(CD), do_)
                out.append(carry[hh] + _dot(ds, kj))
            return tuple(out)

        init = tuple(jnp.zeros((bq, QKW), F32) for _ in range(HP))
        carry = lax.fori_loop(0, i, lambda j, cr: block(j, cr, False), init)
        carry = block(i, carry, True)
        for hh in range(HP):
            dq_ref[:, _qk(hh)] = carry[hh].astype(dq_ref.dtype)

    qb = pl.BlockSpec((bq, HP * QKW), lambda h, i: (i, h))
    kb = pl.BlockSpec((rows, HP * QKW), lambda h, i: (0, h))
    ob = pl.BlockSpec((bq, HPW), lambda h, i: (i, h))
    return pl.pallas_call(
        body, grid=(H // HP, rows // bq),
        in_specs=[qb, kb, pl.BlockSpec((rows, HPW), lambda h, i: (0, H // HP + h)), ob, ob, pl.BlockSpec((HP, bq, 1), lambda h, i: (h, i, 0))],
        out_specs=[qb, kb, pl.BlockSpec((rows, HPW), lambda h, i: (0, h))],
        out_shape=[jax.ShapeDtypeStruct((rows, H * QKW), CD), jax.ShapeDtypeStruct((rows, H * QKW), F32), jax.ShapeDtypeStruct((rows, H * LANES), F32)],
        compiler_params=pltpu.CompilerParams(dimension_semantics=("arbitrary", "arbitrary"), vmem_limit_bytes=VMEM_LIMIT_WIDE), name=name,
    )(q, kcat, kv, o, do, lse)


def _log_sigmoids(z):
    lb = jnp.minimum(z, 0.0) - jnp.log(1.0 + jnp.exp(-jnp.abs(z)))
    return lb, lb - z


def _split(x):
    hi = x.astype(CD)
    return hi, (x - hi.astype(F32)).astype(CD)


def _rounded(x):
    hi, lo = _split(x)
    return hi.astype(F32) + lo.astype(F32)


def _tri(ck, rel, stack=1):
    j, s = np.arange(ck)[:, None], np.arange(ck)[None, :]
    return jnp.asarray(np.tile({"gt": j > s, "le": j <= s, "lt": j < s}[rel], (stack, 1)), CD)


def _cumsum_mm(hi, lo, tri2):
    return _dot(jnp.concatenate([hi, lo], axis=1), tri2)


def _sb_fwd(proj, *, H, qb0, kb0, vb0, bq, ck, scale, name):
    rows = proj.shape[0]
    nsub = bq // ck
    HP, HPW = HP_FWD, HP_FWD * LANES

    def body(q_ref, k_ref, v_ref, u_ref, o_ref, t_ref):
        i = pl.program_id(1)
        u = u_ref[...]
        strict = lax.broadcasted_iota(jnp.int32, (bq, bq), 1) < lax.broadcasted_iota(jnp.int32, (bq, bq), 0)

        def block(j, carry, diag):
            off = pl.multiple_of(j * bq, bq)
            out = []
            for hh in range(HP):
                c, acc = carry[hh]
                z = _dot_nt(q_ref[:, _head(hh)], k_ref[pl.ds(off, bq), _head(hh)]) * scale
                lb, lom = _log_sigmoids(z)
                if diag:
                    lom = jnp.where(strict, lom, 0.0)
                hi, lo = _split(lom)
                parts = [None] * nsub
                for s in reversed(range(nsub)):
                    ss = slice(s * ck, (s + 1) * ck)
                    rin = _cumsum_mm(hi[:, ss], lo[:, ss], u)
                    parts[s] = jnp.exp(lb[:, ss] + rin + c)
                    c = c + rin[:, :1] + _rounded(lom[:, s * ck:s * ck + 1])
                a = parts[0] if nsub == 1 else jnp.concatenate(parts, axis=1)
                if diag:
                    a = jnp.where(strict, a, 0.0)
                out.append((c, acc + _dot(a.astype(CD), v_ref[pl.ds(off, bq), _head(hh)])))
            return tuple(out)

        init = tuple((jnp.zeros((bq, 1), F32), jnp.zeros((bq, LANES), F32)) for _ in range(HP))
        carry = block(i, init, True)
        carry = lax.fori_loop(0, i, lambda jj, cr: block(i - 1 - jj, cr, False), carry)
        for hh in range(HP):
            o_ref[:, _head(hh)] = carry[hh][1].astype(o_ref.dtype)
            t_ref[hh] = carry[hh][0]

    kb = lambda off: pl.BlockSpec((rows, HPW), lambda h, i: (0, off // HP + h))
    return pl.pallas_call(
        body, grid=(H // HP, rows // bq),
        in_specs=[pl.BlockSpec((bq, HPW), lambda h, i: (i, qb0 // HP + h)), kb(kb0), kb(vb0), pl.BlockSpec((2 * ck, ck), lambda h, i: (0, 0))],
        out_specs=[pl.BlockSpec((bq, HPW), lambda h, i: (i, h)), pl.BlockSpec((HP, bq, 1), lambda h, i: (h, i, 0))],
        out_shape=[jax.ShapeDtypeStruct((rows, H * LANES), CD), jax.ShapeDtypeStruct((H, rows, 1), F32)],
        compiler_params=_cp("arbitrary", "arbitrary"), name=name,
    )(proj, proj, proj, _tri(ck, "gt", 2))


def _sb_bwd(proj, do, tot, *, H, qb0, kb0, vb0, bq, ck, scale, name):
    rows = proj.shape[0]
    nsub = bq // ck

    def body(q_ref, k_ref, v_ref, do_ref, t_ref, ule_ref, ult_ref, dq_ref, dk_ref, dv_ref):
        i = pl.program_id(1)

        @pl.when(i == 0)
        def _():
            dk_ref[...] = jnp.zeros_like(dk_ref)
            dv_ref[...] = jnp.zeros_like(dv_ref)

        ule, ult = ule_ref[...], ult_ref[...]
        strict = lax.broadcasted_iota(jnp.int32, (bq, bq), 1) < lax.broadcasted_iota(jnp.int32, (bq, bq), 0)

        def block(j, carry, diag):
            off = pl.multiple_of(j * bq, bq)
            out = []
            for hh in range(HP):
                pc, gc, dq = carry[hh]
                q_, do_, tot_ = q_ref[:, _head(hh)], do_ref[:, _head(hh)], t_ref[hh]
                kj, vj = k_ref[pl.ds(off, bq), _head(hh)], v_ref[pl.ds(off, bq), _head(hh)]
                z = _dot_nt(q_, kj) * scale
                lb, lom = _log_sigmoids(z)
                if diag:
                    lom = jnp.where(strict, lom, 0.0)
                hi, lo = _split(lom)
                parts = []
                for s in range(nsub):
                    ss = slice(s * ck, (s + 1) * ck)
                    pin = _cumsum_mm(hi[:, ss], lo[:, ss], ule)
                    parts.append(jnp.exp(lb[:, ss] + (tot_ - (pc + pin))))
                    pc = pc + pin[:, ck - 1:ck]
                a = parts[0] if nsub == 1 else jnp.concatenate(parts, axis=1)
                if diag:
                    a = jnp.where(strict, a, 0.0)
                g = a * _dot_nt(do_, vj)
                gb = g.astype(CD)
                parts = []
                for s in range(nsub):
                    ss = slice(s * ck, (s + 1) * ck)
                    gin = _dot(gb[:, ss], ult)
                    parts.append(gc + gin)
                    gc = gc + gin[:, ck - 1:ck] + g[:, (s + 1) * ck - 1:(s + 1) * ck].astype(CD).astype(F32)
                big_g = parts[0] if nsub == 1 else jnp.concatenate(parts, axis=1)
                sg = jnp.exp(lb)
                dz = (g * (1.0 - sg) - big_g * sg) * scale
                if diag:
                    dz = jnp.where(strict, dz, 0.0)
                dz = dz.astype(CD)
                dk_ref[pl.ds(off, bq), _head(hh)] += _dot_tn(dz, q_)
                dv_ref[pl.ds(off, bq), _head(hh)] += _dot_tn(a.astype(CD), do_)
                out.append((pc, gc, dq + _dot(dz, kj)))
            return tuple(out)

        init = tuple((jnp.zeros((bq, 1), F32), jnp.zeros((bq, 1), F32), jnp.zeros((bq, LANES), F32)) for _ in range(HP))
        carry = lax.fori_loop(0, i, lambda j, cr: block(j, cr, False), init)
        carry = block(i, carry, True)
        for hh in range(HP):
            dq_ref[:, _head(hh)] = carry[hh][2].astype(dq_ref.dtype)

    kb = lambda off: pl.BlockSpec((rows, HPW), lambda h, i: (0, off // HP + h))
    ob = pl.BlockSpec((bq, HPW), lambda h, i: (i, h))
    tri = pl.BlockSpec((ck, ck), lambda h, i: (0, 0))
    acc = jax.ShapeDtypeStruct((rows, H * LANES), F32)
    return pl.pallas_call(
        body, grid=(H // HP, rows // bq),
        in_specs=[pl.BlockSpec((bq, HPW), lambda h, i: (i, qb0 // HP + h)), kb(kb0), kb(vb0), ob,
                  pl.BlockSpec((HP, bq, 1), lambda h, i: (h, i, 0)), pl.BlockSpec((2 * ck, ck), lambda h, i: (0, 0)), tri],
        out_specs=[ob, kb(0), kb(0)],
        out_shape=[jax.ShapeDtypeStruct((rows, H * LANES), CD), acc, acc],
        compiler_params=_cp("arbitrary", "arbitrary"), name=name,
    )(proj, proj, proj, do, tot, _tri(ck, "le", 2), _tri(ck, "lt"))


def _dev_index(p):
    return 4 * p[0] + 2 * p[1] + p[2]


def _all_gather(arrays, *, name, after=None):
    A = len(arrays)
    extra = [] if after is None else [after]

    def body(*refs):
        ins, outs = refs[:A], refs[A + len(extra):2 * A + len(extra)]
        send_sems, recv_sems, local_sems = refs[2 * A + len(extra):]
        x, y, c = lax.axis_index("x"), lax.axis_index("y"), lax.axis_index("c")
        me, sibling = (x, y, c), (x, y, 1 - c)
        chips = [(1 - x, y), (x, 1 - y), (1 - x, 1 - y)]

        def copy(a, k, block, to, src=None):
            dst = outs[a].at[_dev_index(block)]
            return pltpu.make_async_remote_copy(
                src_ref=dst if src is None else src, dst_ref=dst, send_sem=send_sems.at[a * 7 + k],
                recv_sem=recv_sems.at[a * 7 + k], device_id=to, device_id_type=MESH)

        mine = [pltpu.make_async_copy(ins[a], outs[a].at[_dev_index(me)], local_sems.at[a]) for a in range(A)]
        for cp in mine:
            cp.start()
        first = []
        for a in range(A):
            first.append(copy(a, 0, me, sibling, src=ins[a]))
            first += [copy(a, 1 + j, me, (*chip, c), src=ins[a]) for j, chip in enumerate(chips)]
        for cp in first:
            cp.start()
        passed = []
        for j, chip in enumerate(chips):
            for a in range(A):
                copy(a, 1 + j, (*chip, c), me).wait_recv()
                fwd = copy(a, 4 + j, (*chip, c), sibling)
                fwd.start()
                passed.append(fwd)
        for a in range(A):
            copy(a, 0, sibling, me).wait_recv()
            for j, chip in enumerate(chips):
                copy(a, 4 + j, (*chip, 1 - c), me).wait_recv()
        for cp in first + passed:
            cp.wait_send()
        for cp in mine:
            cp.wait()

    any_spec = pl.BlockSpec(memory_space=pl.ANY)
    return pl.pallas_call(
        body, in_specs=[any_spec] * (A + len(extra)), out_specs=[any_spec] * A,
        out_shape=[jax.ShapeDtypeStruct((N_DEV,) + a.shape, a.dtype) for a in arrays],
        scratch_shapes=[pltpu.SemaphoreType.DMA((7 * A,)), pltpu.SemaphoreType.DMA((7 * A,)), pltpu.SemaphoreType.DMA((A,))],
        name=name,
    )(*arrays, *extra)


_HBM = pl.BlockSpec(memory_space=pltpu.HBM)
_SEM = pl.BlockSpec(memory_space=pltpu.SEMAPHORE)
_EFFECT = pltpu.SideEffectType.DATAFLOW_SIDE_EFFECTING
_PEER_ORDER = (0, 1, 3, 2, 4, 5, 6)


def _peers():
    x, y, c = lax.axis_index("x"), lax.axis_index("y"), lax.axis_index("c")
    return _dev_index((x, y, c)), [((1 - x) if r & 4 else x, (1 - y) if r & 2 else y, (1 - c) if r & 1 else c) for r in range(1, N_DEV)]


def _xchg_copy(src, land, sends, recvs, a, k, me_i, peers, scatter, arriving):
    peer_i = _dev_index(peers[k])
    return pltpu.make_async_remote_copy(
        src_ref=src.at[peer_i] if scatter else src, dst_ref=land.at[peer_i if arriving else me_i],
        send_sem=sends.at[a * 7 + k], recv_sem=recvs.at[a * 7 + k], device_id=peers[k], device_id_type=MESH)


def _xchg_start(arrays, *, scatter, name, after=None):
    A = len(arrays)
    lands = [lax.empty(a.shape if scatter else (N_DEV,) + a.shape, a.dtype) for a in arrays]
    extra = [] if after is None else [after]

    def body(*refs):
        srcs, land = refs[:A], refs[A:2 * A]
        sends, recvs, token = refs[2 * A + len(extra)], refs[2 * A + len(extra) + 1], refs[-1]
        me_i, peers = _peers()
        for k in _PEER_ORDER:
            for a in range(A):
                _xchg_copy(srcs[a], land[a], sends, recvs, a, k, me_i, peers, scatter, False).start()
        token[...] = jnp.zeros_like(token)

    hbm = lambda a: pltpu.HBM(a.shape, a.dtype)
    outs = pl.pallas_call(
        body, name=name, in_specs=[_HBM] * (2 * A) + [pl.BlockSpec(memory_space=pl.ANY)] * len(extra),
        out_shape=(pltpu.SemaphoreType.DMA((7 * A,)), pltpu.SemaphoreType.DMA((7 * A,)), *[hbm(a) for a in arrays],
                   *[hbm(a) for a in lands], jax.ShapeDtypeStruct((8, LANES), F32)),
        out_specs=(_SEM, _SEM, *([_HBM] * (2 * A)), pl.BlockSpec(memory_space=pltpu.VMEM)),
        input_output_aliases={n: 2 + n for n in range(2 * A)},
        compiler_params=pltpu.CompilerParams(has_side_effects=_EFFECT),
    )(*[pltpu.with_memory_space_constraint(a, pltpu.HBM) for a in arrays],
      *[pltpu.with_memory_space_constraint(a, pltpu.HBM) for a in lands], *extra)
    return outs[0], outs[1], list(outs[2:2 + A]), list(outs[2 + A:2 + 2 * A]), outs[-1]


def _xchg_wait(handle, after, *, scatter, name):
    sends, recvs, srcs, lands, _ = handle
    A = len(srcs)

    def body(*refs):
        src, land = refs[:A], refs[A:2 * A]
        send_sems, recv_sems = refs[2 * A], refs[2 * A + 1]
        me_i, peers = _peers()
        for k in _PEER_ORDER:
            for a in range(A):
                cp = _xchg_copy(src[a], land[a], send_sems, recv_sems, a, k, me_i, peers, scatter, True)
                cp.wait_send()
                cp.wait_recv()

    outs = pl.pallas_call(
        body, name=name, in_specs=[_HBM] * (2 * A) + [_SEM, _SEM, pl.BlockSpec(memory_space=pl.ANY)],
        out_shape=tuple(pltpu.HBM(a.shape, a.dtype) for a in srcs + lands), out_specs=tuple([_HBM] * (2 * A)),
        input_output_aliases={n: n for n in range(2 * A)},
        compiler_params=pltpu.CompilerParams(has_side_effects=_EFFECT),
    )(*srcs, *lands, sends, recvs, after)
    my = _dev_index((lax.axis_index("x"), lax.axis_index("y"), lax.axis_index("c")))
    got = []
    for src, land in zip(outs[:A], outs[A:]):
        own = lax.dynamic_index_in_dim(src, my, 0, keepdims=True) if scatter else src[None]
        got.append(lax.dynamic_update_index_in_dim(land, own, my, 0))
    return got


def _adamw(g, w, m, v):
    m = ADAM_B1 * m + (1.0 - ADAM_B1) * g
    v = ADAM_B2 * v + (1.0 - ADAM_B2) * (g * g)
    delta = -ADAM_LR * ((m / ADAM_BC1) / (jnp.sqrt(v / ADAM_BC2) + ADAM_EPS) + ADAM_WD * w)
    return delta, m, v


def _adam_pieces(pieces, w, m, v, bufs, *, layer, name):
    depth, R, C = w.shape
    tr = _pick(R, [t for t in (512, 256, 128, 64, 32, 16) if t * C * 4 <= 2**20])

    def body(p_ref, w_ref, m_ref, v_ref, b0, b1, b2, b3, g_ref, d_ref, mo_ref, vo_ref):
        g = p_ref[0].astype(F32)
        for k in range(1, N_DEV):
            g = g + p_ref[k].astype(F32)
        g_ref[0] = g
        d_ref[0], mo_ref[0], vo_ref[0] = _adamw(g, w_ref[0], m_ref[0], v_ref[0])

    blk = pl.BlockSpec((1, tr, C), lambda i: (layer, i, 0))
    anyspec = pl.BlockSpec(memory_space=pl.ANY)
    return pl.pallas_call(
        body, grid=(R // tr,),
        in_specs=[pl.BlockSpec((N_DEV, tr, C), lambda i: (0, i, 0)), blk, blk, blk] + [anyspec] * 4,
        out_specs=[blk] * 4, out_shape=[jax.ShapeDtypeStruct((depth, R, C), F32)] * 4,
        input_output_aliases={4: 0, 5: 1, 6: 2, 7: 3}, compiler_params=_cp("parallel"), name=name,
    )(pieces, w, m, v, *bufs)


def _adam_direct(g, w, m, v, *, name):
    sh = jax.ShapeDtypeStruct(w.shape, F32)

    def body(g_ref, w_ref, m_ref, v_ref, d_ref, mo_ref, vo_ref):
        d_ref[...], mo_ref[...], vo_ref[...] = _adamw(g_ref[...], w_ref[...], m_ref[...], v_ref[...])

    return pl.pallas_call(body, out_shape=[sh] * 3, compiler_params=_cp(), name=name)(g, w, m, v)


def _sum_devices(parts, *, name):
    def body(p_ref, o_ref):
        s = p_ref[0]
        for k in range(1, N_DEV):
            s = s + p_ref[k]
        o_ref[...] = s

    return pl.pallas_call(body, out_shape=jax.ShapeDtypeStruct(parts.shape[1:], F32), compiler_params=_cp(), name=name)(parts)


def _rope_cols_pad(w):
    z = jnp.zeros(w.shape[:-1] + (ROPE // 2,), w.dtype)
    return jnp.concatenate([w[..., :ROPE // 2], z, w[..., ROPE // 2:], z], axis=-1)


def _rope_cols_unpad(w):
    return jnp.concatenate([w[..., :ROPE // 2], w[..., ROPE:ROPE + ROPE // 2]], axis=-1)


class _Dims:
    def __init__(self, D, S, NM, QL, KVL, WM, WS):
        self.D, self.S, self.NM, self.QL, self.KVL, self.WM, self.WS = D, S, NM, QL, KVL, WM, WS
        self.H = WM // LANES
        self.HW = self.H * LANES
        self.L = NM + S
        self.bq = 384 if self.L > 1024 else 128
        self.Lp = -(-self.L // self.bq) * self.bq
        self.ck_sb = LANES
        self.tr = 128
        self.sizes = (QL, KVL, ROPE, WM, WS, WS, WS, WS)
        self.d_in = sum(self.sizes)
        front = QL + KVL + LANES
        self.front = -(-front // WM) * WM
        self.pad = self.front - front
        self.dn = self.front + WM + 4 * WS
        assert WM == WS and QL % KVL == 0 and KVL % LANES == 0 and self.d_in % N_DEV == 0 and D % N_DEV == 0
        self.mla_scale = 1.0 / math.sqrt(LANES + ROPE)
        self.sb_scale = 1.0 / math.sqrt(LANES)


def _w_in_nice(d, w):
    parts = jnp.split(w, np.cumsum(d.sizes)[:-1].tolist(), axis=1)
    return jnp.concatenate([parts[0], parts[1], _rope_cols_pad(parts[2]), jnp.zeros((w.shape[0], d.pad), w.dtype)] + parts[3:], axis=1)


def _w_in_orig(d, w):
    a, b = d.QL + d.KVL, d.front
    return jnp.concatenate([w[:, :a], _rope_cols_unpad(w[:, a:a + LANES]), w[:, b:]], axis=1)


def _w_uq_nice(d, w):
    w3 = w.reshape(d.QL, d.H, LANES + ROPE)
    return jnp.concatenate([w3[..., :LANES], _rope_cols_pad(w3[..., LANES:])], axis=-1).reshape(d.QL, d.H * QKW)


def _w_uq_orig(d, w):
    w3 = w.reshape(d.QL, d.H, QKW)
    return jnp.concatenate([w3[..., :LANES], _rope_cols_unpad(w3[..., LANES:])], axis=-1).reshape(d.QL, d.H * (LANES + ROPE))


def _w_ukv_nice(d, w):
    return w.reshape(d.KVL, d.H, 2, LANES).transpose(0, 2, 1, 3).reshape(d.KVL, 2 * d.HW)


def _w_ukv_orig(d, w):
    return w.reshape(d.KVL, 2, d.H, LANES).transpose(0, 2, 1, 3).reshape(d.KVL, 2 * d.HW)


def _cols_from_shards(g):
    return g.transpose(1, 0, 2).reshape(g.shape[1], -1)


def _cols_to_shards(w):
    return w.reshape(w.shape[0], N_DEV, -1).transpose(1, 0, 2)


def _rope_tables(rows):
    inv_freq = ROPE_THETA ** (-jnp.arange(0, ROPE, 2, dtype=F32) / ROPE)
    ang = jnp.arange(rows, dtype=jnp.int32).astype(F32)[:, None] * inv_freq[None, :]
    cos, sin, z = jnp.cos(ang), jnp.sin(ang), jnp.zeros_like(ang)
    return jnp.concatenate([cos, z, cos, z], axis=1), jnp.concatenate([-sin, z, sin, z], axis=1)


def _flat_pack(vecs):
    flat = jnp.concatenate([v.reshape(-1).astype(F32) for v in vecs])
    n = flat.shape[0]
    rows = -(-n // (8 * 1024)) * 8
    offs = np.cumsum([0] + [int(np.prod(v.shape)) for v in vecs])
    return jnp.pad(flat, (0, rows * 1024 - n)).reshape(rows, 1024), offs


def kernel(x, meta_tokens, g_norm, w_in, g_q, g_kv, w_uq, w_ukv, g_out_mla, g_out_sb, w_o, g_final, loss_target, m_meta_tokens, m_g_norm, m_w_in, m_g_q, m_g_kv, m_w_uq, m_w_ukv, m_g_out_mla, m_g_out_sb, m_w_o, m_g_final, v_meta_tokens, v_g_norm, v_w_in, v_g_q, v_g_kv, v_w_uq, v_w_ukv, v_g_out_mla, v_g_out_sb, v_w_o, v_g_final):
    depth = g_norm.shape[0]
    d = _Dims(D=x.shape[-1], S=x.shape[1], NM=meta_tokens.shape[0], QL=g_q.shape[1], KVL=g_kv.shape[1],
              WM=g_out_mla.shape[1], WS=g_out_sb.shape[1])
    D, Lp, H, tr = d.D, d.Lp, d.H, d.tr
    my = _dev_index((lax.axis_index("x"), lax.axis_index("y"), lax.axis_index("c")))

    shards = lambda l: [w_in[l].astype(CD), w_uq[l].astype(CD), w_ukv[l].astype(CD), w_o[l].astype(CD)]
    g_in0, g_meta = _all_gather([shards(0)[0], meta_tokens], name="gather_layer0")
    gathers, token = {}, None
    for l in range(depth):
        gathers[l] = _xchg_start(shards(l)[1:] if l == 0 else shards(l), scatter=False, name=f"gather_start_{l}", after=token)
        token = gathers[l][4]
    started = token[0, 0]

    def relayout(g_uq, g_ukv, g_o):
        return _w_uq_nice(d, _cols_from_shards(g_uq)), _w_ukv_nice(d, _cols_from_shards(g_ukv)), g_o.reshape(D, D)

    wn, wuq, wukv, wo = [None] * depth, [None] * depth, [None] * depth, [None] * depth
    meta_full = _cols_from_shards(g_meta)

    cs, sn = _rope_tables(Lp)
    h = jnp.concatenate([meta_full, x[0], jnp.zeros((Lp - d.L, D), F32)], axis=0)
    tgt = jnp.concatenate([jnp.zeros((d.NM, D), F32), loss_target[0], jnp.zeros((Lp - d.L, D), F32)], axis=0)

    b128 = lambda col: col // LANES
    krb = b128(d.QL + d.KVL)
    zb_mla, zb_sb = d.front // d.WM, (d.front + d.WM + 3 * d.WS) // d.WS
    qb0, kb0, vb0 = b128(d.front + d.WM), b128(d.front + d.WM + d.WS), b128(d.front + d.WM + 2 * d.WS)
    row = lambda a, i: a[i][None, :]

    saved = []
    for i in range(depth):
        if i == 0:
            wn[i] = _w_in_nice(d, _cols_from_shards(g_in0))
            u = _rms_fwd(h, row(g_norm, i) + started, cb=0, width=D, tr=tr, name=f"rms_in_{i}")
            proj = _mm(u, wn[i], out_dtype=CD, name=f"proj_{i}")
            wuq[i], wukv[i], wo[i] = relayout(*_xchg_wait(gathers[i], proj, scatter=False, name=f"gather_wait_{i}"))
        else:
            g_in, *g_rest = _xchg_wait(gathers[i], h, scatter=False, name=f"gather_wait_{i}")
            wn[i] = _w_in_nice(d, _cols_from_shards(g_in))
            wuq[i], wukv[i], wo[i] = relayout(*g_rest)
            u = _rms_fwd(h, row(g_norm, i), cb=0, width=D, tr=tr, name=f"rms_in_{i}")
            proj = _mm(u, wn[i], out_dtype=CD, name=f"proj_{i}")
        cqn = _rms_fwd(proj, row(g_q, i), cb=0, width=d.QL, tr=tr, name=f"rms_q_{i}")
        ckvn = _rms_fwd(proj, row(g_kv, i), cb=d.QL // d.KVL, width=d.KVL, tr=tr, name=f"rms_kv_{i}")
        q_raw = _mm(cqn, wuq[i], out_dtype=F32, name=f"up_q_{i}")
        kvn = _mm(ckvn, wukv[i], out_dtype=CD, name=f"up_kv_{i}")
        q, kcat = _rope_fwd(q_raw, kvn, proj, cs, sn, H=H, krb=krb, tr=tr, name=f"rope_{i}")
        o_mla, lse = _mla_fwd(q, kcat, kvn, H=H, bq=d.bq, scale=d.mla_scale, name=f"mla_fwd_{i}")
        o_sb, tot = _sb_fwd(proj, H=H, qb0=qb0, kb0=kb0, vb0=vb0, bq=d.bq, ck=d.ck_sb, scale=d.sb_scale, name=f"sb_fwd_{i}")
        y = _gate_fwd(o_mla, o_sb, proj, row(g_out_mla, i), row(g_out_sb, i), zb_mla=zb_mla, zb_sb=zb_sb, tr=tr, name=f"gate_fwd_{i}")
        h_next = _mm(y, wo[i], out_dtype=F32, res=h, name=f"out_proj_{i}")
        saved.append((h, u, proj, cqn, ckvn, q, kvn, kcat, o_mla, lse, o_sb, tot, y))
        h = h_next

    dh, dg_final, loss_part = _loss_head(h, g_final[None, :], tgt, lo=d.NM, hi=d.L, tr=tr, name="loss_head")

    dg_norm, dg_q, dg_kv, dg_om, dg_os = [None] * depth, [None] * depth, [None] * depth, [None] * depth, [None] * depth
    big_w = {"w_in": (w_in, m_w_in, v_w_in), "w_uq": (w_uq, m_w_uq, v_w_uq), "w_ukv": (w_ukv, m_w_ukv, v_w_ukv), "w_o": (w_o, m_w_o, v_w_o)}
    res = {n: [lax.empty(w.shape, F32) for _ in range(4)] for n, (w, _, _) in big_w.items()}
    exchanges = [[] for _ in range(depth)]

    def send(l, tag, pieces, after=None):
        handle = _xchg_start(list(pieces.values()), scatter=True, name=f"exch_start_{l}{tag}", after=after)
        exchanges[l].append((handle, list(pieces), tag))
        return handle[4]

    def finish(l, after):
        got = None
        for handle, sent, tag in exchanges[l]:
            got = _xchg_wait(handle, after, scatter=True, name=f"exch_wait_{l}{tag}")
            for n, p in zip(sent, got):
                w, m, v = big_w[n]
                res[n] = _adam_pieces(p, w, m, v, res[n], layer=l, name=f"adam_{n}_{l}")
        return got[0]

    cat = lambda parts: jnp.concatenate(parts, axis=0)
    zpad = jnp.zeros((Lp, d.pad), CD)
    for i in reversed(range(depth)):
        h_in, u, proj, cqn, ckvn, q, kvn, kcat, o_mla, lse, o_sb, tot, y = saved[i]
        dy = _mm(dh, wo[i], tb=True, out_dtype=CD, name=f"d_y_{i}")
        dwo = _mm(y, dh, ta=True, out_dtype=CD, name=f"d_wo_{i}")
        do_mla, do_sb, dz_mla, dz_sb, dg_om[i], dg_os[i] = _gate_bwd(
            dy, o_mla, o_sb, proj, row(g_out_mla, i), row(g_out_sb, i), zb_mla=zb_mla, zb_sb=zb_sb, tr=tr, name=f"gate_bwd_{i}")
        dq_mla, dkcat, dv = _mla_bwd(q, kcat, kvn, o_mla, do_mla, lse, H=H, bq=d.bq, scale=d.mla_scale, name=f"mla_bwd_{i}")
        dq_sb, dk_sb, dv_sb = _sb_bwd(proj, do_sb, tot, H=H, qb0=qb0, kb0=kb0, vb0=vb0, bq=d.bq, ck=d.ck_sb, scale=d.sb_scale, name=f"sb_bwd_{i}")
        dq, dkv, dk_r = _rope_bwd(dq_mla, dkcat, dv, cs, sn, H=H, tr=tr, name=f"rope_bwd_{i}")
        dcqn = _mm(dq, wuq[i], tb=True, out_dtype=CD, name=f"d_cq_{i}")
        dwuq = _mm(cqn, dq, ta=True, out_dtype=CD, name=f"d_wuq_{i}")
        dckvn = _mm(dkv, wukv[i], tb=True, out_dtype=CD, name=f"d_ckv_{i}")
        dwukv = _mm(ckvn, dkv, ta=True, out_dtype=CD, name=f"d_wukv_{i}")
        dc_q, dg_q[i] = _rms_bwd(proj, row(g_q, i), dcqn, cb=0, width=d.QL, tr=tr, out_dtype=CD, name=f"rms_q_bwd_{i}")
        dc_kv, dg_kv[i] = _rms_bwd(proj, row(g_kv, i), dckvn, cb=d.QL // d.KVL, width=d.KVL, tr=tr, out_dtype=CD, name=f"rms_kv_bwd_{i}")
        rest = {"w_uq": _cols_to_shards(_w_uq_orig(d, dwuq)), "w_ukv": _cols_to_shards(_w_ukv_orig(d, dwukv)),
                "w_o": dwo.reshape(N_DEV, D // N_DEV, D)}
        placed = None
        if i == 0:
            small_a, offs_a = _flat_pack(([cat(dg_norm[1:])] if depth > 1 else []) + [cat(dg_q), cat(dg_kv), cat(dg_om), cat(dg_os), dg_final, loss_part])
            (small_a_all,) = _all_gather([small_a], name="gather_small_a")
            placed = send(0, "a", rest, after=small_a_all)
        dproj = jnp.concatenate([dc_q, dc_kv, dk_r, zpad, dz_mla, dq_sb, dk_sb.astype(CD), dv_sb.astype(CD), dz_sb], axis=1)
        dwn = _mm(u, dproj, ta=True, out_dtype=CD, after=placed, name=f"d_wn_{i}")
        pieces = {"w_in": _cols_to_shards(_w_in_orig(d, dwn))}
        if i > 0:
            pieces.update(rest)
        du = _mm(dproj, wn[i], tb=True, out_dtype=CD, after=send(i, "b", pieces), name=f"d_u_{i}")
        dh, dg_norm[i] = _rms_bwd(h_in, row(g_norm, i), du, cb=0, width=D, tr=tr, res=dh, out_dtype=F32, name=f"rms_in_bwd_{i}")
        if i + 1 < depth:
            finish(i + 1, dh)
    landed = finish(0, dh)

    small_b, offs_b = _flat_pack([dg_norm[0], dh[:d.NM]])
    (small_b_all,) = _all_gather([small_b], name="gather_small_b", after=landed)
    sum_a = _sum_devices(small_a_all, name="sum_small_a").reshape(-1)
    sum_b = _sum_devices(small_b_all, name="sum_small_b").reshape(-1)
    n_a = int(offs_a[-2])
    small_sum = jnp.concatenate([sum_b[:D], sum_a[:n_a], sum_b[D:offs_b[2]], sum_a[n_a:n_a + 1]])
    offs = np.cumsum([0, depth * D, depth * d.QL, depth * d.KVL, depth * d.WM, depth * d.WS, D, d.NM * D, 1])
    seg = lambda k, shape: small_sum[offs[k]:offs[k + 1]].reshape(shape)
    loss = small_sum[offs[7]]
    grad_x = dh[d.NM:d.L][None]

    g_meta_full = seg(6, (d.NM, D))
    ncol = meta_tokens.shape[1]
    g_meta_mine = lax.dynamic_slice(g_meta_full, (0, my * ncol), (d.NM, ncol))
    res["meta_tokens"] = [g_meta_mine] + list(_adam_direct(g_meta_mine, meta_tokens, m_meta_tokens, v_meta_tokens, name="adam_meta"))

    small_names = ["g_norm", "g_q", "g_kv", "g_out_mla", "g_out_sb", "g_final"]
    small_w = [g_norm, g_q, g_kv, g_out_mla, g_out_sb, g_final]
    small_m = [m_g_norm, m_g_q, m_g_kv, m_g_out_mla, m_g_out_sb, m_g_final]
    small_v = [v_g_norm, v_g_q, v_g_kv, v_g_out_mla, v_g_out_sb, v_g_final]
    g_flat, _ = _flat_pack([small_sum[:offs[6]]])
    wf, _ = _flat_pack(small_w)
    mf, _ = _flat_pack(small_m)
    vf, _ = _flat_pack(small_v)
    d_f, m_f, v_f = _adam_direct(g_flat, wf, mf, vf, name="adam_small")
    for k, nm in enumerate(small_names):
        sl = lambda f: f.reshape(-1)[offs[k]:offs[k + 1]].reshape(small_w[k].shape)
        res[nm] = [sl(g_flat), sl(d_f), sl(m_f), sl(v_f)]

    names = ["meta_tokens", "g_norm", "w_in", "g_q", "g_kv", "w_uq", "w_ukv", "g_out_mla", "g_out_sb", "w_o", "g_final"]
    return (loss, grad_x, *[res[n][0] for n in names], *[res[n][1] for n in names],
            *[res[n][2] for n in names], *[res[n][3] for n in names])
```

```python
import math

import numpy as np
import jax
import jax.numpy as jnp
from jax import lax
from jax.experimental import pallas as pl
from jax.experimental.pallas import tpu as pltpu

CD = jnp.bfloat16
F32 = jnp.float32

LANES = 128
N_DEV = 8
EPS = 1e-6
ROPE = 64
ROPE_THETA = 10000.0
NEG = -1e30
VMEM_LIMIT = 48 * 2**20
VMEM_LIMIT_WIDE = 56 * 2**20

ADAM_LR, ADAM_B1, ADAM_B2, ADAM_EPS, ADAM_WD, ADAM_STEP = 0.001, 0.9, 0.999, 1e-08, 0.01, 10
ADAM_BC1 = 1.0 - ADAM_B1**ADAM_STEP
ADAM_BC2 = 1.0 - ADAM_B2**ADAM_STEP

MESH = pl.DeviceIdType.MESH


def _cp(*sem):
    return pltpu.CompilerParams(dimension_semantics=tuple(sem) if sem else None, vmem_limit_bytes=VMEM_LIMIT)


def _pick(dim, prefs):
    for p in prefs:
        if dim % p == 0:
            return p
    raise ValueError(f"no tile for {dim} in {prefs}")


def _dot(a, b):
    return jnp.dot(a, b, preferred_element_type=F32)


def _dot_nt(a, b):
    return lax.dot_general(a, b, (((1,), (1,)), ((), ())), preferred_element_type=F32)


def _dot_tn(a, b):
    return lax.dot_general(a, b, (((0,), (0,)), ((), ())), preferred_element_type=F32)


def _mm(a, b, *, ta=False, tb=False, out_dtype, res=None, after=None, name):
    M, K = (a.shape[1], a.shape[0]) if ta else a.shape
    N = b.shape[0] if tb else b.shape[1]
    tm = _pick(M, (1056, 1024, 512, 256, 128))
    tn = _pick(N, (1024, 512, 384, 256, 128))
    tk = _pick(K, (2112, 1056, 1024, 512, 384, 256, 128) if ta else (2048, 1024, 512, 384, 256, 128))
    nk = K // tk
    a_dims = (((0,), (1 if tb else 0,)), ((), ())) if ta else (((1,), (1 if tb else 0,)), ((), ()))

    def body(*refs):
        if after is not None:
            refs = refs[:-3] + refs[-2:]
        if res is None:
            a_ref, b_ref, o_ref, acc = refs
        else:
            a_ref, b_ref, r_ref, o_ref, acc = refs
        k = pl.program_id(2)

        @pl.when(k == 0)
        def _():
            acc[...] = jnp.zeros_like(acc)

        acc[...] += lax.dot_general(a_ref[...].astype(CD), b_ref[...].astype(CD), a_dims, preferred_element_type=F32)

        @pl.when(k == nk - 1)
        def _():
            r = acc[...]
            if res is not None:
                r = r + r_ref[...]
            o_ref[...] = r.astype(out_dtype)

    a_spec = pl.BlockSpec((tk, tm), lambda i, j, k: (k, i)) if ta else pl.BlockSpec((tm, tk), lambda i, j, k: (i, k))
    b_spec = pl.BlockSpec((tn, tk), lambda i, j, k: (j, k)) if tb else pl.BlockSpec((tk, tn), lambda i, j, k: (k, j))
    o_spec = pl.BlockSpec((tm, tn), lambda i, j, k: (i, j))
    in_specs, args = [a_spec, b_spec], [a, b]
    if res is not None:
        in_specs.append(o_spec)
        args.append(res)
    if after is not None:
        in_specs.append(pl.BlockSpec(memory_space=pl.ANY))
        args.append(after)
    return pl.pallas_call(
        body, grid=(M // tm, N // tn, nk), in_specs=in_specs, out_specs=o_spec,
        out_shape=jax.ShapeDtypeStruct((M, N), out_dtype), scratch_shapes=[pltpu.VMEM((tm, tn), F32)],
        compiler_params=_cp("parallel", "parallel", "arbitrary"), name=name,
    )(*args)


def _rms_fwd(x, g, *, cb, width, tr, name):
    rows = x.shape[0]

    def body(x_ref, g_ref, o_ref):
        xv = x_ref[...].astype(F32)
        r = lax.rsqrt(jnp.mean(xv * xv, axis=-1, keepdims=True) + EPS)
        o_ref[...] = (xv * r * g_ref[...]).astype(o_ref.dtype)

    return pl.pallas_call(
        body, grid=(rows // tr,),
        in_specs=[pl.BlockSpec((tr, width), lambda i: (i, cb)), pl.BlockSpec((1, width), lambda i: (0, 0))],
        out_specs=pl.BlockSpec((tr, width), lambda i: (i, 0)),
        out_shape=jax.ShapeDtypeStruct((rows, width), CD), compiler_params=_cp("parallel"), name=name,
    )(x, g)


def _rms_bwd(x, g, dy, *, cb, width, tr, res=None, out_dtype, name):
    rows = x.shape[0]

    def body(*refs):
        if res is None:
            x_ref, g_ref, dy_ref, dx_ref, dg_ref = refs
        else:
            x_ref, g_ref, dy_ref, r_ref, dx_ref, dg_ref = refs

        @pl.when(pl.program_id(0) == 0)
        def _():
            dg_ref[...] = jnp.zeros_like(dg_ref)

        xv = x_ref[...].astype(F32)
        r = lax.rsqrt(jnp.mean(xv * xv, axis=-1, keepdims=True) + EPS)
        xh = xv * r
        dyv = dy_ref[...].astype(F32)
        dxh = dyv * g_ref[...]
        dx = r * (dxh - xh * jnp.mean(dxh * xh, axis=-1, keepdims=True))
        if res is not None:
            dx = dx + r_ref[...]
        dx_ref[...] = dx.astype(out_dtype)
        dg_ref[...] += jnp.sum(dyv * xh, axis=0, keepdims=True)

    blk = pl.BlockSpec((tr, width), lambda i: (i, 0))
    in_specs = [pl.BlockSpec((tr, width), lambda i: (i, cb)), pl.BlockSpec((1, width), lambda i: (0, 0)), blk]
    args = [x, g, dy]
    if res is not None:
        in_specs.append(blk)
        args.append(res)
    return pl.pallas_call(
        body, grid=(rows // tr,), in_specs=in_specs,
        out_specs=[blk, pl.BlockSpec((1, width), lambda i: (0, 0))],
        out_shape=[jax.ShapeDtypeStruct((rows, width), out_dtype), jax.ShapeDtypeStruct((1, width), F32)],
        compiler_params=_cp("arbitrary"), name=name,
    )(*args)


def _sigmoid(z):
    return 1.0 / (1.0 + jnp.exp(-z))


def _gate_fwd(o_mla, o_sb, proj, g_mla, g_sb, *, zb_mla, zb_sb, tr, name):
    rows, W = o_mla.shape

    def body(om_ref, os_ref, zm_ref, zs_ref, gm_ref, gs_ref, y_ref):
        for half, (o_ref, z_ref, g_ref) in enumerate(((om_ref, zm_ref, gm_ref), (os_ref, zs_ref, gs_ref))):
            o = o_ref[...].astype(F32)
            r = lax.rsqrt(jnp.mean(o * o, axis=-1, keepdims=True) + EPS)
            z = z_ref[...].astype(F32)
            y_ref[:, half * W:(half + 1) * W] = ((o * r * g_ref[...]) * (z * _sigmoid(z))).astype(y_ref.dtype)

    blk = pl.BlockSpec((tr, W), lambda i: (i, 0))
    gsp = pl.BlockSpec((1, W), lambda i: (0, 0))
    return pl.pallas_call(
        body, grid=(rows // tr,),
        in_specs=[blk, blk, pl.BlockSpec((tr, W), lambda i: (i, zb_mla)), pl.BlockSpec((tr, W), lambda i: (i, zb_sb)), gsp, gsp],
        out_specs=pl.BlockSpec((tr, 2 * W), lambda i: (i, 0)),
        out_shape=jax.ShapeDtypeStruct((rows, 2 * W), CD), compiler_params=_cp("parallel"), name=name,
    )(o_mla, o_sb, proj, proj, g_mla, g_sb)


def _gate_bwd(dy, o_mla, o_sb, proj, g_mla, g_sb, *, zb_mla, zb_sb, tr, name):
    rows, W = o_mla.shape

    def body(dy_ref, om_ref, os_ref, zm_ref, zs_ref, gm_ref, gs_ref, dom_ref, dos_ref, dzm_ref, dzs_ref, dgm_ref, dgs_ref):
        @pl.when(pl.program_id(0) == 0)
        def _():
            dgm_ref[...] = jnp.zeros_like(dgm_ref)
            dgs_ref[...] = jnp.zeros_like(dgs_ref)

        halves = ((om_ref, zm_ref, gm_ref, dom_ref, dzm_ref, dgm_ref), (os_ref, zs_ref, gs_ref, dos_ref, dzs_ref, dgs_ref))
        for half, (o_ref, z_ref, g_ref, do_ref, dz_ref, dg_ref) in enumerate(halves):
            dyv = dy_ref[:, half * W:(half + 1) * W].astype(F32)
            o = o_ref[...].astype(F32)
            r = lax.rsqrt(jnp.mean(o * o, axis=-1, keepdims=True) + EPS)
            oh = o * r
            g = g_ref[...]
            z = z_ref[...].astype(F32)
            sg = _sigmoid(z)
            dn = dyv * (z * sg)
            dz_ref[...] = (dyv * (oh * g) * (sg * (1.0 + z * (1.0 - sg)))).astype(dz_ref.dtype)
            dxh = dn * g
            do_ref[...] = (r * (dxh - oh * jnp.mean(dxh * oh, axis=-1, keepdims=True))).astype(do_ref.dtype)
            dg_ref[...] += jnp.sum(dn * oh, axis=0, keepdims=True)

    blk = pl.BlockSpec((tr, W), lambda i: (i, 0))
    gsp = pl.BlockSpec((1, W), lambda i: (0, 0))
    act = jax.ShapeDtypeStruct((rows, W), CD)
    gsh = jax.ShapeDtypeStruct((1, W), F32)
    return pl.pallas_call(
        body, grid=(rows // tr,),
        in_specs=[pl.BlockSpec((tr, 2 * W), lambda i: (i, 0)), blk, blk,
                  pl.BlockSpec((tr, W), lambda i: (i, zb_mla)), pl.BlockSpec((tr, W), lambda i: (i, zb_sb)), gsp, gsp],
        out_specs=[blk, blk, blk, blk, gsp, gsp], out_shape=[act, act, act, act, gsh, gsh],
        compiler_params=_cp("arbitrary"), name=name,
    )(dy, o_mla, o_sb, proj, proj, g_mla, g_sb)


QKW = 2 * LANES


def _rope_fwd(q_raw, kv, proj, cs, sn, *, H, krb, tr, name):
    rows = q_raw.shape[0]

    def body(q_ref, kn_ref, kr_ref, c_ref, s_ref, qo_ref, ko_ref):
        c, s = c_ref[...], s_ref[...]
        xk = kr_ref[...].astype(F32)
        kr = (xk * c + pltpu.roll(xk, ROPE, 1) * s).astype(ko_ref.dtype)
        for h in range(H):
            nope, rope = slice(h * QKW, h * QKW + LANES), slice(h * QKW + LANES, (h + 1) * QKW)
            qo_ref[:, nope] = q_ref[:, nope].astype(qo_ref.dtype)
            xh = q_ref[:, rope]
            qo_ref[:, rope] = (xh * c + pltpu.roll(xh, ROPE, 1) * s).astype(qo_ref.dtype)
            ko_ref[:, nope] = kn_ref[:, _head(h)]
            ko_ref[:, rope] = kr

    tab = pl.BlockSpec((tr, LANES), lambda i: (i, 0))
    wide = pl.BlockSpec((tr, H * QKW), lambda i: (i, 0))
    sh = jax.ShapeDtypeStruct((rows, H * QKW), CD)
    return pl.pallas_call(
        body, grid=(rows // tr,),
        in_specs=[wide, pl.BlockSpec((tr, H * LANES), lambda i: (i, 0)), pl.BlockSpec((tr, LANES), lambda i: (i, krb)), tab, tab],
        out_specs=[wide, wide], out_shape=[sh, sh], compiler_params=_cp("parallel"), name=name,
    )(q_raw, kv, proj, cs, sn)


def _rope_bwd(dq, dkcat, dv, cs, sn, *, H, tr, name):
    rows = dq.shape[0]
    HW = H * LANES

    def body(dq_ref, dk_ref, dv_ref, c_ref, s_ref, dqo_ref, dkv_ref, dkr_ref):
        c, s = c_ref[...], s_ref[...]
        dkr = jnp.zeros((tr, LANES), F32)
        for h in range(H):
            nope, rope = slice(h * QKW, h * QKW + LANES), slice(h * QKW + LANES, (h + 1) * QKW)
            dqo_ref[:, nope] = dq_ref[:, nope]
            d = dq_ref[:, rope].astype(F32)
            dqo_ref[:, rope] = (d * c + pltpu.roll(d * s, ROPE, 1)).astype(dqo_ref.dtype)
            dkv_ref[:, _head(h)] = dk_ref[:, nope].astype(dkv_ref.dtype)
            dkr = dkr + dk_ref[:, rope]
        dkv_ref[:, HW:] = dv_ref[...].astype(dkv_ref.dtype)
        dkr_ref[...] = (dkr * c + pltpu.roll(dkr * s, ROPE, 1)).astype(dkr_ref.dtype)

    tab = pl.BlockSpec((tr, LANES), lambda i: (i, 0))
    wide = pl.BlockSpec((tr, H * QKW), lambda i: (i, 0))
    return pl.pallas_call(
        body, grid=(rows // tr,), in_specs=[wide, wide, pl.BlockSpec((tr, HW), lambda i: (i, 0)), tab, tab],
        out_specs=[wide, wide, tab],
        out_shape=[jax.ShapeDtypeStruct((rows, H * QKW), CD), jax.ShapeDtypeStruct((rows, 2 * HW), CD), jax.ShapeDtypeStruct((rows, LANES), CD)],
        compiler_params=_cp("parallel"), name=name,
    )(dq, dkcat, dv, cs, sn)


def _loss_head(h, g, tgt, *, lo, hi, tr, name):
    rows, D = h.shape

    def body(h_ref, g_ref, t_ref, dh_ref, dg_ref, loss_ref):
        i = pl.program_id(0)

        @pl.when(i == 0)
        def _():
            dg_ref[...] = jnp.zeros_like(dg_ref)
            loss_ref[...] = jnp.zeros_like(loss_ref)

        xv = h_ref[...]
        r = lax.rsqrt(jnp.mean(xv * xv, axis=-1, keepdims=True) + EPS)
        xh = xv * r
        g_ = g_ref[...]
        rowid = i * tr + lax.broadcasted_iota(jnp.int32, (tr, 1), 0)
        valid = jnp.logical_and(rowid >= lo, rowid < hi)
        err = jnp.where(valid, xh * g_ - t_ref[...], 0.0)
        loss_ref[...] += jnp.sum(jnp.sum(err * err, axis=-1, keepdims=True), axis=0, keepdims=True) * (0.5 / D)
        dy = err * (1.0 / D)
        dxh = dy * g_
        dh_ref[...] = r * (dxh - xh * jnp.mean(dxh * xh, axis=-1, keepdims=True))
        dg_ref[...] += jnp.sum(dy * xh, axis=0, keepdims=True)

    blk = pl.BlockSpec((tr, D), lambda i: (i, 0))
    gsp = pl.BlockSpec((1, D), lambda i: (0, 0))
    return pl.pallas_call(
        body, grid=(rows // tr,), in_specs=[blk, gsp, blk],
        out_specs=[blk, gsp, pl.BlockSpec((1, 1), lambda i: (0, 0))],
        out_shape=[jax.ShapeDtypeStruct((rows, D), F32), jax.ShapeDtypeStruct((1, D), F32), jax.ShapeDtypeStruct((1, 1), F32)],
        compiler_params=_cp("arbitrary"), name=name,
    )(h, g, tgt)


HP = 2
HPW = HP * LANES
HP_FWD = 4


def _head(hh):
    return slice(hh * LANES, (hh + 1) * LANES)


def _qk(hh):
    return slice(hh * QKW, (hh + 1) * QKW)


def _mla_fwd(q, kcat, kv, *, H, bq, scale, name):
    rows = q.shape[0]
    HP, HPW = HP_FWD, HP_FWD * LANES

    def body(q_ref, k_ref, v_ref, o_ref, lse_ref):
        i = pl.program_id(1)
        causal = lax.broadcasted_iota(jnp.int32, (bq, bq), 1) <= lax.broadcasted_iota(jnp.int32, (bq, bq), 0)

        def block(j, carry, diag):
            off = pl.multiple_of(j * bq, bq)
            out = []
            for hh in range(HP):
                m, l, acc = carry[hh]
                s = _dot_nt(q_ref[:, _qk(hh)], k_ref[pl.ds(off, bq), _qk(hh)]) * scale
                if diag:
                    s = jnp.where(causal, s, NEG)
                m_new = jnp.maximum(m, jnp.max(s, axis=-1, keepdims=True))
                p = jnp.exp(s - m_new)
                alpha = jnp.exp(m - m_new)
                l = alpha * l + jnp.sum(p, axis=-1, keepdims=True)
                acc = alpha * acc + _dot(p.astype(CD), v_ref[pl.ds(off, bq), _head(hh)])
                out.append((m_new, l, acc))
            return tuple(out)

        init = tuple((jnp.full((bq, 1), NEG, F32), jnp.zeros((bq, 1), F32), jnp.zeros((bq, LANES), F32)) for _ in range(HP))
        carry = lax.fori_loop(0, i, lambda j, cr: block(j, cr, False), init)
        carry = block(i, carry, True)
        for hh in range(HP):
            m, l, acc = carry[hh]
            o_ref[:, _head(hh)] = (acc / l).astype(o_ref.dtype)
            lse_ref[hh] = m + jnp.log(l)

    return pl.pallas_call(
        body, grid=(H // HP, rows // bq),
        in_specs=[pl.BlockSpec((bq, HP * QKW), lambda h, i: (i, h)), pl.BlockSpec((rows, HP * QKW), lambda h, i: (0, h)),
                  pl.BlockSpec((rows, HPW), lambda h, i: (0, H // HP + h))],
        out_specs=[pl.BlockSpec((bq, HPW), lambda h, i: (i, h)), pl.BlockSpec((HP, bq, 1), lambda h, i: (h, i, 0))],
        out_shape=[jax.ShapeDtypeStruct((rows, H * LANES), CD), jax.ShapeDtypeStruct((H, rows, 1), F32)],
        compiler_params=_cp("arbitrary", "arbitrary"), name=name,
    )(q, kcat, kv)


def _mla_bwd(q, kcat, kv, o, do, lse, *, H, bq, scale, name):
    rows = q.shape[0]

    def body(q_ref, k_ref, v_ref, o_ref, do_ref, lse_ref, dq_ref, dk_ref, dv_ref):
        i = pl.program_id(1)

        @pl.when(i == 0)
        def _():
            dk_ref[...] = jnp.zeros_like(dk_ref)
            dv_ref[...] = jnp.zeros_like(dv_ref)

        causal = lax.broadcasted_iota(jnp.int32, (bq, bq), 1) <= lax.broadcasted_iota(jnp.int32, (bq, bq), 0)
        delta = [jnp.sum(do_ref[:, _head(hh)].astype(F32) * o_ref[:, _head(hh)].astype(F32), axis=-1, keepdims=True) for hh in range(HP)]

        def block(j, carry, diag):
            off = pl.multiple_of(j * bq, bq)
            out = []
            for hh in range(HP):
                q_, do_ = q_ref[:, _qk(hh)], do_ref[:, _head(hh)]
                kj, vj = k_ref[pl.ds(off, bq), _qk(hh)], v_ref[pl.ds(off, bq), _head(hh)]
                p = jnp.exp(_dot_nt(q_, kj) * scale - lse_ref[hh])
                if diag:
                    p = jnp.where(causal, p, 0.0)
                ds = (p * (_dot_nt(do_, vj) - delta[hh]) * scale).astype(CD)
                dk_ref[pl.ds(off, bq), _qk(hh)] += _dot_tn(ds, q_)
                dv_ref[pl.ds(off, bq), _head(hh)] += _dot_tn(p.astype(CD), do_)
                out.append(carry[hh] + _dot(ds, kj))
            return tuple(out)

        init = tuple(jnp.zeros((bq, QKW), F32) for _ in range(HP))
        carry = lax.fori_loop(0, i, lambda j, cr: block(j, cr, False), init)
        carry = block(i, carry, True)
        for hh in range(HP):
            dq_ref[:, _qk(hh)] = carry[hh].astype(dq_ref.dtype)

    qb = pl.BlockSpec((bq, HP * QKW), lambda h, i: (i, h))
    kb = pl.BlockSpec((rows, HP * QKW), lambda h, i: (0, h))
    ob = pl.BlockSpec((bq, HPW), lambda h, i: (i, h))
    return pl.pallas_call(
        body, grid=(H // HP, rows // bq),
        in_specs=[qb, kb, pl.BlockSpec((rows, HPW), lambda h, i: (0, H // HP + h)), ob, ob, pl.BlockSpec((HP, bq, 1), lambda h, i: (h, i, 0))],
        out_specs=[qb, kb, pl.BlockSpec((rows, HPW), lambda h, i: (0, h))],
        out_shape=[jax.ShapeDtypeStruct((rows, H * QKW), CD), jax.ShapeDtypeStruct((rows, H * QKW), F32), jax.ShapeDtypeStruct((rows, H * LANES), F32)],
        compiler_params=pltpu.CompilerParams(dimension_semantics=("arbitrary", "arbitrary"), vmem_limit_bytes=VMEM_LIMIT_WIDE), name=name,
    )(q, kcat, kv, o, do, lse)


def _log_sigmoids(z):
    lb = jnp.minimum(z, 0.0) - jnp.log(1.0 + jnp.exp(-jnp.abs(z)))
    return lb, lb - z


def _split(x):
    hi = x.astype(CD)
    return hi, (x - hi.astype(F32)).astype(CD)


def _rounded(x):
    hi, lo = _split(x)
    return hi.astype(F32) + lo.astype(F32)


def _tri(ck, rel, stack=1):
    j, s = np.arange(ck)[:, None], np.arange(ck)[None, :]
    return jnp.asarray(np.tile({"gt": j > s, "le": j <= s, "lt": j < s}[rel], (stack, 1)), CD)


def _cumsum_mm(hi, lo, tri2):
    return _dot(jnp.concatenate([hi, lo], axis=1), tri2)


def _sb_fwd(proj, *, H, qb0, kb0, vb0, bq, ck, scale, name):
    rows = proj.shape[0]
    nsub = bq // ck
    HP, HPW = HP_FWD, HP_FWD * LANES

    def body(q_ref, k_ref, v_ref, u_ref, o_ref, t_ref):
        i = pl.program_id(1)
        u = u_ref[...]
        strict = lax.broadcasted_iota(jnp.int32, (bq, bq), 1) < lax.broadcasted_iota(jnp.int32, (bq, bq), 0)

        def block(j, carry, diag):
            off = pl.multiple_of(j * bq, bq)
            out = []
            for hh in range(HP):
                c, acc = carry[hh]
                z = _dot_nt(q_ref[:, _head(hh)], k_ref[pl.ds(off, bq), _head(hh)]) * scale
                lb, lom = _log_sigmoids(z)
                if diag:
                    lom = jnp.where(strict, lom, 0.0)
                hi, lo = _split(lom)
                parts = [None] * nsub
                for s in reversed(range(nsub)):
                    ss = slice(s * ck, (s + 1) * ck)
                    rin = _cumsum_mm(hi[:, ss], lo[:, ss], u)
                    parts[s] = jnp.exp(lb[:, ss] + rin + c)
                    c = c + rin[:, :1] + _rounded(lom[:, s * ck:s * ck + 1])
                a = parts[0] if nsub == 1 else jnp.concatenate(parts, axis=1)
                if diag:
                    a = jnp.where(strict, a, 0.0)
                out.append((c, acc + _dot(a.astype(CD), v_ref[pl.ds(off, bq), _head(hh)])))
            return tuple(out)

        init = tuple((jnp.zeros((bq, 1), F32), jnp.zeros((bq, LANES), F32)) for _ in range(HP))
        carry = block(i, init, True)
        carry = lax.fori_loop(0, i, lambda jj, cr: block(i - 1 - jj, cr, False), carry)
        for hh in range(HP):
            o_ref[:, _head(hh)] = carry[hh][1].astype(o_ref.dtype)
            t_ref[hh] = carry[hh][0]

    kb = lambda off: pl.BlockSpec((rows, HPW), lambda h, i: (0, off // HP + h))
    return pl.pallas_call(
        body, grid=(H // HP, rows // bq),
        in_specs=[pl.BlockSpec((bq, HPW), lambda h, i: (i, qb0 // HP + h)), kb(kb0), kb(vb0), pl.BlockSpec((2 * ck, ck), lambda h, i: (0, 0))],
        out_specs=[pl.BlockSpec((bq, HPW), lambda h, i: (i, h)), pl.BlockSpec((HP, bq, 1), lambda h, i: (h, i, 0))],
        out_shape=[jax.ShapeDtypeStruct((rows, H * LANES), CD), jax.ShapeDtypeStruct((H, rows, 1), F32)],
        compiler_params=_cp("arbitrary", "arbitrary"), name=name,
    )(proj, proj, proj, _tri(ck, "gt", 2))


def _sb_bwd(proj, do, tot, *, H, qb0, kb0, vb0, bq, ck, scale, name):
    rows = proj.shape[0]
    nsub = bq // ck

    def body(q_ref, k_ref, v_ref, do_ref, t_ref, ule_ref, ult_ref, dq_ref, dk_ref, dv_ref):
        i = pl.program_id(1)

        @pl.when(i == 0)
        def _():
            dk_ref[...] = jnp.zeros_like(dk_ref)
            dv_ref[...] = jnp.zeros_like(dv_ref)

        ule, ult = ule_ref[...], ult_ref[...]
        strict = lax.broadcasted_iota(jnp.int32, (bq, bq), 1) < lax.broadcasted_iota(jnp.int32, (bq, bq), 0)

        def block(j, carry, diag):
            off = pl.multiple_of(j * bq, bq)
            out = []
            for hh in range(HP):
                pc, gc, dq = carry[hh]
                q_, do_, tot_ = q_ref[:, _head(hh)], do_ref[:, _head(hh)], t_ref[hh]
                kj, vj = k_ref[pl.ds(off, bq), _head(hh)], v_ref[pl.ds(off, bq), _head(hh)]
                z = _dot_nt(q_, kj) * scale
                lb, lom = _log_sigmoids(z)
                if diag:
                    lom = jnp.where(strict, lom, 0.0)
                hi, lo = _split(lom)
                parts = []
                for s in range(nsub):
                    ss = slice(s * ck, (s + 1) * ck)
                    pin = _cumsum_mm(hi[:, ss], lo[:, ss], ule)
                    parts.append(jnp.exp(lb[:, ss] + (tot_ - (pc + pin))))
                    pc = pc + pin[:, ck - 1:ck]
                a = parts[0] if nsub == 1 else jnp.concatenate(parts, axis=1)
                if diag:
                    a = jnp.where(strict, a, 0.0)
                g = a * _dot_nt(do_, vj)
                gb = g.astype(CD)
                parts = []
                for s in range(nsub):
                    ss = slice(s * ck, (s + 1) * ck)
                    gin = _dot(gb[:, ss], ult)
                    parts.append(gc + gin)
                    gc = gc + gin[:, ck - 1:ck] + g[:, (s + 1) * ck - 1:(s + 1) * ck].astype(CD).astype(F32)
                big_g = parts[0] if nsub == 1 else jnp.concatenate(parts, axis=1)
                sg = jnp.exp(lb)
                dz = (g * (1.0 - sg) - big_g * sg) * scale
                if diag:
                    dz = jnp.where(strict, dz, 0.0)
                dz = dz.astype(CD)
                dk_ref[pl.ds(off, bq), _head(hh)] += _dot_tn(dz, q_)
                dv_ref[pl.ds(off, bq), _head(hh)] += _dot_tn(a.astype(CD), do_)
                out.append((pc, gc, dq + _dot(dz, kj)))
            return tuple(out)

        init = tuple((jnp.zeros((bq, 1), F32), jnp.zeros((bq, 1), F32), jnp.zeros((bq, LANES), F32)) for _ in range(HP))
        carry = lax.fori_loop(0, i, lambda j, cr: block(j, cr, False), init)
        carry = block(i, carry, True)
        for hh in range(HP):
            dq_ref[:, _head(hh)] = carry[hh][2].astype(dq_ref.dtype)

    kb = lambda off: pl.BlockSpec((rows, HPW), lambda h, i: (0, off // HP + h))
    ob = pl.BlockSpec((bq, HPW), lambda h, i: (i, h))
    tri = pl.BlockSpec((ck, ck), lambda h, i: (0, 0))
    acc = jax.ShapeDtypeStruct((rows, H * LANES), F32)
    return pl.pallas_call(
        body, grid=(H // HP, rows // bq),
        in_specs=[pl.BlockSpec((bq, HPW), lambda h, i: (i, qb0 // HP + h)), kb(kb0), kb(vb0), ob,
                  pl.BlockSpec((HP, bq, 1), lambda h, i: (h, i, 0)), pl.BlockSpec((2 * ck, ck), lambda h, i: (0, 0)), tri],
        out_specs=[ob, kb(0), kb(0)],
        out_shape=[jax.ShapeDtypeStruct((rows, H * LANES), CD), acc, acc],
        compiler_params=_cp("arbitrary", "arbitrary"), name=name,
    )(proj, proj, proj, do, tot, _tri(ck, "le", 2), _tri(ck, "lt"))


def _dev_index(p):
    return 4 * p[0] + 2 * p[1] + p[2]


def _all_gather(arrays, *, name, after=None):
    A = len(arrays)
    extra = [] if after is None else [after]

    def body(*refs):
        ins, outs = refs[:A], refs[A + len(extra):2 * A + len(extra)]
        send_sems, recv_sems, local_sems = refs[2 * A + len(extra):]
        x, y, c = lax.axis_index("x"), lax.axis_index("y"), lax.axis_index("c")
        me, sibling = (x, y, c), (x, y, 1 - c)
        chips = [(1 - x, y), (x, 1 - y), (1 - x, 1 - y)]

        def copy(a, k, block, to, src=None):
            dst = outs[a].at[_dev_index(block)]
            return pltpu.make_async_remote_copy(
                src_ref=dst if src is None else src, dst_ref=dst, send_sem=send_sems.at[a * 7 + k],
                recv_sem=recv_sems.at[a * 7 + k], device_id=to, device_id_type=MESH)

        mine = [pltpu.make_async_copy(ins[a], outs[a].at[_dev_index(me)], local_sems.at[a]) for a in range(A)]
        for cp in mine:
            cp.start()
        first = []
        for a in range(A):
            first.append(copy(a, 0, me, sibling, src=ins[a]))
            first += [copy(a, 1 + j, me, (*chip, c), src=ins[a]) for j, chip in enumerate(chips)]
        for cp in first:
            cp.start()
        passed = []
        for j, chip in enumerate(chips):
            for a in range(A):
                copy(a, 1 + j, (*chip, c), me).wait_recv()
                fwd = copy(a, 4 + j, (*chip, c), sibling)
                fwd.start()
                passed.append(fwd)
        for a in range(A):
            copy(a, 0, sibling, me).wait_recv()
            for j, chip in enumerate(chips):
                copy(a, 4 + j, (*chip, 1 - c), me).wait_recv()
        for cp in first + passed:
            cp.wait_send()
        for cp in mine:
            cp.wait()

    any_spec = pl.BlockSpec(memory_space=pl.ANY)
    return pl.pallas_call(
        body, in_specs=[any_spec] * (A + len(extra)), out_specs=[any_spec] * A,
        out_shape=[jax.ShapeDtypeStruct((N_DEV,) + a.shape, a.dtype) for a in arrays],
        scratch_shapes=[pltpu.SemaphoreType.DMA((7 * A,)), pltpu.SemaphoreType.DMA((7 * A,)), pltpu.SemaphoreType.DMA((A,))],
        name=name,
    )(*arrays, *extra)


_HBM = pl.BlockSpec(memory_space=pltpu.HBM)
_SEM = pl.BlockSpec(memory_space=pltpu.SEMAPHORE)
_EFFECT = pltpu.SideEffectType.DATAFLOW_SIDE_EFFECTING
_PEER_ORDER = (0, 1, 3, 2, 4, 5, 6)


def _peers():
    x, y, c = lax.axis_index("x"), lax.axis_index("y"), lax.axis_index("c")
    return _dev_index((x, y, c)), [((1 - x) if r & 4 else x, (1 - y) if r & 2 else y, (1 - c) if r & 1 else c) for r in range(1, N_DEV)]


def _xchg_copy(src, land, sends, recvs, a, k, me_i, peers, scatter, arriving):
    peer_i = _dev_index(peers[k])
    return pltpu.make_async_remote_copy(
        src_ref=src.at[peer_i] if scatter else src, dst_ref=land.at[peer_i if arriving else me_i],
        send_sem=sends.at[a * 7 + k], recv_sem=recvs.at[a * 7 + k], device_id=peers[k], device_id_type=MESH)


def _xchg_start(arrays, *, scatter, name, after=None):
    A = len(arrays)
    lands = [lax.empty(a.shape if scatter else (N_DEV,) + a.shape, a.dtype) for a in arrays]
    extra = [] if after is None else [after]

    def body(*refs):
        srcs, land = refs[:A], refs[A:2 * A]
        sends, recvs, token = refs[2 * A + len(extra)], refs[2 * A + len(extra) + 1], refs[-1]
        me_i, peers = _peers()
        for k in _PEER_ORDER:
            for a in range(A):
                _xchg_copy(srcs[a], land[a], sends, recvs, a, k, me_i, peers, scatter, False).start()
        token[...] = jnp.zeros_like(token)

    hbm = lambda a: pltpu.HBM(a.shape, a.dtype)
    outs = pl.pallas_call(
        body, name=name, in_specs=[_HBM] * (2 * A) + [pl.BlockSpec(memory_space=pl.ANY)] * len(extra),
        out_shape=(pltpu.SemaphoreType.DMA((7 * A,)), pltpu.SemaphoreType.DMA((7 * A,)), *[hbm(a) for a in arrays],
                   *[hbm(a) for a in lands], jax.ShapeDtypeStruct((8, LANES), F32)),
        out_specs=(_SEM, _SEM, *([_HBM] * (2 * A)), pl.BlockSpec(memory_space=pltpu.VMEM)),
        input_output_aliases={n: 2 + n for n in range(2 * A)},
        compiler_params=pltpu.CompilerParams(has_side_effects=_EFFECT),
    )(*[pltpu.with_memory_space_constraint(a, pltpu.HBM) for a in arrays],
      *[pltpu.with_memory_space_constraint(a, pltpu.HBM) for a in lands], *extra)
    return outs[0], outs[1], list(outs[2:2 + A]), list(outs[2 + A:2 + 2 * A]), outs[-1]


def _xchg_wait(handle, after, *, scatter, name):
    sends, recvs, srcs, lands, _ = handle
    A = len(srcs)

    def body(*refs):
        src, land = refs[:A], refs[A:2 * A]
        send_sems, recv_sems = refs[2 * A], refs[2 * A + 1]
        me_i, peers = _peers()
        for k in _PEER_ORDER:
            for a in range(A):
                cp = _xchg_copy(src[a], land[a], send_sems, recv_sems, a, k, me_i, peers, scatter, True)
                cp.wait_send()
                cp.wait_recv()

    outs = pl.pallas_call(
        body, name=name, in_specs=[_HBM] * (2 * A) + [_SEM, _SEM, pl.BlockSpec(memory_space=pl.ANY)],
        out_shape=tuple(pltpu.HBM(a.shape, a.dtype) for a in srcs + lands), out_specs=tuple([_HBM] * (2 * A)),
        input_output_aliases={n: n for n in range(2 * A)},
        compiler_params=pltpu.CompilerParams(has_side_effects=_EFFECT),
    )(*srcs, *lands, sends, recvs, after)
    my = _dev_index((lax.axis_index("x"), lax.axis_index("y"), lax.axis_index("c")))
    got = []
    for src, land in zip(outs[:A], outs[A:]):
        own = lax.dynamic_index_in_dim(src, my, 0, keepdims=True) if scatter else src[None]
        got.append(lax.dynamic_update_index_in_dim(land, own, my, 0))
    return got


def _adamw(g, w, m, v):
    m = ADAM_B1 * m + (1.0 - ADAM_B1) * g
    v = ADAM_B2 * v + (1.0 - ADAM_B2) * (g * g)
    delta = -ADAM_LR * ((m / ADAM_BC1) / (jnp.sqrt(v / ADAM_BC2) + ADAM_EPS) + ADAM_WD * w)
    return delta, m, v


def _adam_pieces(pieces, w, m, v, bufs, *, layer, name):
    depth, R, C = w.shape
    if R % 16 == 0:
        tr, tc = _pick(R, [t for t in (512, 256, 128, 64, 32, 16) if t * C * 4 <= 2**20]), C
    else:
        tr, tc = R, _pick(C, [t for t in (512, 256, 128) if R * t * 4 <= 2**21])

    def body(p_ref, w_ref, m_ref, v_ref, b0, b1, b2, b3, g_ref, d_ref, mo_ref, vo_ref):
        g = p_ref[0].astype(F32)
        for k in range(1, N_DEV):
            g = g + p_ref[k].astype(F32)
        g_ref[0] = g
        d_ref[0], mo_ref[0], vo_ref[0] = _adamw(g, w_ref[0], m_ref[0], v_ref[0])

    blk = pl.BlockSpec((1, tr, tc), lambda i, j: (layer, i, j))
    anyspec = pl.BlockSpec(memory_space=pl.ANY)
    return pl.pallas_call(
        body, grid=(R // tr, C // tc),
        in_specs=[pl.BlockSpec((N_DEV, tr, tc), lambda i, j: (0, i, j)), blk, blk, blk] + [anyspec] * 4,
        out_specs=[blk] * 4, out_shape=[jax.ShapeDtypeStruct((depth, R, C), F32)] * 4,
        input_output_aliases={4: 0, 5: 1, 6: 2, 7: 3}, compiler_params=_cp("parallel", "parallel"), name=name,
    )(pieces, w, m, v, *bufs)


def _adam_direct(g, w, m, v, *, name):
    sh = jax.ShapeDtypeStruct(w.shape, F32)

    def body(g_ref, w_ref, m_ref, v_ref, d_ref, mo_ref, vo_ref):
        d_ref[...], mo_ref[...], vo_ref[...] = _adamw(g_ref[...], w_ref[...], m_ref[...], v_ref[...])

    return pl.pallas_call(body, out_shape=[sh] * 3, compiler_params=_cp(), name=name)(g, w, m, v)


def _sum_devices(parts, *, name):
    def body(p_ref, o_ref):
        s = p_ref[0]
        for k in range(1, N_DEV):
            s = s + p_ref[k]
        o_ref[...] = s

    return pl.pallas_call(body, out_shape=jax.ShapeDtypeStruct(parts.shape[1:], F32), compiler_params=_cp(), name=name)(parts)


def _rope_cols_pad(w):
    z = jnp.zeros(w.shape[:-1] + (ROPE // 2,), w.dtype)
    return jnp.concatenate([w[..., :ROPE // 2], z, w[..., ROPE // 2:], z], axis=-1)


def _rope_cols_unpad(w):
    return jnp.concatenate([w[..., :ROPE // 2], w[..., ROPE:ROPE + ROPE // 2]], axis=-1)


class _Dims:
    def __init__(self, D, S, NM, QL, KVL, WM, WS):
        self.D, self.S, self.NM, self.QL, self.KVL, self.WM, self.WS = D, S, NM, QL, KVL, WM, WS
        self.H = WM // LANES
        self.HW = self.H * LANES
        self.L = NM + S
        self.bq = 384 if self.L > 1024 else 128
        self.Lp = -(-self.L // self.bq) * self.bq
        self.ck_sb = LANES
        self.tr = 128
        self.sizes = (QL, KVL, ROPE, WM, WS, WS, WS, WS)
        self.d_in = sum(self.sizes)
        front = QL + KVL + LANES
        self.front = -(-front // WM) * WM
        self.pad = self.front - front
        self.dn = self.front + WM + 4 * WS
        assert WM == WS and QL % KVL == 0 and KVL % LANES == 0 and self.d_in % N_DEV == 0 and D % N_DEV == 0
        self.mla_scale = 1.0 / math.sqrt(LANES + ROPE)
        self.sb_scale = 1.0 / math.sqrt(LANES)


def _w_in_nice(d, wt):
    parts = jnp.split(wt, np.cumsum(d.sizes)[:-1].tolist(), axis=0)
    z = jnp.zeros((ROPE // 2, wt.shape[1]), wt.dtype)
    k_r = [parts[2][:ROPE // 2], z, parts[2][ROPE // 2:], z]
    return jnp.concatenate([parts[0], parts[1]] + k_r + [jnp.zeros((d.pad, wt.shape[1]), wt.dtype)] + parts[3:], axis=0)


def _w_in_orig(d, wt):
    a, b = d.QL + d.KVL, d.front
    return jnp.concatenate([wt[:a + ROPE // 2], wt[a + ROPE:a + ROPE + ROPE // 2], wt[b:]], axis=0)


def _w_uq_nice(d, w):
    w3 = w.reshape(d.QL, d.H, LANES + ROPE)
    return jnp.concatenate([w3[..., :LANES], _rope_cols_pad(w3[..., LANES:])], axis=-1).reshape(d.QL, d.H * QKW)


def _w_uq_orig(d, w):
    w3 = w.reshape(d.QL, d.H, QKW)
    return jnp.concatenate([w3[..., :LANES], _rope_cols_unpad(w3[..., LANES:])], axis=-1).reshape(d.QL, d.H * (LANES + ROPE))


def _w_ukv_nice(d, w):
    return w.reshape(d.KVL, d.H, 2, LANES).transpose(0, 2, 1, 3).reshape(d.KVL, 2 * d.HW)


def _w_ukv_orig(d, w):
    return w.reshape(d.KVL, 2, d.H, LANES).transpose(0, 2, 1, 3).reshape(d.KVL, 2 * d.HW)


def _cols_from_shards(g):
    return g.transpose(1, 0, 2).reshape(g.shape[1], -1)


def _cols_to_shards(w):
    return w.reshape(w.shape[0], N_DEV, -1).transpose(1, 0, 2)


def _rope_tables(rows):
    inv_freq = ROPE_THETA ** (-jnp.arange(0, ROPE, 2, dtype=F32) / ROPE)
    ang = jnp.arange(rows, dtype=jnp.int32).astype(F32)[:, None] * inv_freq[None, :]
    cos, sin, z = jnp.cos(ang), jnp.sin(ang), jnp.zeros_like(ang)
    return jnp.concatenate([cos, z, cos, z], axis=1), jnp.concatenate([-sin, z, sin, z], axis=1)


def _flat_pack(vecs):
    flat = jnp.concatenate([v.reshape(-1).astype(F32) for v in vecs])
    n = flat.shape[0]
    rows = -(-n // (8 * 1024)) * 8
    offs = np.cumsum([0] + [int(np.prod(v.shape)) for v in vecs])
    return jnp.pad(flat, (0, rows * 1024 - n)).reshape(rows, 1024), offs


def kernel(x, meta_tokens, g_norm, w_in, g_q, g_kv, w_uq, w_ukv, g_out_mla, g_out_sb, w_o, g_final, loss_target, m_meta_tokens, m_g_norm, m_w_in, m_g_q, m_g_kv, m_w_uq, m_w_ukv, m_g_out_mla, m_g_out_sb, m_w_o, m_g_final, v_meta_tokens, v_g_norm, v_w_in, v_g_q, v_g_kv, v_w_uq, v_w_ukv, v_g_out_mla, v_g_out_sb, v_w_o, v_g_final):
    depth = g_norm.shape[0]
    d = _Dims(D=x.shape[-1], S=x.shape[1], NM=meta_tokens.shape[0], QL=g_q.shape[1], KVL=g_kv.shape[1],
              WM=g_out_mla.shape[1], WS=g_out_sb.shape[1])
    D, Lp, H, tr = d.D, d.Lp, d.H, d.tr
    my = _dev_index((lax.axis_index("x"), lax.axis_index("y"), lax.axis_index("c")))

    w_in_t, m_w_in_t, v_w_in_t = (jnp.swapaxes(a, 1, 2) for a in (w_in, m_w_in, v_w_in))
    shards = lambda l: [w_in_t[l].astype(CD), w_uq[l].astype(CD), w_ukv[l].astype(CD), w_o[l].astype(CD)]
    g_in0, g_meta = _all_gather([shards(0)[0], meta_tokens], name="gather_layer0")
    gathers, token = {}, None
    for l in range(depth):
        gathers[l] = _xchg_start(shards(l)[1:] if l == 0 else shards(l), scatter=False, name=f"gather_start_{l}", after=token)
        token = gathers[l][4]
    started = token[0, 0]

    def relayout(g_uq, g_ukv, g_o):
        return _w_uq_nice(d, _cols_from_shards(g_uq)), _w_ukv_nice(d, _cols_from_shards(g_ukv)), g_o.reshape(D, D)

    wn, wuq, wukv, wo = [None] * depth, [None] * depth, [None] * depth, [None] * depth
    meta_full = _cols_from_shards(g_meta)

    cs, sn = _rope_tables(Lp)
    h = jnp.concatenate([meta_full, x[0], jnp.zeros((Lp - d.L, D), F32)], axis=0)
    tgt = jnp.concatenate([jnp.zeros((d.NM, D), F32), loss_target[0], jnp.zeros((Lp - d.L, D), F32)], axis=0)

    b128 = lambda col: col // LANES
    krb = b128(d.QL + d.KVL)
    zb_mla, zb_sb = d.front // d.WM, (d.front + d.WM + 3 * d.WS) // d.WS
    qb0, kb0, vb0 = b128(d.front + d.WM), b128(d.front + d.WM + d.WS), b128(d.front + d.WM + 2 * d.WS)
    row = lambda a, i: a[i][None, :]

    saved = []
    for i in range(depth):
        if i == 0:
            wn[i] = _w_in_nice(d, g_in0.reshape(d.d_in, D))
            u = _rms_fwd(h, row(g_norm, i) + started, cb=0, width=D, tr=tr, name=f"rms_in_{i}")
            proj = _mm(u, wn[i], tb=True, out_dtype=CD, name=f"proj_{i}")
            wuq[i], wukv[i], wo[i] = relayout(*_xchg_wait(gathers[i], proj, scatter=False, name=f"gather_wait_{i}"))
        else:
            g_in, *g_rest = _xchg_wait(gathers[i], h, scatter=False, name=f"gather_wait_{i}")
            wn[i] = _w_in_nice(d, g_in.reshape(d.d_in, D))
            wuq[i], wukv[i], wo[i] = relayout(*g_rest)
            u = _rms_fwd(h, row(g_norm, i), cb=0, width=D, tr=tr, name=f"rms_in_{i}")
            proj = _mm(u, wn[i], tb=True, out_dtype=CD, name=f"proj_{i}")
        cqn = _rms_fwd(proj, row(g_q, i), cb=0, width=d.QL, tr=tr, name=f"rms_q_{i}")
        ckvn = _rms_fwd(proj, row(g_kv, i), cb=d.QL // d.KVL, width=d.KVL, tr=tr, name=f"rms_kv_{i}")
        q_raw = _mm(cqn, wuq[i], out_dtype=F32, name=f"up_q_{i}")
        kvn = _mm(ckvn, wukv[i], out_dtype=CD, name=f"up_kv_{i}")
        q, kcat = _rope_fwd(q_raw, kvn, proj, cs, sn, H=H, krb=krb, tr=tr, name=f"rope_{i}")
        o_mla, lse = _mla_fwd(q, kcat, kvn, H=H, bq=d.bq, scale=d.mla_scale, name=f"mla_fwd_{i}")
        o_sb, tot = _sb_fwd(proj, H=H, qb0=qb0, kb0=kb0, vb0=vb0, bq=d.bq, ck=d.ck_sb, scale=d.sb_scale, name=f"sb_fwd_{i}")
        y = _gate_fwd(o_mla, o_sb, proj, row(g_out_mla, i), row(g_out_sb, i), zb_mla=zb_mla, zb_sb=zb_sb, tr=tr, name=f"gate_fwd_{i}")
        h_next = _mm(y, wo[i], out_dtype=F32, res=h, name=f"out_proj_{i}")
        saved.append((h, u, proj, cqn, ckvn, q, kvn, kcat, o_mla, lse, o_sb, tot, y))
        h = h_next

    dh, dg_final, loss_part = _loss_head(h, g_final[None, :], tgt, lo=d.NM, hi=d.L, tr=tr, name="loss_head")

    dg_norm, dg_q, dg_kv, dg_om, dg_os = [None] * depth, [None] * depth, [None] * depth, [None] * depth, [None] * depth
    big_w = {"w_in": (w_in_t, m_w_in_t, v_w_in_t), "w_uq": (w_uq, m_w_uq, v_w_uq), "w_ukv": (w_ukv, m_w_ukv, v_w_ukv), "w_o": (w_o, m_w_o, v_w_o)}
    res = {n: [lax.empty(w.shape, F32) for _ in range(4)] for n, (w, _, _) in big_w.items()}
    exchanges = [[] for _ in range(depth)]

    def send(l, tag, pieces, after=None):
        handle = _xchg_start(list(pieces.values()), scatter=True, name=f"exch_start_{l}{tag}", after=after)
        exchanges[l].append((handle, list(pieces), tag))
        return handle[4]

    def finish(l, after):
        got = None
        for handle, sent, tag in exchanges[l]:
            got = _xchg_wait(handle, after, scatter=True, name=f"exch_wait_{l}{tag}")
            for n, p in zip(sent, got):
                w, m, v = big_w[n]
                res[n] = _adam_pieces(p, w, m, v, res[n], layer=l, name=f"adam_{n}_{l}")
        return got[0]

    cat = lambda parts: jnp.concatenate(parts, axis=0)
    zpad = jnp.zeros((Lp, d.pad), CD)
    for i in reversed(range(depth)):
        h_in, u, proj, cqn, ckvn, q, kvn, kcat, o_mla, lse, o_sb, tot, y = saved[i]
        dy = _mm(dh, wo[i], tb=True, out_dtype=CD, name=f"d_y_{i}")
        dwo = _mm(y, dh, ta=True, out_dtype=CD, name=f"d_wo_{i}")
        do_mla, do_sb, dz_mla, dz_sb, dg_om[i], dg_os[i] = _gate_bwd(
            dy, o_mla, o_sb, proj, row(g_out_mla, i), row(g_out_sb, i), zb_mla=zb_mla, zb_sb=zb_sb, tr=tr, name=f"gate_bwd_{i}")
        dq_mla, dkcat, dv = _mla_bwd(q, kcat, kvn, o_mla, do_mla, lse, H=H, bq=d.bq, scale=d.mla_scale, name=f"mla_bwd_{i}")
        dq_sb, dk_sb, dv_sb = _sb_bwd(proj, do_sb, tot, H=H, qb0=qb0, kb0=kb0, vb0=vb0, bq=d.bq, ck=d.ck_sb, scale=d.sb_scale, name=f"sb_bwd_{i}")
        dq, dkv, dk_r = _rope_bwd(dq_mla, dkcat, dv, cs, sn, H=H, tr=tr, name=f"rope_bwd_{i}")
        dcqn = _mm(dq, wuq[i], tb=True, out_dtype=CD, name=f"d_cq_{i}")
        dwuq = _mm(cqn, dq, ta=True, out_dtype=CD, name=f"d_wuq_{i}")
        dckvn = _mm(dkv, wukv[i], tb=True, out_dtype=CD, name=f"d_ckv_{i}")
        dwukv = _mm(ckvn, dkv, ta=True, out_dtype=CD, name=f"d_wukv_{i}")
        dc_q, dg_q[i] = _rms_bwd(proj, row(g_q, i), dcqn, cb=0, width=d.QL, tr=tr, out_dtype=CD, name=f"rms_q_bwd_{i}")
        dc_kv, dg_kv[i] = _rms_bwd(proj, row(g_kv, i), dckvn, cb=d.QL // d.KVL, width=d.KVL, tr=tr, out_dtype=CD, name=f"rms_kv_bwd_{i}")
        rest = {"w_uq": _cols_to_shards(_w_uq_orig(d, dwuq)), "w_ukv": _cols_to_shards(_w_ukv_orig(d, dwukv)),
                "w_o": dwo.reshape(N_DEV, D // N_DEV, D)}
        placed = None
        if i == 0:
            small_a, offs_a = _flat_pack(([cat(dg_norm[1:])] if depth > 1 else []) + [cat(dg_q), cat(dg_kv), cat(dg_om), cat(dg_os), dg_final, loss_part])
            (small_a_all,) = _all_gather([small_a], name="gather_small_a")
            placed = send(0, "a", rest, after=small_a_all)
        dproj = jnp.concatenate([dc_q, dc_kv, dk_r, zpad, dz_mla, dq_sb, dk_sb.astype(CD), dv_sb.astype(CD), dz_sb], axis=1)
        dwn = _mm(dproj, u, ta=True, out_dtype=CD, after=placed, name=f"d_wn_{i}")
        pieces = {"w_in": _w_in_orig(d, dwn).reshape(N_DEV, d.d_in // N_DEV, D)}
        if i > 0:
            pieces.update(rest)
        du = _mm(dproj, wn[i], out_dtype=CD, after=send(i, "b", pieces), name=f"d_u_{i}")
        dh, dg_norm[i] = _rms_bwd(h_in, row(g_norm, i), du, cb=0, width=D, tr=tr, res=dh, out_dtype=F32, name=f"rms_in_bwd_{i}")
        if i + 1 < depth:
            finish(i + 1, dh)
    landed = finish(0, dh)

    small_b, offs_b = _flat_pack([dg_norm[0], dh[:d.NM]])
    (small_b_all,) = _all_gather([small_b], name="gather_small_b", after=landed)
    sum_a = _sum_devices(small_a_all, name="sum_small_a").reshape(-1)
    sum_b = _sum_devices(small_b_all, name="sum_small_b").reshape(-1)
    n_a = int(offs_a[-2])
    small_sum = jnp.concatenate([sum_b[:D], sum_a[:n_a], sum_b[D:offs_b[2]], sum_a[n_a:n_a + 1]])
    offs = np.cumsum([0, depth * D, depth * d.QL, depth * d.KVL, depth * d.WM, depth * d.WS, D, d.NM * D, 1])
    seg = lambda k, shape: small_sum[offs[k]:offs[k + 1]].reshape(shape)
    loss = small_sum[offs[7]]
    grad_x = dh[d.NM:d.L][None]

    g_meta_full = seg(6, (d.NM, D))
    ncol = meta_tokens.shape[1]
    g_meta_mine = lax.dynamic_slice(g_meta_full, (0, my * ncol), (d.NM, ncol))
    res["meta_tokens"] = [g_meta_mine] + list(_adam_direct(g_meta_mine, meta_tokens, m_meta_tokens, v_meta_tokens, name="adam_meta"))

    small_names = ["g_norm", "g_q", "g_kv", "g_out_mla", "g_out_sb", "g_final"]
    small_w = [g_norm, g_q, g_kv, g_out_mla, g_out_sb, g_final]
    small_m = [m_g_norm, m_g_q, m_g_kv, m_g_out_mla, m_g_out_sb, m_g_final]
    small_v = [v_g_norm, v_g_q, v_g_kv, v_g_out_mla, v_g_out_sb, v_g_final]
    g_flat, _ = _flat_pack([small_sum[:offs[6]]])
    wf, _ = _flat_pack(small_w)
    mf, _ = _flat_pack(small_m)
    vf, _ = _flat_pack(small_v)
    d_f, m_f, v_f = _adam_direct(g_flat, wf, mf, vf, name="adam_small")
    for k, nm in enumerate(small_names):
        sl = lambda f: f.reshape(-1)[offs[k]:offs[k + 1]].reshape(small_w[k].shape)
        res[nm] = [sl(g_flat), sl(d_f), sl(m_f), sl(v_f)]

    names = ["meta_tokens", "g_norm", "w_in", "g_q", "g_kv", "w_uq", "w_ukv", "g_out_mla", "g_out_sb", "w_o", "g_final"]
    out = lambda n, j: jnp.swapaxes(res[n][j], 1, 2) if n == "w_in" else res[n][j]
    return (loss, grad_x, *[out(n, 0) for n in names], *[out(n, 1) for n in names],
            *[out(n, 2) for n in names], *[out(n, 3) for n in names])
```

```python
import math

import numpy as np
import jax
import jax.numpy as jnp
from jax import lax
from jax.experimental import pallas as pl
from jax.experimental.pallas import tpu as pltpu

CD = jnp.bfloat16
F32 = jnp.float32

LANES = 128
N_DEV = 8
EPS = 1e-6
ROPE = 64
ROPE_THETA = 10000.0
NEG = -1e30
VMEM_LIMIT = 48 * 2**20
VMEM_LIMIT_WIDE = 56 * 2**20

ADAM_LR, ADAM_B1, ADAM_B2, ADAM_EPS, ADAM_WD, ADAM_STEP = 0.001, 0.9, 0.999, 1e-08, 0.01, 10
ADAM_BC1 = 1.0 - ADAM_B1**ADAM_STEP
ADAM_BC2 = 1.0 - ADAM_B2**ADAM_STEP

MESH = pl.DeviceIdType.MESH


def _cp(*sem):
    return pltpu.CompilerParams(dimension_semantics=tuple(sem) if sem else None, vmem_limit_bytes=VMEM_LIMIT)


def _pick(dim, prefs):
    for p in prefs:
        if dim % p == 0:
            return p
    raise ValueError(f"no tile for {dim} in {prefs}")


def _dot(a, b):
    return jnp.dot(a, b, preferred_element_type=F32)


def _dot_nt(a, b):
    return lax.dot_general(a, b, (((1,), (1,)), ((), ())), preferred_element_type=F32)


def _dot_tn(a, b):
    return lax.dot_general(a, b, (((0,), (0,)), ((), ())), preferred_element_type=F32)


def _mm(a, b, *, ta=False, tb=False, out_dtype, res=None, after=None, name):
    M, K = (a.shape[1], a.shape[0]) if ta else a.shape
    N = b.shape[0] if tb else b.shape[1]
    tm = _pick(M, (1056, 1024, 512, 256, 128))
    tn = _pick(N, (1024, 512, 384, 256, 128))
    tk = _pick(K, (2112, 1056, 1024, 512, 384, 256, 128) if ta else (2048, 1024, 512, 384, 256, 128))
    nk = K // tk
    a_dims = (((0,), (1 if tb else 0,)), ((), ())) if ta else (((1,), (1 if tb else 0,)), ((), ()))

    def body(*refs):
        if after is not None:
            refs = refs[:-3] + refs[-2:]
        if res is None:
            a_ref, b_ref, o_ref, acc = refs
        else:
            a_ref, b_ref, r_ref, o_ref, acc = refs
        k = pl.program_id(2)

        @pl.when(k == 0)
        def _():
            acc[...] = jnp.zeros_like(acc)

        acc[...] += lax.dot_general(a_ref[...].astype(CD), b_ref[...].astype(CD), a_dims, preferred_element_type=F32)

        @pl.when(k == nk - 1)
        def _():
            r = acc[...]
            if res is not None:
                r = r + r_ref[...]
            o_ref[...] = r.astype(out_dtype)

    a_spec = pl.BlockSpec((tk, tm), lambda i, j, k: (k, i)) if ta else pl.BlockSpec((tm, tk), lambda i, j, k: (i, k))
    b_spec = pl.BlockSpec((tn, tk), lambda i, j, k: (j, k)) if tb else pl.BlockSpec((tk, tn), lambda i, j, k: (k, j))
    o_spec = pl.BlockSpec((tm, tn), lambda i, j, k: (i, j))
    in_specs, args = [a_spec, b_spec], [a, b]
    if res is not None:
        in_specs.append(o_spec)
        args.append(res)
    if after is not None:
        in_specs.append(pl.BlockSpec(memory_space=pl.ANY))
        args.append(after)
    return pl.pallas_call(
        body, grid=(M // tm, N // tn, nk), in_specs=in_specs, out_specs=o_spec,
        out_shape=jax.ShapeDtypeStruct((M, N), out_dtype), scratch_shapes=[pltpu.VMEM((tm, tn), F32)],
        compiler_params=_cp("parallel", "parallel", "arbitrary"), name=name,
    )(*args)


def _rms_fwd(x, g, *, cb, width, tr, name):
    rows = x.shape[0]

    def body(x_ref, g_ref, o_ref):
        xv = x_ref[...].astype(F32)
        r = lax.rsqrt(jnp.mean(xv * xv, axis=-1, keepdims=True) + EPS)
        o_ref[...] = (xv * r * g_ref[...]).astype(o_ref.dtype)

    return pl.pallas_call(
        body, grid=(rows // tr,),
        in_specs=[pl.BlockSpec((tr, width), lambda i: (i, cb)), pl.BlockSpec((1, width), lambda i: (0, 0))],
        out_specs=pl.BlockSpec((tr, width), lambda i: (i, 0)),
        out_shape=jax.ShapeDtypeStruct((rows, width), CD), compiler_params=_cp("parallel"), name=name,
    )(x, g)


def _rms_bwd(x, g, dy, *, cb, width, tr, res=None, out_dtype, name):
    rows = x.shape[0]

    def body(*refs):
        if res is None:
            x_ref, g_ref, dy_ref, dx_ref, dg_ref = refs
        else:
            x_ref, g_ref, dy_ref, r_ref, dx_ref, dg_ref = refs

        @pl.when(pl.program_id(0) == 0)
        def _():
            dg_ref[...] = jnp.zeros_like(dg_ref)

        xv = x_ref[...].astype(F32)
        r = lax.rsqrt(jnp.mean(xv * xv, axis=-1, keepdims=True) + EPS)
        xh = xv * r
        dyv = dy_ref[...].astype(F32)
        dxh = dyv * g_ref[...]
        dx = r * (dxh - xh * jnp.mean(dxh * xh, axis=-1, keepdims=True))
        if res is not None:
            dx = dx + r_ref[...]
        dx_ref[...] = dx.astype(out_dtype)
        dg_ref[...] += jnp.sum(dyv * xh, axis=0, keepdims=True)

    blk = pl.BlockSpec((tr, width), lambda i: (i, 0))
    in_specs = [pl.BlockSpec((tr, width), lambda i: (i, cb)), pl.BlockSpec((1, width), lambda i: (0, 0)), blk]
    args = [x, g, dy]
    if res is not None:
        in_specs.append(blk)
        args.append(res)
    return pl.pallas_call(
        body, grid=(rows // tr,), in_specs=in_specs,
        out_specs=[blk, pl.BlockSpec((1, width), lambda i: (0, 0))],
        out_shape=[jax.ShapeDtypeStruct((rows, width), out_dtype), jax.ShapeDtypeStruct((1, width), F32)],
        compiler_params=_cp("arbitrary"), name=name,
    )(*args)


def _sigmoid(z):
    return 1.0 / (1.0 + jnp.exp(-z))


def _gate_fwd(o_mla, o_sb, proj, g_mla, g_sb, *, zb_mla, zb_sb, tr, name):
    rows, W = o_mla.shape

    def body(om_ref, os_ref, zm_ref, zs_ref, gm_ref, gs_ref, y_ref):
        for half, (o_ref, z_ref, g_ref) in enumerate(((om_ref, zm_ref, gm_ref), (os_ref, zs_ref, gs_ref))):
            o = o_ref[...].astype(F32)
            r = lax.rsqrt(jnp.mean(o * o, axis=-1, keepdims=True) + EPS)
            z = z_ref[...].astype(F32)
            y_ref[:, half * W:(half + 1) * W] = ((o * r * g_ref[...]) * (z * _sigmoid(z))).astype(y_ref.dtype)

    blk = pl.BlockSpec((tr, W), lambda i: (i, 0))
    gsp = pl.BlockSpec((1, W), lambda i: (0, 0))
    return pl.pallas_call(
        body, grid=(rows // tr,),
        in_specs=[blk, blk, pl.BlockSpec((tr, W), lambda i: (i, zb_mla)), pl.BlockSpec((tr, W), lambda i: (i, zb_sb)), gsp, gsp],
        out_specs=pl.BlockSpec((tr, 2 * W), lambda i: (i, 0)),
        out_shape=jax.ShapeDtypeStruct((rows, 2 * W), CD), compiler_params=_cp("parallel"), name=name,
    )(o_mla, o_sb, proj, proj, g_mla, g_sb)


def _gate_bwd(dy, o_mla, o_sb, proj, g_mla, g_sb, *, zb_mla, zb_sb, tr, name):
    rows, W = o_mla.shape

    def body(dy_ref, om_ref, os_ref, zm_ref, zs_ref, gm_ref, gs_ref, dom_ref, dos_ref, dzm_ref, dzs_ref, dgm_ref, dgs_ref):
        @pl.when(pl.program_id(0) == 0)
        def _():
            dgm_ref[...] = jnp.zeros_like(dgm_ref)
            dgs_ref[...] = jnp.zeros_like(dgs_ref)

        halves = ((om_ref, zm_ref, gm_ref, dom_ref, dzm_ref, dgm_ref), (os_ref, zs_ref, gs_ref, dos_ref, dzs_ref, dgs_ref))
        for half, (o_ref, z_ref, g_ref, do_ref, dz_ref, dg_ref) in enumerate(halves):
            dyv = dy_ref[:, half * W:(half + 1) * W].astype(F32)
            o = o_ref[...].astype(F32)
            r = lax.rsqrt(jnp.mean(o * o, axis=-1, keepdims=True) + EPS)
            oh = o * r
            g = g_ref[...]
            z = z_ref[...].astype(F32)
            sg = _sigmoid(z)
            dn = dyv * (z * sg)
            dz_ref[...] = (dyv * (oh * g) * (sg * (1.0 + z * (1.0 - sg)))).astype(dz_ref.dtype)
            dxh = dn * g
            do_ref[...] = (r * (dxh - oh * jnp.mean(dxh * oh, axis=-1, keepdims=True))).astype(do_ref.dtype)
            dg_ref[...] += jnp.sum(dn * oh, axis=0, keepdims=True)

    blk = pl.BlockSpec((tr, W), lambda i: (i, 0))
    gsp = pl.BlockSpec((1, W), lambda i: (0, 0))
    act = jax.ShapeDtypeStruct((rows, W), CD)
    gsh = jax.ShapeDtypeStruct((1, W), F32)
    return pl.pallas_call(
        body, grid=(rows // tr,),
        in_specs=[pl.BlockSpec((tr, 2 * W), lambda i: (i, 0)), blk, blk,
                  pl.BlockSpec((tr, W), lambda i: (i, zb_mla)), pl.BlockSpec((tr, W), lambda i: (i, zb_sb)), gsp, gsp],
        out_specs=[blk, blk, blk, blk, gsp, gsp], out_shape=[act, act, act, act, gsh, gsh],
        compiler_params=_cp("arbitrary"), name=name,
    )(dy, o_mla, o_sb, proj, proj, g_mla, g_sb)


QKW = 2 * LANES


def _rope_fwd(q_raw, kv, proj, cs, sn, *, H, krb, tr, name):
    rows = q_raw.shape[0]

    def body(q_ref, kn_ref, kr_ref, c_ref, s_ref, qo_ref, ko_ref):
        c, s = c_ref[...], s_ref[...]
        xk = kr_ref[...].astype(F32)
        kr = (xk * c + pltpu.roll(xk, ROPE, 1) * s).astype(ko_ref.dtype)
        for h in range(H):
            nope, rope = slice(h * QKW, h * QKW + LANES), slice(h * QKW + LANES, (h + 1) * QKW)
            qo_ref[:, nope] = q_ref[:, nope].astype(qo_ref.dtype)
            xh = q_ref[:, rope]
            qo_ref[:, rope] = (xh * c + pltpu.roll(xh, ROPE, 1) * s).astype(qo_ref.dtype)
            ko_ref[:, nope] = kn_ref[:, _head(h)]
            ko_ref[:, rope] = kr

    tab = pl.BlockSpec((tr, LANES), lambda i: (i, 0))
    wide = pl.BlockSpec((tr, H * QKW), lambda i: (i, 0))
    sh = jax.ShapeDtypeStruct((rows, H * QKW), CD)
    return pl.pallas_call(
        body, grid=(rows // tr,),
        in_specs=[wide, pl.BlockSpec((tr, H * LANES), lambda i: (i, 0)), pl.BlockSpec((tr, LANES), lambda i: (i, krb)), tab, tab],
        out_specs=[wide, wide], out_shape=[sh, sh], compiler_params=_cp("parallel"), name=name,
    )(q_raw, kv, proj, cs, sn)


def _rope_bwd(dq, dkcat, dv, cs, sn, *, H, tr, name):
    rows = dq.shape[0]
    HW = H * LANES

    def body(dq_ref, dk_ref, dv_ref, c_ref, s_ref, dqo_ref, dkv_ref, dkr_ref):
        c, s = c_ref[...], s_ref[...]
        dkr = jnp.zeros((tr, LANES), F32)
        for h in range(H):
            nope, rope = slice(h * QKW, h * QKW + LANES), slice(h * QKW + LANES, (h + 1) * QKW)
            dqo_ref[:, nope] = dq_ref[:, nope]
            d = dq_ref[:, rope].astype(F32)
            dqo_ref[:, rope] = (d * c + pltpu.roll(d * s, ROPE, 1)).astype(dqo_ref.dtype)
            dkv_ref[:, _head(h)] = dk_ref[:, nope].astype(dkv_ref.dtype)
            dkr = dkr + dk_ref[:, rope]
        dkv_ref[:, HW:] = dv_ref[...].astype(dkv_ref.dtype)
        dkr_ref[...] = (dkr * c + pltpu.roll(dkr * s, ROPE, 1)).astype(dkr_ref.dtype)

    tab = pl.BlockSpec((tr, LANES), lambda i: (i, 0))
    wide = pl.BlockSpec((tr, H * QKW), lambda i: (i, 0))
    return pl.pallas_call(
        body, grid=(rows // tr,), in_specs=[wide, wide, pl.BlockSpec((tr, HW), lambda i: (i, 0)), tab, tab],
        out_specs=[wide, wide, tab],
        out_shape=[jax.ShapeDtypeStruct((rows, H * QKW), CD), jax.ShapeDtypeStruct((rows, 2 * HW), CD), jax.ShapeDtypeStruct((rows, LANES), CD)],
        compiler_params=_cp("parallel"), name=name,
    )(dq, dkcat, dv, cs, sn)


def _loss_head(h, g, tgt, *, lo, hi, tr, name):
    rows, D = h.shape

    def body(h_ref, g_ref, t_ref, dh_ref, dg_ref, loss_ref):
        i = pl.program_id(0)

        @pl.when(i == 0)
        def _():
            dg_ref[...] = jnp.zeros_like(dg_ref)
            loss_ref[...] = jnp.zeros_like(loss_ref)

        xv = h_ref[...]
        r = lax.rsqrt(jnp.mean(xv * xv, axis=-1, keepdims=True) + EPS)
        xh = xv * r
        g_ = g_ref[...]
        rowid = i * tr + lax.broadcasted_iota(jnp.int32, (tr, 1), 0)
        valid = jnp.logical_and(rowid >= lo, rowid < hi)
        err = jnp.where(valid, xh * g_ - t_ref[...], 0.0)
        loss_ref[...] += jnp.sum(jnp.sum(err * err, axis=-1, keepdims=True), axis=0, keepdims=True) * (0.5 / D)
        dy = err * (1.0 / D)
        dxh = dy * g_
        dh_ref[...] = r * (dxh - xh * jnp.mean(dxh * xh, axis=-1, keepdims=True))
        dg_ref[...] += jnp.sum(dy * xh, axis=0, keepdims=True)

    blk = pl.BlockSpec((tr, D), lambda i: (i, 0))
    gsp = pl.BlockSpec((1, D), lambda i: (0, 0))
    return pl.pallas_call(
        body, grid=(rows // tr,), in_specs=[blk, gsp, blk],
        out_specs=[blk, gsp, pl.BlockSpec((1, 1), lambda i: (0, 0))],
        out_shape=[jax.ShapeDtypeStruct((rows, D), F32), jax.ShapeDtypeStruct((1, D), F32), jax.ShapeDtypeStruct((1, 1), F32)],
        compiler_params=_cp("arbitrary"), name=name,
    )(h, g, tgt)


HP = 2
HPW = HP * LANES
HP_FWD = 4


def _head(hh):
    return slice(hh * LANES, (hh + 1) * LANES)


def _qk(hh):
    return slice(hh * QKW, (hh + 1) * QKW)


def _mla_fwd(q, kcat, kv, *, H, bq, scale, name):
    rows = q.shape[0]
    HP, HPW = HP_FWD, HP_FWD * LANES

    def body(q_ref, k_ref, v_ref, o_ref, lse_ref):
        i = pl.program_id(1)
        causal = lax.broadcasted_iota(jnp.int32, (bq, bq), 1) <= lax.broadcasted_iota(jnp.int32, (bq, bq), 0)

        def block(j, carry, diag):
            off = pl.multiple_of(j * bq, bq)
            out = []
            for hh in range(HP):
                m, l, acc = carry[hh]
                s = _dot_nt(q_ref[:, _qk(hh)], k_ref[pl.ds(off, bq), _qk(hh)]) * scale
                if diag:
                    s = jnp.where(causal, s, NEG)
                m_new = jnp.maximum(m, jnp.max(s, axis=-1, keepdims=True))
                p = jnp.exp(s - m_new)
                alpha = jnp.exp(m - m_new)
                l = alpha * l + jnp.sum(p, axis=-1, keepdims=True)
                acc = alpha * acc + _dot(p.astype(CD), v_ref[pl.ds(off, bq), _head(hh)])
                out.append((m_new, l, acc))
            return tuple(out)

        init = tuple((jnp.full((bq, 1), NEG, F32), jnp.zeros((bq, 1), F32), jnp.zeros((bq, LANES), F32)) for _ in range(HP))
        carry = lax.fori_loop(0, i, lambda j, cr: block(j, cr, False), init)
        carry = block(i, carry, True)
        for hh in range(HP):
            m, l, acc = carry[hh]
            o_ref[:, _head(hh)] = (acc / l).astype(o_ref.dtype)
            lse_ref[hh] = m + jnp.log(l)

    return pl.pallas_call(
        body, grid=(H // HP, rows // bq),
        in_specs=[pl.BlockSpec((bq, HP * QKW), lambda h, i: (i, h)), pl.BlockSpec((rows, HP * QKW), lambda h, i: (0, h)),
                  pl.BlockSpec((rows, HPW), lambda h, i: (0, H // HP + h))],
        out_specs=[pl.BlockSpec((bq, HPW), lambda h, i: (i, h)), pl.BlockSpec((HP, bq, 1), lambda h, i: (h, i, 0))],
        out_shape=[jax.ShapeDtypeStruct((rows, H * LANES), CD), jax.ShapeDtypeStruct((H, rows, 1), F32)],
        compiler_params=_cp("arbitrary", "arbitrary"), name=name,
    )(q, kcat, kv)


def _mla_bwd(q, kcat, kv, o, do, lse, *, H, bq, scale, name):
    rows = q.shape[0]

    def body(q_ref, k_ref, v_ref, o_ref, do_ref, lse_ref, dq_ref, dk_ref, dv_ref):
        i = pl.program_id(1)

        @pl.when(i == 0)
        def _():
            dk_ref[...] = jnp.zeros_like(dk_ref)
            dv_ref[...] = jnp.zeros_like(dv_ref)

        causal = lax.broadcasted_iota(jnp.int32, (bq, bq), 1) <= lax.broadcasted_iota(jnp.int32, (bq, bq), 0)
        delta = [jnp.sum(do_ref[:, _head(hh)].astype(F32) * o_ref[:, _head(hh)].astype(F32), axis=-1, keepdims=True) for hh in range(HP)]

        def block(j, carry, diag):
            off = pl.multiple_of(j * bq, bq)
            out = []
            for hh in range(HP):
                q_, do_ = q_ref[:, _qk(hh)], do_ref[:, _head(hh)]
                kj, vj = k_ref[pl.ds(off, bq), _qk(hh)], v_ref[pl.ds(off, bq), _head(hh)]
                p = jnp.exp(_dot_nt(q_, kj) * scale - lse_ref[hh])
                if diag:
                    p = jnp.where(causal, p, 0.0)
                ds = (p * (_dot_nt(do_, vj) - delta[hh]) * scale).astype(CD)
                dk_ref[pl.ds(off, bq), _qk(hh)] += _dot_tn(ds, q_)
                dv_ref[pl.ds(off, bq), _head(hh)] += _dot_tn(p.astype(CD), do_)
                out.append(carry[hh] + _dot(ds, kj))
            return tuple(out)

        init = tuple(jnp.zeros((bq, QKW), F32) for _ in range(HP))
        carry = lax.fori_loop(0, i, lambda j, cr: block(j, cr, False), init)
        carry = block(i, carry, True)
        for hh in range(HP):
            dq_ref[:, _qk(hh)] = carry[hh].astype(dq_ref.dtype)

    qb = pl.BlockSpec((bq, HP * QKW), lambda h, i: (i, h))
    kb = pl.BlockSpec((rows, HP * QKW), lambda h, i: (0, h))
    ob = pl.BlockSpec((bq, HPW), lambda h, i: (i, h))
    return pl.pallas_call(
        body, grid=(H // HP, rows // bq),
        in_specs=[qb, kb, pl.BlockSpec((rows, HPW), lambda h, i: (0, H // HP + h)), ob, ob, pl.BlockSpec((HP, bq, 1), lambda h, i: (h, i, 0))],
        out_specs=[qb, kb, pl.BlockSpec((rows, HPW), lambda h, i: (0, h))],
        out_shape=[jax.ShapeDtypeStruct((rows, H * QKW), CD), jax.ShapeDtypeStruct((rows, H * QKW), F32), jax.ShapeDtypeStruct((rows, H * LANES), F32)],
        compiler_params=pltpu.CompilerParams(dimension_semantics=("arbitrary", "arbitrary"), vmem_limit_bytes=VMEM_LIMIT_WIDE), name=name,
    )(q, kcat, kv, o, do, lse)


def _log_sigmoids(z):
    lb = jnp.minimum(z, 0.0) - jnp.log(1.0 + jnp.exp(-jnp.abs(z)))
    return lb, lb - z


def _split(x):
    hi = x.astype(CD)
    return hi, (x - hi.astype(F32)).astype(CD)


def _rounded(x):
    hi, lo = _split(x)
    return hi.astype(F32) + lo.astype(F32)


def _tri(ck, rel, stack=1):
    j, s = np.arange(ck)[:, None], np.arange(ck)[None, :]
    return jnp.asarray(np.tile({"gt": j > s, "le": j <= s, "lt": j < s}[rel], (stack, 1)), CD)


def _cumsum_mm(hi, lo, tri2):
    return _dot(jnp.concatenate([hi, lo], axis=1), tri2)


def _sb_fwd(proj, *, H, qb0, kb0, vb0, bq, ck, scale, name):
    rows = proj.shape[0]
    nsub = bq // ck
    HP, HPW = HP_FWD, HP_FWD * LANES

    def body(q_ref, k_ref, v_ref, u_ref, o_ref, t_ref):
        i = pl.program_id(1)
        u = u_ref[...]
        strict = lax.broadcasted_iota(jnp.int32, (bq, bq), 1) < lax.broadcasted_iota(jnp.int32, (bq, bq), 0)

        def block(j, carry, diag):
            off = pl.multiple_of(j * bq, bq)
            out = []
            for hh in range(HP):
                c, acc = carry[hh]
                z = _dot_nt(q_ref[:, _head(hh)], k_ref[pl.ds(off, bq), _head(hh)]) * scale
                lb, lom = _log_sigmoids(z)
                if diag:
                    lom = jnp.where(strict, lom, 0.0)
                hi, lo = _split(lom)
                parts = [None] * nsub
                for s in reversed(range(nsub)):
                    ss = slice(s * ck, (s + 1) * ck)
                    rin = _cumsum_mm(hi[:, ss], lo[:, ss], u)
                    parts[s] = jnp.exp(lb[:, ss] + rin + c)
                    c = c + rin[:, :1] + _rounded(lom[:, s * ck:s * ck + 1])
                a = parts[0] if nsub == 1 else jnp.concatenate(parts, axis=1)
                if diag:
                    a = jnp.where(strict, a, 0.0)
                out.append((c, acc + _dot(a.astype(CD), v_ref[pl.ds(off, bq), _head(hh)])))
            return tuple(out)

        init = tuple((jnp.zeros((bq, 1), F32), jnp.zeros((bq, LANES), F32)) for _ in range(HP))
        carry = block(i, init, True)
        carry = lax.fori_loop(0, i, lambda jj, cr: block(i - 1 - jj, cr, False), carry)
        for hh in range(HP):
            o_ref[:, _head(hh)] = carry[hh][1].astype(o_ref.dtype)
            t_ref[hh] = carry[hh][0]

    kb = lambda off: pl.BlockSpec((rows, HPW), lambda h, i: (0, off // HP + h))
    return pl.pallas_call(
        body, grid=(H // HP, rows // bq),
        in_specs=[pl.BlockSpec((bq, HPW), lambda h, i: (i, qb0 // HP + h)), kb(kb0), kb(vb0), pl.BlockSpec((2 * ck, ck), lambda h, i: (0, 0))],
        out_specs=[pl.BlockSpec((bq, HPW), lambda h, i: (i, h)), pl.BlockSpec((HP, bq, 1), lambda h, i: (h, i, 0))],
        out_shape=[jax.ShapeDtypeStruct((rows, H * LANES), CD), jax.ShapeDtypeStruct((H, rows, 1), F32)],
        compiler_params=_cp("arbitrary", "arbitrary"), name=name,
    )(proj, proj, proj, _tri(ck, "gt", 2))


def _sb_bwd(proj, do, tot, *, H, qb0, kb0, vb0, bq, ck, scale, name):
    rows = proj.shape[0]
    nsub = bq // ck

    def body(q_ref, k_ref, v_ref, do_ref, t_ref, ule_ref, ult_ref, dq_ref, dk_ref, dv_ref):
        i = pl.program_id(1)

        @pl.when(i == 0)
        def _():
            dk_ref[...] = jnp.zeros_like(dk_ref)
            dv_ref[...] = jnp.zeros_like(dv_ref)

        ule, ult = ule_ref[...], ult_ref[...]
        strict = lax.broadcasted_iota(jnp.int32, (bq, bq), 1) < lax.broadcasted_iota(jnp.int32, (bq, bq), 0)

        def block(j, carry, diag):
            off = pl.multiple_of(j * bq, bq)
            out = []
            for hh in range(HP):
                pc, gc, dq = carry[hh]
                q_, do_, tot_ = q_ref[:, _head(hh)], do_ref[:, _head(hh)], t_ref[hh]
                kj, vj = k_ref[pl.ds(off, bq), _head(hh)], v_ref[pl.ds(off, bq), _head(hh)]
                z = _dot_nt(q_, kj) * scale
                lb, lom = _log_sigmoids(z)
                if diag:
                    lom = jnp.where(strict, lom, 0.0)
                hi, lo = _split(lom)
                parts = []
                for s in range(nsub):
                    ss = slice(s * ck, (s + 1) * ck)
                    pin = _cumsum_mm(hi[:, ss], lo[:, ss], ule)
                    parts.append(jnp.exp(lb[:, ss] + (tot_ - (pc + pin))))
                    pc = pc + pin[:, ck - 1:ck]
                a = parts[0] if nsub == 1 else jnp.concatenate(parts, axis=1)
                if diag:
                    a = jnp.where(strict, a, 0.0)
                g = a * _dot_nt(do_, vj)
                gb = g.astype(CD)
                parts = []
                for s in range(nsub):
                    ss = slice(s * ck, (s + 1) * ck)
                    gin = _dot(gb[:, ss], ult)
                    parts.append(gc + gin)
                    gc = gc + gin[:, ck - 1:ck] + g[:, (s + 1) * ck - 1:(s + 1) * ck].astype(CD).astype(F32)
                big_g = parts[0] if nsub == 1 else jnp.concatenate(parts, axis=1)
                sg = jnp.exp(lb)
                dz = (g * (1.0 - sg) - big_g * sg) * scale
                if diag:
                    dz = jnp.where(strict, dz, 0.0)
                dz = dz.astype(CD)
                dk_ref[pl.ds(off, bq), _head(hh)] += _dot_tn(dz, q_)
                dv_ref[pl.ds(off, bq), _head(hh)] += _dot_tn(a.astype(CD), do_)
                out.append((pc, gc, dq + _dot(dz, kj)))
            return tuple(out)

        init = tuple((jnp.zeros((bq, 1), F32), jnp.zeros((bq, 1), F32), jnp.zeros((bq, LANES), F32)) for _ in range(HP))
        carry = lax.fori_loop(0, i, lambda j, cr: block(j, cr, False), init)
        carry = block(i, carry, True)
        for hh in range(HP):
            dq_ref[:, _head(hh)] = carry[hh][2].astype(dq_ref.dtype)

    kb = lambda off: pl.BlockSpec((rows, HPW), lambda h, i: (0, off // HP + h))
    ob = pl.BlockSpec((bq, HPW), lambda h, i: (i, h))
    tri = pl.BlockSpec((ck, ck), lambda h, i: (0, 0))
    acc = jax.ShapeDtypeStruct((rows, H * LANES), F32)
    return pl.pallas_call(
        body, grid=(H // HP, rows // bq),
        in_specs=[pl.BlockSpec((bq, HPW), lambda h, i: (i, qb0 // HP + h)), kb(kb0), kb(vb0), ob,
                  pl.BlockSpec((HP, bq, 1), lambda h, i: (h, i, 0)), pl.BlockSpec((2 * ck, ck), lambda h, i: (0, 0)), tri],
        out_specs=[ob, kb(0), kb(0)],
        out_shape=[jax.ShapeDtypeStruct((rows, H * LANES), CD), acc, acc],
        compiler_params=_cp("arbitrary", "arbitrary"), name=name,
    )(proj, proj, proj, do, tot, _tri(ck, "le", 2), _tri(ck, "lt"))


def _dev_index(p):
    return 4 * p[0] + 2 * p[1] + p[2]


def _all_gather(arrays, *, name, after=None):
    A = len(arrays)
    extra = [] if after is None else [after]

    def body(*refs):
        ins, outs = refs[:A], refs[A + len(extra):2 * A + len(extra)]
        send_sems, recv_sems, local_sems = refs[2 * A + len(extra):]
        x, y, c = lax.axis_index("x"), lax.axis_index("y"), lax.axis_index("c")
        me, sibling = (x, y, c), (x, y, 1 - c)
        chips = [(1 - x, y), (x, 1 - y), (1 - x, 1 - y)]

        def copy(a, k, block, to, src=None):
            dst = outs[a].at[_dev_index(block)]
            return pltpu.make_async_remote_copy(
                src_ref=dst if src is None else src, dst_ref=dst, send_sem=send_sems.at[a * 7 + k],
                recv_sem=recv_sems.at[a * 7 + k], device_id=to, device_id_type=MESH)

        mine = [pltpu.make_async_copy(ins[a], outs[a].at[_dev_index(me)], local_sems.at[a]) for a in range(A)]
        for cp in mine:
            cp.start()
        first = []
        for a in range(A):
            first.append(copy(a, 0, me, sibling, src=ins[a]))
            first += [copy(a, 1 + j, me, (*chip, c), src=ins[a]) for j, chip in enumerate(chips)]
        for cp in first:
            cp.start()
        passed = []
        for j, chip in enumerate(chips):
            for a in range(A):
                copy(a, 1 + j, (*chip, c), me).wait_recv()
                fwd = copy(a, 4 + j, (*chip, c), sibling)
                fwd.start()
                passed.append(fwd)
        for a in range(A):
            copy(a, 0, sibling, me).wait_recv()
            for j, chip in enumerate(chips):
                copy(a, 4 + j, (*chip, 1 - c), me).wait_recv()
        for cp in first + passed:
            cp.wait_send()
        for cp in mine:
            cp.wait()

    any_spec = pl.BlockSpec(memory_space=pl.ANY)
    return pl.pallas_call(
        body, in_specs=[any_spec] * (A + len(extra)), out_specs=[any_spec] * A,
        out_shape=[jax.ShapeDtypeStruct((N_DEV,) + a.shape, a.dtype) for a in arrays],
        scratch_shapes=[pltpu.SemaphoreType.DMA((7 * A,)), pltpu.SemaphoreType.DMA((7 * A,)), pltpu.SemaphoreType.DMA((A,))],
        name=name,
    )(*arrays, *extra)


_HBM = pl.BlockSpec(memory_space=pltpu.HBM)
_SEM = pl.BlockSpec(memory_space=pltpu.SEMAPHORE)
_EFFECT = pltpu.SideEffectType.DATAFLOW_SIDE_EFFECTING
_PEER_ORDER = (0, 1, 3, 2, 4, 5, 6)


def _peers():
    x, y, c = lax.axis_index("x"), lax.axis_index("y"), lax.axis_index("c")
    return _dev_index((x, y, c)), [((1 - x) if r & 4 else x, (1 - y) if r & 2 else y, (1 - c) if r & 1 else c) for r in range(1, N_DEV)]


def _xchg_copy(src, land, sends, recvs, a, k, me_i, peers, scatter, arriving):
    peer_i = _dev_index(peers[k])
    return pltpu.make_async_remote_copy(
        src_ref=src.at[peer_i] if scatter else src, dst_ref=land.at[peer_i if arriving else me_i],
        send_sem=sends.at[a * 7 + k], recv_sem=recvs.at[a * 7 + k], device_id=peers[k], device_id_type=MESH)


def _xchg_start(arrays, *, scatter, name, after=None):
    A = len(arrays)
    lands = [lax.empty(a.shape if scatter else (N_DEV,) + a.shape, a.dtype) for a in arrays]
    extra = [] if after is None else [after]

    def body(*refs):
        srcs, land = refs[:A], refs[A:2 * A]
        sends, recvs, token = refs[2 * A + len(extra)], refs[2 * A + len(extra) + 1], refs[-1]
        me_i, peers = _peers()
        for k in _PEER_ORDER:
            for a in range(A):
                _xchg_copy(srcs[a], land[a], sends, recvs, a, k, me_i, peers, scatter, False).start()
        token[...] = jnp.zeros_like(token)

    hbm = lambda a: pltpu.HBM(a.shape, a.dtype)
    outs = pl.pallas_call(
        body, name=name, in_specs=[_HBM] * (2 * A) + [pl.BlockSpec(memory_space=pl.ANY)] * len(extra),
        out_shape=(pltpu.SemaphoreType.DMA((7 * A,)), pltpu.SemaphoreType.DMA((7 * A,)), *[hbm(a) for a in arrays],
                   *[hbm(a) for a in lands], jax.ShapeDtypeStruct((8, LANES), F32)),
        out_specs=(_SEM, _SEM, *([_HBM] * (2 * A)), pl.BlockSpec(memory_space=pltpu.VMEM)),
        input_output_aliases={n: 2 + n for n in range(2 * A)},
        compiler_params=pltpu.CompilerParams(has_side_effects=_EFFECT),
    )(*[pltpu.with_memory_space_constraint(a, pltpu.HBM) for a in arrays],
      *[pltpu.with_memory_space_constraint(a, pltpu.HBM) for a in lands], *extra)
    return outs[0], outs[1], list(outs[2:2 + A]), list(outs[2 + A:2 + 2 * A]), outs[-1]


def _xchg_wait(handle, after, *, scatter, name):
    sends, recvs, srcs, lands, _ = handle
    A = len(srcs)

    def body(*refs):
        src, land = refs[:A], refs[A:2 * A]
        send_sems, recv_sems = refs[2 * A], refs[2 * A + 1]
        me_i, peers = _peers()
        for k in _PEER_ORDER:
            for a in range(A):
                cp = _xchg_copy(src[a], land[a], send_sems, recv_sems, a, k, me_i, peers, scatter, True)
                cp.wait_send()
                cp.wait_recv()

    outs = pl.pallas_call(
        body, name=name, in_specs=[_HBM] * (2 * A) + [_SEM, _SEM, pl.BlockSpec(memory_space=pl.ANY)],
        out_shape=tuple(pltpu.HBM(a.shape, a.dtype) for a in srcs + lands), out_specs=tuple([_HBM] * (2 * A)),
        input_output_aliases={n: n for n in range(2 * A)},
        compiler_params=pltpu.CompilerParams(has_side_effects=_EFFECT),
    )(*srcs, *lands, sends, recvs, after)
    my = _dev_index((lax.axis_index("x"), lax.axis_index("y"), lax.axis_index("c")))
    got = []
    for src, land in zip(outs[:A], outs[A:]):
        own = lax.dynamic_index_in_dim(src, my, 0, keepdims=True) if scatter else src[None]
        got.append(lax.dynamic_update_index_in_dim(land, own, my, 0))
    return got


def _adamw(g, w, m, v):
    m = ADAM_B1 * m + (1.0 - ADAM_B1) * g
    v = ADAM_B2 * v + (1.0 - ADAM_B2) * (g * g)
    delta = -ADAM_LR * ((m / ADAM_BC1) / (jnp.sqrt(v / ADAM_BC2) + ADAM_EPS) + ADAM_WD * w)
    return delta, m, v


def _adam_pieces(pieces, w, m, v, bufs, *, layer, name):
    depth, R, C = w.shape
    if R % 16 == 0:
        tr, tc = _pick(R, [t for t in (512, 256, 128, 64, 32, 16) if t * C * 4 <= 2**20]), C
    else:
        tr, tc = R, _pick(C, [t for t in (512, 256, 128) if R * t * 4 <= 2**21])

    def body(p_ref, w_ref, m_ref, v_ref, b0, b1, b2, b3, g_ref, d_ref, mo_ref, vo_ref):
        g = p_ref[0].astype(F32)
        for k in range(1, N_DEV):
            g = g + p_ref[k].astype(F32)
        g_ref[0] = g
        d_ref[0], mo_ref[0], vo_ref[0] = _adamw(g, w_ref[0], m_ref[0], v_ref[0])

    blk = pl.BlockSpec((1, tr, tc), lambda i, j: (layer, i, j))
    anyspec = pl.BlockSpec(memory_space=pl.ANY)
    return pl.pallas_call(
        body, grid=(R // tr, C // tc),
        in_specs=[pl.BlockSpec((N_DEV, tr, tc), lambda i, j: (0, i, j)), blk, blk, blk] + [anyspec] * 4,
        out_specs=[blk] * 4, out_shape=[jax.ShapeDtypeStruct((depth, R, C), F32)] * 4,
        input_output_aliases={4: 0, 5: 1, 6: 2, 7: 3}, compiler_params=_cp("parallel", "parallel"), name=name,
    )(pieces, w, m, v, *bufs)


def _adam_direct(g, w, m, v, *, name):
    sh = jax.ShapeDtypeStruct(w.shape, F32)

    def body(g_ref, w_ref, m_ref, v_ref, d_ref, mo_ref, vo_ref):
        d_ref[...], mo_ref[...], vo_ref[...] = _adamw(g_ref[...], w_ref[...], m_ref[...], v_ref[...])

    return pl.pallas_call(body, out_shape=[sh] * 3, compiler_params=_cp(), name=name)(g, w, m, v)


def _sum_devices(parts, *, name):
    def body(p_ref, o_ref):
        s = p_ref[0]
        for k in range(1, N_DEV):
            s = s + p_ref[k]
        o_ref[...] = s

    return pl.pallas_call(body, out_shape=jax.ShapeDtypeStruct(parts.shape[1:], F32), compiler_params=_cp(), name=name)(parts)


def _rope_cols_pad(w):
    z = jnp.zeros(w.shape[:-1] + (ROPE // 2,), w.dtype)
    return jnp.concatenate([w[..., :ROPE // 2], z, w[..., ROPE // 2:], z], axis=-1)


def _rope_cols_unpad(w):
    return jnp.concatenate([w[..., :ROPE // 2], w[..., ROPE:ROPE + ROPE // 2]], axis=-1)


class _Dims:
    def __init__(self, D, S, NM, QL, KVL, WM, WS):
        self.D, self.S, self.NM, self.QL, self.KVL, self.WM, self.WS = D, S, NM, QL, KVL, WM, WS
        self.H = WM // LANES
        self.HW = self.H * LANES
        self.L = NM + S
        self.bq = 384 if self.L > 1024 else 128
        self.Lp = -(-self.L // self.bq) * self.bq
        self.ck_sb = LANES
        self.tr = 128
        self.sizes = (QL, KVL, ROPE, WM, WS, WS, WS, WS)
        self.d_in = sum(self.sizes)
        front = QL + KVL + LANES
        self.front = -(-front // WM) * WM
        self.pad = self.front - front
        self.dn = self.front + WM + 4 * WS
        assert WM == WS and QL % KVL == 0 and KVL % LANES == 0 and self.d_in % N_DEV == 0 and D % N_DEV == 0
        self.mla_scale = 1.0 / math.sqrt(LANES + ROPE)
        self.sb_scale = 1.0 / math.sqrt(LANES)


def _w_in_nice(d, wt):
    parts = jnp.split(wt, np.cumsum(d.sizes)[:-1].tolist(), axis=0)
    z = jnp.zeros((ROPE // 2, wt.shape[1]), wt.dtype)
    k_r = [parts[2][:ROPE // 2], z, parts[2][ROPE // 2:], z]
    return jnp.concatenate([parts[0], parts[1]] + k_r + [jnp.zeros((d.pad, wt.shape[1]), wt.dtype)] + parts[3:], axis=0)


def _w_in_orig(d, wt):
    a, b = d.QL + d.KVL, d.front
    return jnp.concatenate([wt[:a + ROPE // 2], wt[a + ROPE:a + ROPE + ROPE // 2], wt[b:]], axis=0)


def _w_uq_nice(d, w):
    w3 = w.reshape(d.QL, d.H, LANES + ROPE)
    return jnp.concatenate([w3[..., :LANES], _rope_cols_pad(w3[..., LANES:])], axis=-1).reshape(d.QL, d.H * QKW)


def _w_uq_orig(d, w):
    w3 = w.reshape(d.QL, d.H, QKW)
    return jnp.concatenate([w3[..., :LANES], _rope_cols_unpad(w3[..., LANES:])], axis=-1).reshape(d.QL, d.H * (LANES + ROPE))


def _w_ukv_nice(d, w):
    return w.reshape(d.KVL, d.H, 2, LANES).transpose(0, 2, 1, 3).reshape(d.KVL, 2 * d.HW)


def _w_ukv_orig(d, w):
    return w.reshape(d.KVL, 2, d.H, LANES).transpose(0, 2, 1, 3).reshape(d.KVL, 2 * d.HW)


def _cols_from_shards(g):
    return g.transpose(1, 0, 2).reshape(g.shape[1], -1)


def _cols_to_shards(w):
    return w.reshape(w.shape[0], N_DEV, -1).transpose(1, 0, 2)


def _rope_tables(rows):
    inv_freq = ROPE_THETA ** (-jnp.arange(0, ROPE, 2, dtype=F32) / ROPE)
    ang = jnp.arange(rows, dtype=jnp.int32).astype(F32)[:, None] * inv_freq[None, :]
    cos, sin, z = jnp.cos(ang), jnp.sin(ang), jnp.zeros_like(ang)
    return jnp.concatenate([cos, z, cos, z], axis=1), jnp.concatenate([-sin, z, sin, z], axis=1)


def _flat_pack(vecs):
    flat = jnp.concatenate([v.reshape(-1).astype(F32) for v in vecs])
    n = flat.shape[0]
    rows = -(-n // (8 * 1024)) * 8
    offs = np.cumsum([0] + [int(np.prod(v.shape)) for v in vecs])
    return jnp.pad(flat, (0, rows * 1024 - n)).reshape(rows, 1024), offs


def kernel(x, meta_tokens, g_norm, w_in, g_q, g_kv, w_uq, w_ukv, g_out_mla, g_out_sb, w_o, g_final, loss_target, m_meta_tokens, m_g_norm, m_w_in, m_g_q, m_g_kv, m_w_uq, m_w_ukv, m_g_out_mla, m_g_out_sb, m_w_o, m_g_final, v_meta_tokens, v_g_norm, v_w_in, v_g_q, v_g_kv, v_w_uq, v_w_ukv, v_g_out_mla, v_g_out_sb, v_w_o, v_g_final):
    depth = g_norm.shape[0]
    d = _Dims(D=x.shape[-1], S=x.shape[1], NM=meta_tokens.shape[0], QL=g_q.shape[1], KVL=g_kv.shape[1],
              WM=g_out_mla.shape[1], WS=g_out_sb.shape[1])
    D, Lp, H, tr = d.D, d.Lp, d.H, d.tr
    my = _dev_index((lax.axis_index("x"), lax.axis_index("y"), lax.axis_index("c")))

    w_in_t, m_w_in_t, v_w_in_t = (jnp.swapaxes(a, 1, 2) for a in (w_in, m_w_in, v_w_in))
    shards = lambda l: [w_in_t[l].astype(CD), w_uq[l].astype(CD), w_ukv[l].astype(CD), w_o[l].astype(CD)]
    g_in0, g_meta = _all_gather([shards(0)[0], meta_tokens], name="gather_layer0")
    gathers, token = {}, g_meta
    for l in range(depth):
        gathers[l] = _xchg_start(shards(l)[1:] if l == 0 else shards(l), scatter=False, name=f"gather_start_{l}", after=token)
        token = gathers[l][4]
    started = token[0, 0]

    def relayout(g_uq, g_ukv, g_o):
        return _w_uq_nice(d, _cols_from_shards(g_uq)), _w_ukv_nice(d, _cols_from_shards(g_ukv)), g_o.reshape(D, D)

    wn, wuq, wukv, wo = [None] * depth, [None] * depth, [None] * depth, [None] * depth
    meta_full = _cols_from_shards(g_meta)

    cs, sn = _rope_tables(Lp)
    h = jnp.concatenate([meta_full, x[0], jnp.zeros((Lp - d.L, D), F32)], axis=0)
    tgt = jnp.concatenate([jnp.zeros((d.NM, D), F32), loss_target[0], jnp.zeros((Lp - d.L, D), F32)], axis=0)

    b128 = lambda col: col // LANES
    krb = b128(d.QL + d.KVL)
    zb_mla, zb_sb = d.front // d.WM, (d.front + d.WM + 3 * d.WS) // d.WS
    qb0, kb0, vb0 = b128(d.front + d.WM), b128(d.front + d.WM + d.WS), b128(d.front + d.WM + 2 * d.WS)
    row = lambda a, i: a[i][None, :]

    saved = []
    for i in range(depth):
        if i == 0:
            wn[i] = _w_in_nice(d, g_in0.reshape(d.d_in, D))
            u = _rms_fwd(h, row(g_norm, i) + started, cb=0, width=D, tr=tr, name=f"rms_in_{i}")
            proj = _mm(u, wn[i], tb=True, out_dtype=CD, name=f"proj_{i}")
            wuq[i], wukv[i], wo[i] = relayout(*_xchg_wait(gathers[i], proj, scatter=False, name=f"gather_wait_{i}"))
        else:
            g_in, *g_rest = _xchg_wait(gathers[i], h, scatter=False, name=f"gather_wait_{i}")
            wn[i] = _w_in_nice(d, g_in.reshape(d.d_in, D))
            wuq[i], wukv[i], wo[i] = relayout(*g_rest)
            u = _rms_fwd(h, row(g_norm, i), cb=0, width=D, tr=tr, name=f"rms_in_{i}")
            proj = _mm(u, wn[i], tb=True, out_dtype=CD, name=f"proj_{i}")
        cqn = _rms_fwd(proj, row(g_q, i), cb=0, width=d.QL, tr=tr, name=f"rms_q_{i}")
        ckvn = _rms_fwd(proj, row(g_kv, i), cb=d.QL // d.KVL, width=d.KVL, tr=tr, name=f"rms_kv_{i}")
        q_raw = _mm(cqn, wuq[i], out_dtype=F32, name=f"up_q_{i}")
        kvn = _mm(ckvn, wukv[i], out_dtype=CD, name=f"up_kv_{i}")
        q, kcat = _rope_fwd(q_raw, kvn, proj, cs, sn, H=H, krb=krb, tr=tr, name=f"rope_{i}")
        o_mla, lse = _mla_fwd(q, kcat, kvn, H=H, bq=d.bq, scale=d.mla_scale, name=f"mla_fwd_{i}")
        o_sb, tot = _sb_fwd(proj, H=H, qb0=qb0, kb0=kb0, vb0=vb0, bq=d.bq, ck=d.ck_sb, scale=d.sb_scale, name=f"sb_fwd_{i}")
        y = _gate_fwd(o_mla, o_sb, proj, row(g_out_mla, i), row(g_out_sb, i), zb_mla=zb_mla, zb_sb=zb_sb, tr=tr, name=f"gate_fwd_{i}")
        h_next = _mm(y, wo[i], out_dtype=F32, res=h, name=f"out_proj_{i}")
        saved.append((h, u, proj, cqn, ckvn, q, kvn, kcat, o_mla, lse, o_sb, tot, y))
        h = h_next

    dh, dg_final, loss_part = _loss_head(h, g_final[None, :], tgt, lo=d.NM, hi=d.L, tr=tr, name="loss_head")

    dg_norm, dg_q, dg_kv, dg_om, dg_os = [None] * depth, [None] * depth, [None] * depth, [None] * depth, [None] * depth
    big_w = {"w_in": (w_in_t, m_w_in_t, v_w_in_t), "w_uq": (w_uq, m_w_uq, v_w_uq), "w_ukv": (w_ukv, m_w_ukv, v_w_ukv), "w_o": (w_o, m_w_o, v_w_o)}
    res = {n: [lax.empty(w.shape, F32) for _ in range(4)] for n, (w, _, _) in big_w.items()}
    exchanges = [[] for _ in range(depth)]

    def send(l, tag, pieces, after=None):
        handle = _xchg_start(list(pieces.values()), scatter=True, name=f"exch_start_{l}{tag}", after=after)
        exchanges[l].append((handle, list(pieces), tag))
        return handle[4]

    def finish(l, after):
        got = None
        for handle, sent, tag in exchanges[l]:
            got = _xchg_wait(handle, after, scatter=True, name=f"exch_wait_{l}{tag}")
            for n, p in zip(sent, got):
                w, m, v = big_w[n]
                res[n] = _adam_pieces(p, w, m, v, res[n], layer=l, name=f"adam_{n}_{l}")
        return got[0]

    cat = lambda parts: jnp.concatenate(parts, axis=0)
    zpad = jnp.zeros((Lp, d.pad), CD)
    for i in reversed(range(depth)):
        h_in, u, proj, cqn, ckvn, q, kvn, kcat, o_mla, lse, o_sb, tot, y = saved[i]
        dy = _mm(dh, wo[i], tb=True, out_dtype=CD, name=f"d_y_{i}")
        dwo = _mm(y, dh, ta=True, out_dtype=CD, name=f"d_wo_{i}")
        do_mla, do_sb, dz_mla, dz_sb, dg_om[i], dg_os[i] = _gate_bwd(
            dy, o_mla, o_sb, proj, row(g_out_mla, i), row(g_out_sb, i), zb_mla=zb_mla, zb_sb=zb_sb, tr=tr, name=f"gate_bwd_{i}")
        dq_mla, dkcat, dv = _mla_bwd(q, kcat, kvn, o_mla, do_mla, lse, H=H, bq=d.bq, scale=d.mla_scale, name=f"mla_bwd_{i}")
        dq_sb, dk_sb, dv_sb = _sb_bwd(proj, do_sb, tot, H=H, qb0=qb0, kb0=kb0, vb0=vb0, bq=d.bq, ck=d.ck_sb, scale=d.sb_scale, name=f"sb_bwd_{i}")
        dq, dkv, dk_r = _rope_bwd(dq_mla, dkcat, dv, cs, sn, H=H, tr=tr, name=f"rope_bwd_{i}")
        dcqn = _mm(dq, wuq[i], tb=True, out_dtype=CD, name=f"d_cq_{i}")
        dwuq = _mm(cqn, dq, ta=True, out_dtype=CD, name=f"d_wuq_{i}")
        dckvn = _mm(dkv, wukv[i], tb=True, out_dtype=CD, name=f"d_ckv_{i}")
        dwukv = _mm(ckvn, dkv, ta=True, out_dtype=CD, name=f"d_wukv_{i}")
        dc_q, dg_q[i] = _rms_bwd(proj, row(g_q, i), dcqn, cb=0, width=d.QL, tr=tr, out_dtype=CD, name=f"rms_q_bwd_{i}")
        dc_kv, dg_kv[i] = _rms_bwd(proj, row(g_kv, i), dckvn, cb=d.QL // d.KVL, width=d.KVL, tr=tr, out_dtype=CD, name=f"rms_kv_bwd_{i}")
        rest = {"w_uq": _cols_to_shards(_w_uq_orig(d, dwuq)), "w_ukv": _cols_to_shards(_w_ukv_orig(d, dwukv)),
                "w_o": dwo.reshape(N_DEV, D // N_DEV, D)}
        placed = None
        if i == 0:
            small_a, offs_a = _flat_pack(([cat(dg_norm[1:])] if depth > 1 else []) + [cat(dg_q), cat(dg_kv), cat(dg_om), cat(dg_os), dg_final, loss_part])
            (small_a_all,) = _all_gather([small_a], name="gather_small_a")
            placed = send(0, "a", rest, after=small_a_all)
        dproj = jnp.concatenate([dc_q, dc_kv, dk_r, zpad, dz_mla, dq_sb, dk_sb.astype(CD), dv_sb.astype(CD), dz_sb], axis=1)
        dwn = _mm(dproj, u, ta=True, out_dtype=CD, after=placed, name=f"d_wn_{i}")
        pieces = {"w_in": _w_in_orig(d, dwn).reshape(N_DEV, d.d_in // N_DEV, D)}
        if i > 0:
            pieces.update(rest)
        du = _mm(dproj, wn[i], out_dtype=CD, after=send(i, "b", pieces), name=f"d_u_{i}")
        dh, dg_norm[i] = _rms_bwd(h_in, row(g_norm, i), du, cb=0, width=D, tr=tr, res=dh, out_dtype=F32, name=f"rms_in_bwd_{i}")
        if i + 1 < depth:
            finish(i + 1, dh)
    landed = finish(0, dh)

    small_b, offs_b = _flat_pack([dg_norm[0], dh[:d.NM]])
    (small_b_all,) = _all_gather([small_b], name="gather_small_b", after=landed)
    sum_a = _sum_devices(small_a_all, name="sum_small_a").reshape(-1)
    sum_b = _sum_devices(small_b_all, name="sum_small_b").reshape(-1)
    n_a = int(offs_a[-2])
    small_sum = jnp.concatenate([sum_b[:D], sum_a[:n_a], sum_b[D:offs_b[2]], sum_a[n_a:n_a + 1]])
    offs = np.cumsum([0, depth * D, depth * d.QL, depth * d.KVL, depth * d.WM, depth * d.WS, D, d.NM * D, 1])
    seg = lambda k, shape: small_sum[offs[k]:offs[k + 1]].reshape(shape)
    loss = small_sum[offs[7]]
    grad_x = dh[d.NM:d.L][None]

    g_meta_full = seg(6, (d.NM, D))
    ncol = meta_tokens.shape[1]
    g_meta_mine = lax.dynamic_slice(g_meta_full, (0, my * ncol), (d.NM, ncol))
    res["meta_tokens"] = [g_meta_mine] + list(_adam_direct(g_meta_mine, meta_tokens, m_meta_tokens, v_meta_tokens, name="adam_meta"))

    small_names = ["g_norm", "g_q", "g_kv", "g_out_mla", "g_out_sb", "g_final"]
    small_w = [g_norm, g_q, g_kv, g_out_mla, g_out_sb, g_final]
    small_m = [m_g_norm, m_g_q, m_g_kv, m_g_out_mla, m_g_out_sb, m_g_final]
    small_v = [v_g_norm, v_g_q, v_g_kv, v_g_out_mla, v_g_out_sb, v_g_final]
    g_flat, _ = _flat_pack([small_sum[:offs[6]]])
    wf, _ = _flat_pack(small_w)
    mf, _ = _flat_pack(small_m)
    vf, _ = _flat_pack(small_v)
    d_f, m_f, v_f = _adam_direct(g_flat, wf, mf, vf, name="adam_small")
    for k, nm in enumerate(small_names):
        sl = lambda f: f.reshape(-1)[offs[k]:offs[k + 1]].reshape(small_w[k].shape)
        res[nm] = [sl(g_flat), sl(d_f), sl(m_f), sl(v_f)]

    names = ["meta_tokens", "g_norm", "w_in", "g_q", "g_kv", "w_uq", "w_ukv", "g_out_mla", "g_out_sb", "w_o", "g_final"]
    out = lambda n, j: jnp.swapaxes(res[n][j], 1, 2) if n == "w_in" else res[n][j]
    return (loss, grad_x, *[out(n, 0) for n in names], *[out(n, 1) for n in names],
            *[out(n, 2) for n in names], *[out(n, 3) for n in names])
```

```python
import math

import numpy as np
import jax
import jax.numpy as jnp
from jax import lax
from jax.experimental import pallas as pl
from jax.experimental.pallas import tpu as pltpu

CD = jnp.bfloat16
F32 = jnp.float32

LANES = 128
N_DEV = 8
EPS = 1e-6
ROPE = 64
ROPE_THETA = 10000.0
NEG = -1e30
VMEM_LIMIT = 48 * 2**20
VMEM_LIMIT_WIDE = 56 * 2**20

ADAM_LR, ADAM_B1, ADAM_B2, ADAM_EPS, ADAM_WD, ADAM_STEP = 0.001, 0.9, 0.999, 1e-08, 0.01, 10
ADAM_BC1 = 1.0 - ADAM_B1**ADAM_STEP
ADAM_BC2 = 1.0 - ADAM_B2**ADAM_STEP

MESH = pl.DeviceIdType.MESH


def _cp(*sem):
    return pltpu.CompilerParams(dimension_semantics=tuple(sem) if sem else None, vmem_limit_bytes=VMEM_LIMIT)


def _pick(dim, prefs):
    for p in prefs:
        if dim % p == 0:
            return p
    raise ValueError(f"no tile for {dim} in {prefs}")


def _dot(a, b):
    return jnp.dot(a, b, preferred_element_type=F32)


def _dot_nt(a, b):
    return lax.dot_general(a, b, (((1,), (1,)), ((), ())), preferred_element_type=F32)


def _dot_tn(a, b):
    return lax.dot_general(a, b, (((0,), (0,)), ((), ())), preferred_element_type=F32)


def _mm(a, b, *, ta=False, tb=False, out_dtype, res=None, after=None, name):
    M, K = (a.shape[1], a.shape[0]) if ta else a.shape
    N = b.shape[0] if tb else b.shape[1]
    tm = _pick(M, (1056, 1024, 512, 256, 128))
    tk = _pick(K, (4224, 2112, 1056, 1024, 512, 384, 256, 128) if ta else (4096, 2048, 1024, 512, 384, 256, 128)[a.dtype == F32:])
    tn = _pick(N, (512, 384, 256, 128) if tk > 2112 else (1024, 512, 384, 256, 128))
    nk = K // tk
    a_dims = (((0,), (1 if tb else 0,)), ((), ())) if ta else (((1,), (1 if tb else 0,)), ((), ()))

    def body(*refs):
        if after is not None:
            refs = refs[:-3] + refs[-2:]
        if res is None:
            a_ref, b_ref, o_ref, acc = refs
        else:
            a_ref, b_ref, r_ref, o_ref, acc = refs
        k = pl.program_id(2)

        @pl.when(k == 0)
        def _():
            acc[...] = jnp.zeros_like(acc)

        acc[...] += lax.dot_general(a_ref[...].astype(CD), b_ref[...].astype(CD), a_dims, preferred_element_type=F32)

        @pl.when(k == nk - 1)
        def _():
            r = acc[...]
            if res is not None:
                r = r + r_ref[...]
            o_ref[...] = r.astype(out_dtype)

    a_spec = pl.BlockSpec((tk, tm), lambda i, j, k: (k, i)) if ta else pl.BlockSpec((tm, tk), lambda i, j, k: (i, k))
    b_spec = pl.BlockSpec((tn, tk), lambda i, j, k: (j, k)) if tb else pl.BlockSpec((tk, tn), lambda i, j, k: (k, j))
    o_spec = pl.BlockSpec((tm, tn), lambda i, j, k: (i, j))
    in_specs, args = [a_spec, b_spec], [a, b]
    if res is not None:
        in_specs.append(o_spec)
        args.append(res)
    if after is not None:
        in_specs.append(pl.BlockSpec(memory_space=pl.ANY))
        args.append(after)
    return pl.pallas_call(
        body, grid=(M // tm, N // tn, nk), in_specs=in_specs, out_specs=o_spec,
        out_shape=jax.ShapeDtypeStruct((M, N), out_dtype), scratch_shapes=[pltpu.VMEM((tm, tn), F32)],
        compiler_params=_cp("parallel", "parallel", "arbitrary"), name=name,
    )(*args)


def _rms_fwd(x, g, *, cb, width, tr, name):
    rows = x.shape[0]

    def body(x_ref, g_ref, o_ref):
        xv = x_ref[...].astype(F32)
        r = lax.rsqrt(jnp.mean(xv * xv, axis=-1, keepdims=True) + EPS)
        o_ref[...] = (xv * r * g_ref[...]).astype(o_ref.dtype)

    return pl.pallas_call(
        body, grid=(rows // tr,),
        in_specs=[pl.BlockSpec((tr, width), lambda i: (i, cb)), pl.BlockSpec((1, width), lambda i: (0, 0))],
        out_specs=pl.BlockSpec((tr, width), lambda i: (i, 0)),
        out_shape=jax.ShapeDtypeStruct((rows, width), CD), compiler_params=_cp("parallel"), name=name,
    )(x, g)


def _rms_bwd(x, g, dy, *, cb, width, tr, res=None, out_dtype, name):
    rows = x.shape[0]

    def body(*refs):
        if res is None:
            x_ref, g_ref, dy_ref, dx_ref, dg_ref = refs
        else:
            x_ref, g_ref, dy_ref, r_ref, dx_ref, dg_ref = refs

        @pl.when(pl.program_id(0) == 0)
        def _():
            dg_ref[...] = jnp.zeros_like(dg_ref)

        xv = x_ref[...].astype(F32)
        r = lax.rsqrt(jnp.mean(xv * xv, axis=-1, keepdims=True) + EPS)
        xh = xv * r
        dyv = dy_ref[...].astype(F32)
        dxh = dyv * g_ref[...]
        dx = r * (dxh - xh * jnp.mean(dxh * xh, axis=-1, keepdims=True))
        if res is not None:
            dx = dx + r_ref[...]
        dx_ref[...] = dx.astype(out_dtype)
        dg_ref[...] += jnp.sum(dyv * xh, axis=0, keepdims=True)

    blk = pl.BlockSpec((tr, width), lambda i: (i, 0))
    in_specs = [pl.BlockSpec((tr, width), lambda i: (i, cb)), pl.BlockSpec((1, width), lambda i: (0, 0)), blk]
    args = [x, g, dy]
    if res is not None:
        in_specs.append(blk)
        args.append(res)
    return pl.pallas_call(
        body, grid=(rows // tr,), in_specs=in_specs,
        out_specs=[blk, pl.BlockSpec((1, width), lambda i: (0, 0))],
        out_shape=[jax.ShapeDtypeStruct((rows, width), out_dtype), jax.ShapeDtypeStruct((1, width), F32)],
        compiler_params=_cp("arbitrary"), name=name,
    )(*args)


def _sigmoid(z):
    return 1.0 / (1.0 + jnp.exp(-z))


def _gate_fwd(o_mla, o_sb, proj, g_mla, g_sb, *, zb_mla, zb_sb, tr, name):
    rows, W = o_mla.shape

    def body(om_ref, os_ref, zm_ref, zs_ref, gm_ref, gs_ref, y_ref):
        for half, (o_ref, z_ref, g_ref) in enumerate(((om_ref, zm_ref, gm_ref), (os_ref, zs_ref, gs_ref))):
            o = o_ref[...].astype(F32)
            r = lax.rsqrt(jnp.mean(o * o, axis=-1, keepdims=True) + EPS)
            z = z_ref[...].astype(F32)
            y_ref[:, half * W:(half + 1) * W] = ((o * r * g_ref[...]) * (z * _sigmoid(z))).astype(y_ref.dtype)

    blk = pl.BlockSpec((tr, W), lambda i: (i, 0))
    gsp = pl.BlockSpec((1, W), lambda i: (0, 0))
    return pl.pallas_call(
        body, grid=(rows // tr,),
        in_specs=[blk, blk, pl.BlockSpec((tr, W), lambda i: (i, zb_mla)), pl.BlockSpec((tr, W), lambda i: (i, zb_sb)), gsp, gsp],
        out_specs=pl.BlockSpec((tr, 2 * W), lambda i: (i, 0)),
        out_shape=jax.ShapeDtypeStruct((rows, 2 * W), CD), compiler_params=_cp("parallel"), name=name,
    )(o_mla, o_sb, proj, proj, g_mla, g_sb)


def _gate_bwd(dy, o_mla, o_sb, proj, g_mla, g_sb, *, zb_mla, zb_sb, tr, name):
    rows, W = o_mla.shape

    def body(dy_ref, om_ref, os_ref, zm_ref, zs_ref, gm_ref, gs_ref, dom_ref, dos_ref, dzm_ref, dzs_ref, dgm_ref, dgs_ref):
        @pl.when(pl.program_id(0) == 0)
        def _():
            dgm_ref[...] = jnp.zeros_like(dgm_ref)
            dgs_ref[...] = jnp.zeros_like(dgs_ref)

        halves = ((om_ref, zm_ref, gm_ref, dom_ref, dzm_ref, dgm_ref), (os_ref, zs_ref, gs_ref, dos_ref, dzs_ref, dgs_ref))
        for half, (o_ref, z_ref, g_ref, do_ref, dz_ref, dg_ref) in enumerate(halves):
            dyv = dy_ref[:, half * W:(half + 1) * W].astype(F32)
            o = o_ref[...].astype(F32)
            r = lax.rsqrt(jnp.mean(o * o, axis=-1, keepdims=True) + EPS)
            oh = o * r
            g = g_ref[...]
            z = z_ref[...].astype(F32)
            sg = _sigmoid(z)
            dn = dyv * (z * sg)
            dz_ref[...] = (dyv * (oh * g) * (sg * (1.0 + z * (1.0 - sg)))).astype(dz_ref.dtype)
            dxh = dn * g
            do_ref[...] = (r * (dxh - oh * jnp.mean(dxh * oh, axis=-1, keepdims=True))).astype(do_ref.dtype)
            dg_ref[...] += jnp.sum(dn * oh, axis=0, keepdims=True)

    blk = pl.BlockSpec((tr, W), lambda i: (i, 0))
    gsp = pl.BlockSpec((1, W), lambda i: (0, 0))
    act = jax.ShapeDtypeStruct((rows, W), CD)
    gsh = jax.ShapeDtypeStruct((1, W), F32)
    return pl.pallas_call(
        body, grid=(rows // tr,),
        in_specs=[pl.BlockSpec((tr, 2 * W), lambda i: (i, 0)), blk, blk,
                  pl.BlockSpec((tr, W), lambda i: (i, zb_mla)), pl.BlockSpec((tr, W), lambda i: (i, zb_sb)), gsp, gsp],
        out_specs=[blk, blk, blk, blk, gsp, gsp], out_shape=[act, act, act, act, gsh, gsh],
        compiler_params=_cp("arbitrary"), name=name,
    )(dy, o_mla, o_sb, proj, proj, g_mla, g_sb)


QKW = 2 * LANES


def _rope_fwd(q_raw, kv, proj, cs, sn, *, H, krb, tr, name):
    rows = q_raw.shape[0]

    def body(q_ref, kn_ref, kr_ref, c_ref, s_ref, qo_ref, ko_ref):
        c, s = c_ref[...], s_ref[...]
        xk = kr_ref[...].astype(F32)
        kr = (xk * c + pltpu.roll(xk, ROPE, 1) * s).astype(ko_ref.dtype)
        for h in range(H):
            nope, rope = slice(h * QKW, h * QKW + LANES), slice(h * QKW + LANES, (h + 1) * QKW)
            qo_ref[:, nope] = q_ref[:, nope].astype(qo_ref.dtype)
            xh = q_ref[:, rope]
            qo_ref[:, rope] = (xh * c + pltpu.roll(xh, ROPE, 1) * s).astype(qo_ref.dtype)
            ko_ref[:, nope] = kn_ref[:, _head(h)]
            ko_ref[:, rope] = kr

    tab = pl.BlockSpec((tr, LANES), lambda i: (i, 0))
    wide = pl.BlockSpec((tr, H * QKW), lambda i: (i, 0))
    sh = jax.ShapeDtypeStruct((rows, H * QKW), CD)
    return pl.pallas_call(
        body, grid=(rows // tr,),
        in_specs=[wide, pl.BlockSpec((tr, H * LANES), lambda i: (i, 0)), pl.BlockSpec((tr, LANES), lambda i: (i, krb)), tab, tab],
        out_specs=[wide, wide], out_shape=[sh, sh], compiler_params=_cp("parallel"), name=name,
    )(q_raw, kv, proj, cs, sn)


def _rope_bwd(dq, dkcat, dv, cs, sn, *, H, tr, name):
    rows = dq.shape[0]
    HW = H * LANES

    def body(dq_ref, dk_ref, dv_ref, c_ref, s_ref, dqo_ref, dkv_ref, dkr_ref):
        c, s = c_ref[...], s_ref[...]
        dkr = jnp.zeros((tr, LANES), F32)
        for h in range(H):
            nope, rope = slice(h * QKW, h * QKW + LANES), slice(h * QKW + LANES, (h + 1) * QKW)
            dqo_ref[:, nope] = dq_ref[:, nope]
            d = dq_ref[:, rope].astype(F32)
            dqo_ref[:, rope] = (d * c + pltpu.roll(d * s, ROPE, 1)).astype(dqo_ref.dtype)
            dkv_ref[:, _head(h)] = dk_ref[:, nope].astype(dkv_ref.dtype)
            dkr = dkr + dk_ref[:, rope]
        dkv_ref[:, HW:] = dv_ref[...].astype(dkv_ref.dtype)
        dkr_ref[...] = (dkr * c + pltpu.roll(dkr * s, ROPE, 1)).astype(dkr_ref.dtype)

    tab = pl.BlockSpec((tr, LANES), lambda i: (i, 0))
    wide = pl.BlockSpec((tr, H * QKW), lambda i: (i, 0))
    return pl.pallas_call(
        body, grid=(rows // tr,), in_specs=[wide, wide, pl.BlockSpec((tr, HW), lambda i: (i, 0)), tab, tab],
        out_specs=[wide, wide, tab],
        out_shape=[jax.ShapeDtypeStruct((rows, H * QKW), CD), jax.ShapeDtypeStruct((rows, 2 * HW), CD), jax.ShapeDtypeStruct((rows, LANES), CD)],
        compiler_params=_cp("parallel"), name=name,
    )(dq, dkcat, dv, cs, sn)


def _loss_head(h, g, tgt, *, lo, hi, tr, name):
    rows, D = h.shape

    def body(h_ref, g_ref, t_ref, dh_ref, dg_ref, loss_ref):
        i = pl.program_id(0)

        @pl.when(i == 0)
        def _():
            dg_ref[...] = jnp.zeros_like(dg_ref)
            loss_ref[...] = jnp.zeros_like(loss_ref)

        xv = h_ref[...]
        r = lax.rsqrt(jnp.mean(xv * xv, axis=-1, keepdims=True) + EPS)
        xh = xv * r
        g_ = g_ref[...]
        rowid = i * tr + lax.broadcasted_iota(jnp.int32, (tr, 1), 0)
        valid = jnp.logical_and(rowid >= lo, rowid < hi)
        err = jnp.where(valid, xh * g_ - t_ref[...], 0.0)
        loss_ref[...] += jnp.sum(jnp.sum(err * err, axis=-1, keepdims=True), axis=0, keepdims=True) * (0.5 / D)
        dy = err * (1.0 / D)
        dxh = dy * g_
        dh_ref[...] = r * (dxh - xh * jnp.mean(dxh * xh, axis=-1, keepdims=True))
        dg_ref[...] += jnp.sum(dy * xh, axis=0, keepdims=True)

    blk = pl.BlockSpec((tr, D), lambda i: (i, 0))
    gsp = pl.BlockSpec((1, D), lambda i: (0, 0))
    return pl.pallas_call(
        body, grid=(rows // tr,), in_specs=[blk, gsp, blk],
        out_specs=[blk, gsp, pl.BlockSpec((1, 1), lambda i: (0, 0))],
        out_shape=[jax.ShapeDtypeStruct((rows, D), F32), jax.ShapeDtypeStruct((1, D), F32), jax.ShapeDtypeStruct((1, 1), F32)],
        compiler_params=_cp("arbitrary"), name=name,
    )(h, g, tgt)


HP = 2
HPW = HP * LANES
HP_FWD = 4


def _head(hh):
    return slice(hh * LANES, (hh + 1) * LANES)


def _qk(hh):
    return slice(hh * QKW, (hh + 1) * QKW)


def _mla_fwd(q, kcat, kv, *, H, bq, scale, name):
    rows = q.shape[0]
    HP, HPW = HP_FWD, HP_FWD * LANES

    def body(q_ref, k_ref, v_ref, o_ref, lse_ref):
        i = pl.program_id(1)
        causal = lax.broadcasted_iota(jnp.int32, (bq, bq), 1) <= lax.broadcasted_iota(jnp.int32, (bq, bq), 0)

        def block(j, carry, diag):
            off = pl.multiple_of(j * bq, bq)
            out = []
            for hh in range(HP):
                m, l, acc = carry[hh]
                s = _dot_nt(q_ref[:, _qk(hh)], k_ref[pl.ds(off, bq), _qk(hh)]) * scale
                if diag:
                    s = jnp.where(causal, s, NEG)
                m_new = jnp.maximum(m, jnp.max(s, axis=-1, keepdims=True))
                p = jnp.exp(s - m_new)
                alpha = jnp.exp(m - m_new)
                l = alpha * l + jnp.sum(p, axis=-1, keepdims=True)
                acc = alpha * acc + _dot(p.astype(CD), v_ref[pl.ds(off, bq), _head(hh)])
                out.append((m_new, l, acc))
            return tuple(out)

        init = tuple((jnp.full((bq, 1), NEG, F32), jnp.zeros((bq, 1), F32), jnp.zeros((bq, LANES), F32)) for _ in range(HP))
        carry = lax.fori_loop(0, i, lambda j, cr: block(j, cr, False), init)
        carry = block(i, carry, True)
        for hh in range(HP):
            m, l, acc = carry[hh]
            o_ref[:, _head(hh)] = (acc / l).astype(o_ref.dtype)
            lse_ref[hh] = m + jnp.log(l)

    return pl.pallas_call(
        body, grid=(H // HP, rows // bq),
        in_specs=[pl.BlockSpec((bq, HP * QKW), lambda h, i: (i, h)), pl.BlockSpec((rows, HP * QKW), lambda h, i: (0, h)),
                  pl.BlockSpec((rows, HPW), lambda h, i: (0, H // HP + h))],
        out_specs=[pl.BlockSpec((bq, HPW), lambda h, i: (i, h)), pl.BlockSpec((HP, bq, 1), lambda h, i: (h, i, 0))],
        out_shape=[jax.ShapeDtypeStruct((rows, H * LANES), CD), jax.ShapeDtypeStruct((H, rows, 1), F32)],
        compiler_params=_cp("arbitrary", "arbitrary"), name=name,
    )(q, kcat, kv)


def _mla_bwd(q, kcat, kv, o, do, lse, *, H, bq, scale, name):
    rows = q.shape[0]

    def body(q_ref, k_ref, v_ref, o_ref, do_ref, lse_ref, dq_ref, dk_ref, dv_ref):
        i = pl.program_id(1)

        @pl.when(i == 0)
        def _():
            dk_ref[...] = jnp.zeros_like(dk_ref)
            dv_ref[...] = jnp.zeros_like(dv_ref)

        causal = lax.broadcasted_iota(jnp.int32, (bq, bq), 1) <= lax.broadcasted_iota(jnp.int32, (bq, bq), 0)
        delta = [jnp.sum(do_ref[:, _head(hh)].astype(F32) * o_ref[:, _head(hh)].astype(F32), axis=-1, keepdims=True) for hh in range(HP)]

        def block(j, carry, diag):
            off = pl.multiple_of(j * bq, bq)
            out = []
            for hh in range(HP):
                q_, do_ = q_ref[:, _qk(hh)], do_ref[:, _head(hh)]
                kj, vj = k_ref[pl.ds(off, bq), _qk(hh)], v_ref[pl.ds(off, bq), _head(hh)]
                p = jnp.exp(_dot_nt(q_, kj) * scale - lse_ref[hh])
                if diag:
                    p = jnp.where(causal, p, 0.0)
                ds = (p * (_dot_nt(do_, vj) - delta[hh]) * scale).astype(CD)
                dk_ref[pl.ds(off, bq), _qk(hh)] += _dot_tn(ds, q_)
                dv_ref[pl.ds(off, bq), _head(hh)] += _dot_tn(p.astype(CD), do_)
                out.append(carry[hh] + _dot(ds, kj))
            return tuple(out)

        init = tuple(jnp.zeros((bq, QKW), F32) for _ in range(HP))
        carry = lax.fori_loop(0, i, lambda j, cr: block(j, cr, False), init)
        carry = block(i, carry, True)
        for hh in range(HP):
            dq_ref[:, _qk(hh)] = carry[hh].astype(dq_ref.dtype)

    qb = pl.BlockSpec((bq, HP * QKW), lambda h, i: (i, h))
    kb = pl.BlockSpec((rows, HP * QKW), lambda h, i: (0, h))
    ob = pl.BlockSpec((bq, HPW), lambda h, i: (i, h))
    return pl.pallas_call(
        body, grid=(H // HP, rows // bq),
        in_specs=[qb, kb, pl.BlockSpec((rows, HPW), lambda h, i: (0, H // HP + h)), ob, ob, pl.BlockSpec((HP, bq, 1), lambda h, i: (h, i, 0))],
        out_specs=[qb, kb, pl.BlockSpec((rows, HPW), lambda h, i: (0, h))],
        out_shape=[jax.ShapeDtypeStruct((rows, H * QKW), CD), jax.ShapeDtypeStruct((rows, H * QKW), F32), jax.ShapeDtypeStruct((rows, H * LANES), F32)],
        compiler_params=pltpu.CompilerParams(dimension_semantics=("arbitrary", "arbitrary"), vmem_limit_bytes=VMEM_LIMIT_WIDE), name=name,
    )(q, kcat, kv, o, do, lse)


def _log_sigmoids(z):
    lb = jnp.minimum(z, 0.0) - jnp.log(1.0 + jnp.exp(-jnp.abs(z)))
    return lb, lb - z


def _split(x):
    hi = x.astype(CD)
    return hi, (x - hi.astype(F32)).astype(CD)


def _rounded(x):
    hi, lo = _split(x)
    return hi.astype(F32) + lo.astype(F32)


def _tri(ck, rel, stack=1):
    j, s = np.arange(ck)[:, None], np.arange(ck)[None, :]
    return jnp.asarray(np.tile({"gt": j > s, "le": j <= s, "lt": j < s}[rel], (stack, 1)), CD)


def _cumsum_mm(hi, lo, tri2):
    return _dot(jnp.concatenate([hi, lo], axis=1), tri2)


def _sb_fwd(proj, *, H, qb0, kb0, vb0, bq, ck, scale, name):
    rows = proj.shape[0]
    nsub = bq // ck
    HP, HPW = HP_FWD, HP_FWD * LANES

    def body(q_ref, k_ref, v_ref, u_ref, o_ref, t_ref):
        i = pl.program_id(1)
        u = u_ref[...]
        strict = lax.broadcasted_iota(jnp.int32, (bq, bq), 1) < lax.broadcasted_iota(jnp.int32, (bq, bq), 0)

        def block(j, carry, diag):
            off = pl.multiple_of(j * bq, bq)
            out = []
            for hh in range(HP):
                c, acc = carry[hh]
                z = _dot_nt(q_ref[:, _head(hh)], k_ref[pl.ds(off, bq), _head(hh)]) * scale
                lb, lom = _log_sigmoids(z)
                if diag:
                    lom = jnp.where(strict, lom, 0.0)
                hi, lo = _split(lom)
                parts = [None] * nsub
                for s in reversed(range(nsub)):
                    ss = slice(s * ck, (s + 1) * ck)
                    rin = _cumsum_mm(hi[:, ss], lo[:, ss], u)
                    parts[s] = jnp.exp(lb[:, ss] + rin + c)
                    c = c + rin[:, :1] + _rounded(lom[:, s * ck:s * ck + 1])
                a = parts[0] if nsub == 1 else jnp.concatenate(parts, axis=1)
                if diag:
                    a = jnp.where(strict, a, 0.0)
                out.append((c, acc + _dot(a.astype(CD), v_ref[pl.ds(off, bq), _head(hh)])))
            return tuple(out)

        init = tuple((jnp.zeros((bq, 1), F32), jnp.zeros((bq, LANES), F32)) for _ in range(HP))
        carry = block(i, init, True)
        carry = lax.fori_loop(0, i, lambda jj, cr: block(i - 1 - jj, cr, False), carry)
        for hh in range(HP):
            o_ref[:, _head(hh)] = carry[hh][1].astype(o_ref.dtype)
            t_ref[hh] = carry[hh][0]

    kb = lambda off: pl.BlockSpec((rows, HPW), lambda h, i: (0, off // HP + h))
    return pl.pallas_call(
        body, grid=(H // HP, rows // bq),
        in_specs=[pl.BlockSpec((bq, HPW), lambda h, i: (i, qb0 // HP + h)), kb(kb0), kb(vb0), pl.BlockSpec((2 * ck, ck), lambda h, i: (0, 0))],
        out_specs=[pl.BlockSpec((bq, HPW), lambda h, i: (i, h)), pl.BlockSpec((HP, bq, 1), lambda h, i: (h, i, 0))],
        out_shape=[jax.ShapeDtypeStruct((rows, H * LANES), CD), jax.ShapeDtypeStruct((H, rows, 1), F32)],
        compiler_params=_cp("arbitrary", "arbitrary"), name=name,
    )(proj, proj, proj, _tri(ck, "gt", 2))


def _sb_bwd(proj, do, tot, *, H, qb0, kb0, vb0, bq, ck, scale, name):
    rows = proj.shape[0]
    nsub = bq // ck

    def body(q_ref, k_ref, v_ref, do_ref, t_ref, ule_ref, ult_ref, dq_ref, dk_ref, dv_ref):
        i = pl.program_id(1)

        @pl.when(i == 0)
        def _():
            dk_ref[...] = jnp.zeros_like(dk_ref)
            dv_ref[...] = jnp.zeros_like(dv_ref)

        ule, ult = ule_ref[...], ult_ref[...]
        strict = lax.broadcasted_iota(jnp.int32, (bq, bq), 1) < lax.broadcasted_iota(jnp.int32, (bq, bq), 0)

        def block(j, carry, diag):
            off = pl.multiple_of(j * bq, bq)
            out = []
            for hh in range(HP):
                pc, gc, dq = carry[hh]
                q_, do_, tot_ = q_ref[:, _head(hh)], do_ref[:, _head(hh)], t_ref[hh]
                kj, vj = k_ref[pl.ds(off, bq), _head(hh)], v_ref[pl.ds(off, bq), _head(hh)]
                z = _dot_nt(q_, kj) * scale
                lb, lom = _log_sigmoids(z)
                if diag:
                    lom = jnp.where(strict, lom, 0.0)
                hi, lo = _split(lom)
                parts = []
                for s in range(nsub):
                    ss = slice(s * ck, (s + 1) * ck)
                    pin = _cumsum_mm(hi[:, ss], lo[:, ss], ule)
                    parts.append(jnp.exp(lb[:, ss] + (tot_ - (pc + pin))))
                    pc = pc + pin[:, ck - 1:ck]
                a = parts[0] if nsub == 1 else jnp.concatenate(parts, axis=1)
                if diag:
                    a = jnp.where(strict, a, 0.0)
                g = a * _dot_nt(do_, vj)
                gb = g.astype(CD)
                parts = []
                for s in range(nsub):
                    ss = slice(s * ck, (s + 1) * ck)
                    gin = _dot(gb[:, ss], ult)
                    parts.append(gc + gin)
                    gc = gc + gin[:, ck - 1:ck] + g[:, (s + 1) * ck - 1:(s + 1) * ck].astype(CD).astype(F32)
                big_g = parts[0] if nsub == 1 else jnp.concatenate(parts, axis=1)
                sg = jnp.exp(lb)
                dz = (g * (1.0 - sg) - big_g * sg) * scale
                if diag:
                    dz = jnp.where(strict, dz, 0.0)
                dz = dz.astype(CD)
                dk_ref[pl.ds(off, bq), _head(hh)] += _dot_tn(dz, q_)
                dv_ref[pl.ds(off, bq), _head(hh)] += _dot_tn(a.astype(CD), do_)
                out.append((pc, gc, dq + _dot(dz, kj)))
            return tuple(out)

        init = tuple((jnp.zeros((bq, 1), F32), jnp.zeros((bq, 1), F32), jnp.zeros((bq, LANES), F32)) for _ in range(HP))
        carry = lax.fori_loop(0, i, lambda j, cr: block(j, cr, False), init)
        carry = block(i, carry, True)
        for hh in range(HP):
            dq_ref[:, _head(hh)] = carry[hh][2].astype(dq_ref.dtype)

    kb = lambda off: pl.BlockSpec((rows, HPW), lambda h, i: (0, off // HP + h))
    ob = pl.BlockSpec((bq, HPW), lambda h, i: (i, h))
    tri = pl.BlockSpec((ck, ck), lambda h, i: (0, 0))
    acc = jax.ShapeDtypeStruct((rows, H * LANES), F32)
    return pl.pallas_call(
        body, grid=(H // HP, rows // bq),
        in_specs=[pl.BlockSpec((bq, HPW), lambda h, i: (i, qb0 // HP + h)), kb(kb0), kb(vb0), ob,
                  pl.BlockSpec((HP, bq, 1), lambda h, i: (h, i, 0)), pl.BlockSpec((2 * ck, ck), lambda h, i: (0, 0)), tri],
        out_specs=[ob, kb(0), kb(0)],
        out_shape=[jax.ShapeDtypeStruct((rows, H * LANES), CD), acc, acc],
        compiler_params=_cp("arbitrary", "arbitrary"), name=name,
    )(proj, proj, proj, do, tot, _tri(ck, "le", 2), _tri(ck, "lt"))


def _dev_index(p):
    return 4 * p[0] + 2 * p[1] + p[2]


def _all_gather(arrays, *, name, after=None):
    A = len(arrays)
    extra = [] if after is None else [after]

    def body(*refs):
        ins, outs = refs[:A], refs[A + len(extra):2 * A + len(extra)]
        send_sems, recv_sems, local_sems = refs[2 * A + len(extra):]
        x, y, c = lax.axis_index("x"), lax.axis_index("y"), lax.axis_index("c")
        me, sibling = (x, y, c), (x, y, 1 - c)
        chips = [(1 - x, y), (x, 1 - y), (1 - x, 1 - y)]

        def copy(a, k, block, to, src=None):
            dst = outs[a].at[_dev_index(block)]
            return pltpu.make_async_remote_copy(
                src_ref=dst if src is None else src, dst_ref=dst, send_sem=send_sems.at[a * 7 + k],
                recv_sem=recv_sems.at[a * 7 + k], device_id=to, device_id_type=MESH)

        mine = [pltpu.make_async_copy(ins[a], outs[a].at[_dev_index(me)], local_sems.at[a]) for a in range(A)]
        for cp in mine:
            cp.start()
        first = []
        for a in range(A):
            first.append(copy(a, 0, me, sibling, src=ins[a]))
            first += [copy(a, 1 + j, me, (*chip, c), src=ins[a]) for j, chip in enumerate(chips)]
        for cp in first:
            cp.start()
        passed = []
        for j, chip in enumerate(chips):
            for a in range(A):
                copy(a, 1 + j, (*chip, c), me).wait_recv()
                fwd = copy(a, 4 + j, (*chip, c), sibling)
                fwd.start()
                passed.append(fwd)
        for a in range(A):
            copy(a, 0, sibling, me).wait_recv()
            for j, chip in enumerate(chips):
                copy(a, 4 + j, (*chip, 1 - c), me).wait_recv()
        for cp in first + passed:
            cp.wait_send()
        for cp in mine:
            cp.wait()

    any_spec = pl.BlockSpec(memory_space=pl.ANY)
    return pl.pallas_call(
        body, in_specs=[any_spec] * (A + len(extra)), out_specs=[any_spec] * A,
        out_shape=[jax.ShapeDtypeStruct((N_DEV,) + a.shape, a.dtype) for a in arrays],
        scratch_shapes=[pltpu.SemaphoreType.DMA((7 * A,)), pltpu.SemaphoreType.DMA((7 * A,)), pltpu.SemaphoreType.DMA((A,))],
        name=name,
    )(*arrays, *extra)


_HBM = pl.BlockSpec(memory_space=pltpu.HBM)
_SEM = pl.BlockSpec(memory_space=pltpu.SEMAPHORE)
_EFFECT = pltpu.SideEffectType.DATAFLOW_SIDE_EFFECTING
_PEER_ORDER = (0, 1, 3, 2, 4, 5, 6)


def _peers():
    x, y, c = lax.axis_index("x"), lax.axis_index("y"), lax.axis_index("c")
    return _dev_index((x, y, c)), [((1 - x) if r & 4 else x, (1 - y) if r & 2 else y, (1 - c) if r & 1 else c) for r in range(1, N_DEV)]


def _xchg_copy(src, land, sends, recvs, a, k, me_i, peers, scatter, arriving):
    peer_i = _dev_index(peers[k])
    return pltpu.make_async_remote_copy(
        src_ref=src.at[peer_i] if scatter else src, dst_ref=land.at[peer_i if arriving else me_i],
        send_sem=sends.at[a * 7 + k], recv_sem=recvs.at[a * 7 + k], device_id=peers[k], device_id_type=MESH)


def _xchg_start(arrays, *, scatter, name, after=None):
    A = len(arrays)
    lands = [lax.empty(a.shape if scatter else (N_DEV,) + a.shape, a.dtype) for a in arrays]
    extra = [] if after is None else [after]

    def body(*refs):
        srcs, land = refs[:A], refs[A:2 * A]
        sends, recvs, token = refs[2 * A + len(extra)], refs[2 * A + len(extra) + 1], refs[-1]
        me_i, peers = _peers()
        for k in _PEER_ORDER:
            for a in range(A):
                _xchg_copy(srcs[a], land[a], sends, recvs, a, k, me_i, peers, scatter, False).start()
        token[...] = jnp.zeros_like(token)

    hbm = lambda a: pltpu.HBM(a.shape, a.dtype)
    outs = pl.pallas_call(
        body, name=name, in_specs=[_HBM] * (2 * A) + [pl.BlockSpec(memory_space=pl.ANY)] * len(extra),
        out_shape=(pltpu.SemaphoreType.DMA((7 * A,)), pltpu.SemaphoreType.DMA((7 * A,)), *[hbm(a) for a in arrays],
                   *[hbm(a) for a in lands], jax.ShapeDtypeStruct((8, LANES), F32)),
        out_specs=(_SEM, _SEM, *([_HBM] * (2 * A)), pl.BlockSpec(memory_space=pltpu.VMEM)),
        input_output_aliases={n: 2 + n for n in range(2 * A)},
        compiler_params=pltpu.CompilerParams(has_side_effects=_EFFECT),
    )(*[pltpu.with_memory_space_constraint(a, pltpu.HBM) for a in arrays],
      *[pltpu.with_memory_space_constraint(a, pltpu.HBM) for a in lands], *extra)
    return outs[0], outs[1], list(outs[2:2 + A]), list(outs[2 + A:2 + 2 * A]), outs[-1]


def _xchg_wait(handle, after, *, scatter, name):
    sends, recvs, srcs, lands, _ = handle
    A = len(srcs)

    def body(*refs):
        src, land = refs[:A], refs[A:2 * A]
        send_sems, recv_sems = refs[2 * A], refs[2 * A + 1]
        me_i, peers = _peers()
        for k in _PEER_ORDER:
            for a in range(A):
                cp = _xchg_copy(src[a], land[a], send_sems, recv_sems, a, k, me_i, peers, scatter, True)
                cp.wait_send()
                cp.wait_recv()

    outs = pl.pallas_call(
        body, name=name, in_specs=[_HBM] * (2 * A) + [_SEM, _SEM, pl.BlockSpec(memory_space=pl.ANY)],
        out_shape=tuple(pltpu.HBM(a.shape, a.dtype) for a in srcs + lands), out_specs=tuple([_HBM] * (2 * A)),
        input_output_aliases={n: n for n in range(2 * A)},
        compiler_params=pltpu.CompilerParams(has_side_effects=_EFFECT),
    )(*srcs, *lands, sends, recvs, after)
    my = _dev_index((lax.axis_index("x"), lax.axis_index("y"), lax.axis_index("c")))
    got = []
    for src, land in zip(outs[:A], outs[A:]):
        own = lax.dynamic_index_in_dim(src, my, 0, keepdims=True) if scatter else src[None]
        got.append(lax.dynamic_update_index_in_dim(land, own, my, 0))
    return got


def _adamw(g, w, m, v):
    m = ADAM_B1 * m + (1.0 - ADAM_B1) * g
    v = ADAM_B2 * v + (1.0 - ADAM_B2) * (g * g)
    delta = -ADAM_LR * ((m / ADAM_BC1) / (jnp.sqrt(v / ADAM_BC2) + ADAM_EPS) + ADAM_WD * w)
    return delta, m, v


def _adam_pieces(pieces, w, m, v, bufs, *, layer, name):
    depth, R, C = w.shape
    if R % 16 == 0:
        tr, tc = _pick(R, [t for t in (512, 256, 128, 64, 32, 16) if t * C * 4 <= 2**20]), C
    else:
        tr, tc = R, _pick(C, [t for t in (512, 256, 128) if R * t * 4 <= 2**21])

    def body(p_ref, w_ref, m_ref, v_ref, b0, b1, b2, b3, g_ref, d_ref, mo_ref, vo_ref):
        g = p_ref[0].astype(F32)
        for k in range(1, N_DEV):
            g = g + p_ref[k].astype(F32)
        g_ref[0] = g
        d_ref[0], mo_ref[0], vo_ref[0] = _adamw(g, w_ref[0], m_ref[0], v_ref[0])

    blk = pl.BlockSpec((1, tr, tc), lambda i, j: (layer, i, j))
    anyspec = pl.BlockSpec(memory_space=pl.ANY)
    return pl.pallas_call(
        body, grid=(R // tr, C // tc),
        in_specs=[pl.BlockSpec((N_DEV, tr, tc), lambda i, j: (0, i, j)), blk, blk, blk] + [anyspec] * 4,
        out_specs=[blk] * 4, out_shape=[jax.ShapeDtypeStruct((depth, R, C), F32)] * 4,
        input_output_aliases={4: 0, 5: 1, 6: 2, 7: 3}, compiler_params=_cp("parallel", "parallel"), name=name,
    )(pieces, w, m, v, *bufs)


def _adam_direct(g, w, m, v, *, name):
    sh = jax.ShapeDtypeStruct(w.shape, F32)

    def body(g_ref, w_ref, m_ref, v_ref, d_ref, mo_ref, vo_ref):
        d_ref[...], mo_ref[...], vo_ref[...] = _adamw(g_ref[...], w_ref[...], m_ref[...], v_ref[...])

    return pl.pallas_call(body, out_shape=[sh] * 3, compiler_params=_cp(), name=name)(g, w, m, v)


def _sum_devices(parts, *, name):
    def body(p_ref, o_ref):
        s = p_ref[0]
        for k in range(1, N_DEV):
            s = s + p_ref[k]
        o_ref[...] = s

    return pl.pallas_call(body, out_shape=jax.ShapeDtypeStruct(parts.shape[1:], F32), compiler_params=_cp(), name=name)(parts)


def _rope_cols_pad(w):
    z = jnp.zeros(w.shape[:-1] + (ROPE // 2,), w.dtype)
    return jnp.concatenate([w[..., :ROPE // 2], z, w[..., ROPE // 2:], z], axis=-1)


def _rope_cols_unpad(w):
    return jnp.concatenate([w[..., :ROPE // 2], w[..., ROPE:ROPE + ROPE // 2]], axis=-1)


class _Dims:
    def __init__(self, D, S, NM, QL, KVL, WM, WS):
        self.D, self.S, self.NM, self.QL, self.KVL, self.WM, self.WS = D, S, NM, QL, KVL, WM, WS
        self.H = WM // LANES
        self.HW = self.H * LANES
        self.L = NM + S
        self.bq = 384 if self.L > 1024 else 128
        self.Lp = -(-self.L // self.bq) * self.bq
        self.ck_sb = LANES
        self.tr = 128
        self.sizes = (QL, KVL, ROPE, WM, WS, WS, WS, WS)
        self.d_in = sum(self.sizes)
        front = QL + KVL + LANES
        self.front = -(-front // WM) * WM
        self.pad = self.front - front
        self.dn = self.front + WM + 4 * WS
        assert WM == WS and QL % KVL == 0 and KVL % LANES == 0 and self.d_in % N_DEV == 0 and D % N_DEV == 0
        self.mla_scale = 1.0 / math.sqrt(LANES + ROPE)
        self.sb_scale = 1.0 / math.sqrt(LANES)


def _w_in_nice(d, wt):
    parts = jnp.split(wt, np.cumsum(d.sizes)[:-1].tolist(), axis=0)
    z = jnp.zeros((ROPE // 2, wt.shape[1]), wt.dtype)
    k_r = [parts[2][:ROPE // 2], z, parts[2][ROPE // 2:], z]
    return jnp.concatenate([parts[0], parts[1]] + k_r + [jnp.zeros((d.pad, wt.shape[1]), wt.dtype)] + parts[3:], axis=0)


def _w_in_orig(d, wt):
    a, b = d.QL + d.KVL, d.front
    return jnp.concatenate([wt[:a + ROPE // 2], wt[a + ROPE:a + ROPE + ROPE // 2], wt[b:]], axis=0)


def _w_uq_nice(d, w):
    w3 = w.reshape(d.QL, d.H, LANES + ROPE)
    return jnp.concatenate([w3[..., :LANES], _rope_cols_pad(w3[..., LANES:])], axis=-1).reshape(d.QL, d.H * QKW)


def _w_uq_orig(d, w):
    w3 = w.reshape(d.QL, d.H, QKW)
    return jnp.concatenate([w3[..., :LANES], _rope_cols_unpad(w3[..., LANES:])], axis=-1).reshape(d.QL, d.H * (LANES + ROPE))


def _w_ukv_nice(d, w):
    return w.reshape(d.KVL, d.H, 2, LANES).transpose(0, 2, 1, 3).reshape(d.KVL, 2 * d.HW)


def _w_ukv_orig(d, w):
    return w.reshape(d.KVL, 2, d.H, LANES).transpose(0, 2, 1, 3).reshape(d.KVL, 2 * d.HW)


def _cols_from_shards(g):
    return g.transpose(1, 0, 2).reshape(g.shape[1], -1)


def _cols_to_shards(w):
    return w.reshape(w.shape[0], N_DEV, -1).transpose(1, 0, 2)


def _rope_tables(rows):
    inv_freq = ROPE_THETA ** (-jnp.arange(0, ROPE, 2, dtype=F32) / ROPE)
    ang = jnp.arange(rows, dtype=jnp.int32).astype(F32)[:, None] * inv_freq[None, :]
    cos, sin, z = jnp.cos(ang), jnp.sin(ang), jnp.zeros_like(ang)
    return jnp.concatenate([cos, z, cos, z], axis=1), jnp.concatenate([-sin, z, sin, z], axis=1)


def _flat_pack(vecs):
    flat = jnp.concatenate([v.reshape(-1).astype(F32) for v in vecs])
    n = flat.shape[0]
    rows = -(-n // (8 * 1024)) * 8
    offs = np.cumsum([0] + [int(np.prod(v.shape)) for v in vecs])
    return jnp.pad(flat, (0, rows * 1024 - n)).reshape(rows, 1024), offs


def kernel(x, meta_tokens, g_norm, w_in, g_q, g_kv, w_uq, w_ukv, g_out_mla, g_out_sb, w_o, g_final, loss_target, m_meta_tokens, m_g_norm, m_w_in, m_g_q, m_g_kv, m_w_uq, m_w_ukv, m_g_out_mla, m_g_out_sb, m_w_o, m_g_final, v_meta_tokens, v_g_norm, v_w_in, v_g_q, v_g_kv, v_w_uq, v_w_ukv, v_g_out_mla, v_g_out_sb, v_w_o, v_g_final):
    depth = g_norm.shape[0]
    d = _Dims(D=x.shape[-1], S=x.shape[1], NM=meta_tokens.shape[0], QL=g_q.shape[1], KVL=g_kv.shape[1],
              WM=g_out_mla.shape[1], WS=g_out_sb.shape[1])
    D, Lp, H, tr = d.D, d.Lp, d.H, d.tr
    my = _dev_index((lax.axis_index("x"), lax.axis_index("y"), lax.axis_index("c")))

    w_in_t, m_w_in_t, v_w_in_t = (jnp.swapaxes(a, 1, 2) for a in (w_in, m_w_in, v_w_in))
    shards = lambda l: [w_in_t[l].astype(CD), w_uq[l].astype(CD), w_ukv[l].astype(CD), w_o[l].astype(CD)]
    g_in0, g_meta = _all_gather([shards(0)[0], meta_tokens], name="gather_layer0")
    gathers, token = {}, g_meta
    for l in range(depth):
        gathers[l] = _xchg_start(shards(l)[1:] if l == 0 else shards(l), scatter=False, name=f"gather_start_{l}", after=token)
        token = gathers[l][4]
    started = token[0, 0]

    def relayout(g_uq, g_ukv, g_o):
        return _w_uq_nice(d, _cols_from_shards(g_uq)), _w_ukv_nice(d, _cols_from_shards(g_ukv)), g_o.reshape(D, D)

    wn, wuq, wukv, wo = [None] * depth, [None] * depth, [None] * depth, [None] * depth
    meta_full = _cols_from_shards(g_meta)

    cs, sn = _rope_tables(Lp)
    h = jnp.concatenate([meta_full, x[0], jnp.zeros((Lp - d.L, D), F32)], axis=0)
    tgt = jnp.concatenate([jnp.zeros((d.NM, D), F32), loss_target[0], jnp.zeros((Lp - d.L, D), F32)], axis=0)

    b128 = lambda col: col // LANES
    krb = b128(d.QL + d.KVL)
    zb_mla, zb_sb = d.front // d.WM, (d.front + d.WM + 3 * d.WS) // d.WS
    qb0, kb0, vb0 = b128(d.front + d.WM), b128(d.front + d.WM + d.WS), b128(d.front + d.WM + 2 * d.WS)
    row = lambda a, i: a[i][None, :]

    saved = []
    for i in range(depth):
        if i == 0:
            wn[i] = _w_in_nice(d, g_in0.reshape(d.d_in, D))
            u = _rms_fwd(h, row(g_norm, i) + started, cb=0, width=D, tr=tr, name=f"rms_in_{i}")
            proj = _mm(u, wn[i], tb=True, out_dtype=CD, name=f"proj_{i}")
            wuq[i], wukv[i], wo[i] = relayout(*_xchg_wait(gathers[i], proj, scatter=False, name=f"gather_wait_{i}"))
        else:
            g_in, *g_rest = _xchg_wait(gathers[i], h, scatter=False, name=f"gather_wait_{i}")
            wn[i] = _w_in_nice(d, g_in.reshape(d.d_in, D))
            wuq[i], wukv[i], wo[i] = relayout(*g_rest)
            u = _rms_fwd(h, row(g_norm, i), cb=0, width=D, tr=tr, name=f"rms_in_{i}")
            proj = _mm(u, wn[i], tb=True, out_dtype=CD, name=f"proj_{i}")
        cqn = _rms_fwd(proj, row(g_q, i), cb=0, width=d.QL, tr=tr, name=f"rms_q_{i}")
        ckvn = _rms_fwd(proj, row(g_kv, i), cb=d.QL // d.KVL, width=d.KVL, tr=tr, name=f"rms_kv_{i}")
        q_raw = _mm(cqn, wuq[i], out_dtype=F32, name=f"up_q_{i}")
        kvn = _mm(ckvn, wukv[i], out_dtype=CD, name=f"up_kv_{i}")
        q, kcat = _rope_fwd(q_raw, kvn, proj, cs, sn, H=H, krb=krb, tr=tr, name=f"rope_{i}")
        o_mla, lse = _mla_fwd(q, kcat, kvn, H=H, bq=d.bq, scale=d.mla_scale, name=f"mla_fwd_{i}")
        o_sb, tot = _sb_fwd(proj, H=H, qb0=qb0, kb0=kb0, vb0=vb0, bq=d.bq, ck=d.ck_sb, scale=d.sb_scale, name=f"sb_fwd_{i}")
        y = _gate_fwd(o_mla, o_sb, proj, row(g_out_mla, i), row(g_out_sb, i), zb_mla=zb_mla, zb_sb=zb_sb, tr=tr, name=f"gate_fwd_{i}")
        h_next = _mm(y, wo[i], out_dtype=F32, res=h, name=f"out_proj_{i}")
        saved.append((h, u, proj, cqn, ckvn, q, kvn, kcat, o_mla, lse, o_sb, tot, y))
        h = h_next

    dh, dg_final, loss_part = _loss_head(h, g_final[None, :], tgt, lo=d.NM, hi=d.L, tr=tr, name="loss_head")

    dg_norm, dg_q, dg_kv, dg_om, dg_os = [None] * depth, [None] * depth, [None] * depth, [None] * depth, [None] * depth
    big_w = {"w_in": (w_in_t, m_w_in_t, v_w_in_t), "w_uq": (w_uq, m_w_uq, v_w_uq), "w_ukv": (w_ukv, m_w_ukv, v_w_ukv), "w_o": (w_o, m_w_o, v_w_o)}
    res = {n: [lax.empty(w.shape, F32) for _ in range(4)] for n, (w, _, _) in big_w.items()}
    exchanges = [[] for _ in range(depth)]

    def send(l, tag, pieces, after=None):
        handle = _xchg_start(list(pieces.values()), scatter=True, name=f"exch_start_{l}{tag}", after=after)
        exchanges[l].append((handle, list(pieces), tag))
        return handle[4]

    def finish(l, after):
        got = None
        for handle, sent, tag in exchanges[l]:
            got = _xchg_wait(handle, after, scatter=True, name=f"exch_wait_{l}{tag}")
            for n, p in zip(sent, got):
                w, m, v = big_w[n]
                res[n] = _adam_pieces(p, w, m, v, res[n], layer=l, name=f"adam_{n}_{l}")
        return got[0]

    cat = lambda parts: jnp.concatenate(parts, axis=0)
    zpad = jnp.zeros((Lp, d.pad), CD)
    for i in reversed(range(depth)):
        h_in, u, proj, cqn, ckvn, q, kvn, kcat, o_mla, lse, o_sb, tot, y = saved[i]
        dy = _mm(dh, wo[i], tb=True, out_dtype=CD, name=f"d_y_{i}")
        dwo = _mm(y, dh, ta=True, out_dtype=CD, name=f"d_wo_{i}")
        do_mla, do_sb, dz_mla, dz_sb, dg_om[i], dg_os[i] = _gate_bwd(
            dy, o_mla, o_sb, proj, row(g_out_mla, i), row(g_out_sb, i), zb_mla=zb_mla, zb_sb=zb_sb, tr=tr, name=f"gate_bwd_{i}")
        dq_mla, dkcat, dv = _mla_bwd(q, kcat, kvn, o_mla, do_mla, lse, H=H, bq=d.bq, scale=d.mla_scale, name=f"mla_bwd_{i}")
        dq_sb, dk_sb, dv_sb = _sb_bwd(proj, do_sb, tot, H=H, qb0=qb0, kb0=kb0, vb0=vb0, bq=d.bq, ck=d.ck_sb, scale=d.sb_scale, name=f"sb_bwd_{i}")
        dq, dkv, dk_r = _rope_bwd(dq_mla, dkcat, dv, cs, sn, H=H, tr=tr, name=f"rope_bwd_{i}")
        dcqn = _mm(dq, wuq[i], tb=True, out_dtype=CD, name=f"d_cq_{i}")
        dwuq = _mm(cqn, dq, ta=True, out_dtype=CD, name=f"d_wuq_{i}")
        dckvn = _mm(dkv, wukv[i], tb=True, out_dtype=CD, name=f"d_ckv_{i}")
        dwukv = _mm(ckvn, dkv, ta=True, out_dtype=CD, name=f"d_wukv_{i}")
        dc_q, dg_q[i] = _rms_bwd(proj, row(g_q, i), dcqn, cb=0, width=d.QL, tr=tr, out_dtype=CD, name=f"rms_q_bwd_{i}")
        dc_kv, dg_kv[i] = _rms_bwd(proj, row(g_kv, i), dckvn, cb=d.QL // d.KVL, width=d.KVL, tr=tr, out_dtype=CD, name=f"rms_kv_bwd_{i}")
        rest = {"w_uq": _cols_to_shards(_w_uq_orig(d, dwuq)), "w_ukv": _cols_to_shards(_w_ukv_orig(d, dwukv)),
                "w_o": dwo.reshape(N_DEV, D // N_DEV, D)}
        placed = None
        if i == 0:
            small_a, offs_a = _flat_pack(([cat(dg_norm[1:])] if depth > 1 else []) + [cat(dg_q), cat(dg_kv), cat(dg_om), cat(dg_os), dg_final, loss_part])
            (small_a_all,) = _all_gather([small_a], name="gather_small_a")
            placed = send(0, "a", rest, after=small_a_all)
        dproj = jnp.concatenate([dc_q, dc_kv, dk_r, zpad, dz_mla, dq_sb, dk_sb.astype(CD), dv_sb.astype(CD), dz_sb], axis=1)
        dwn = _mm(dproj, u, ta=True, out_dtype=CD, after=placed, name=f"d_wn_{i}")
        pieces = {"w_in": _w_in_orig(d, dwn).reshape(N_DEV, d.d_in // N_DEV, D)}
        if i > 0:
            pieces.update(rest)
        du = _mm(dproj, wn[i], out_dtype=CD, after=send(i, "b", pieces), name=f"d_u_{i}")
        dh, dg_norm[i] = _rms_bwd(h_in, row(g_norm, i), du, cb=0, width=D, tr=tr, res=dh, out_dtype=F32, name=f"rms_in_bwd_{i}")
        if i + 1 < depth:
            finish(i + 1, dh)
    landed = finish(0, dh)

    small_b, offs_b = _flat_pack([dg_norm[0], dh[:d.NM]])
    (small_b_all,) = _all_gather([small_b], name="gather_small_b", after=landed)
    sum_a = _sum_devices(small_a_all, name="sum_small_a").reshape(-1)
    sum_b = _sum_devices(small_b_all, name="sum_small_b").reshape(-1)
    n_a = int(offs_a[-2])
    small_sum = jnp.concatenate([sum_b[:D], sum_a[:n_a], sum_b[D:offs_b[2]], sum_a[n_a:n_a + 1]])
    offs = np.cumsum([0, depth * D, depth * d.QL, depth * d.KVL, depth * d.WM, depth * d.WS, D, d.NM * D, 1])
    seg = lambda k, shape: small_sum[offs[k]:offs[k + 1]].reshape(shape)
    loss = small_sum[offs[7]]
    grad_x = dh[d.NM:d.L][None]

    g_meta_full = seg(6, (d.NM, D))
    ncol = meta_tokens.shape[1]
    g_meta_mine = lax.dynamic_slice(g_meta_full, (0, my * ncol), (d.NM, ncol))
    res["meta_tokens"] = [g_meta_mine] + list(_adam_direct(g_meta_mine, meta_tokens, m_meta_tokens, v_meta_tokens, name="adam_meta"))

    small_names = ["g_norm", "g_q", "g_kv", "g_out_mla", "g_out_sb", "g_final"]
    small_w = [g_norm, g_q, g_kv, g_out_mla, g_out_sb, g_final]
    small_m = [m_g_norm, m_g_q, m_g_kv, m_g_out_mla, m_g_out_sb, m_g_final]
    small_v = [v_g_norm, v_g_q, v_g_kv, v_g_out_mla, v_g_out_sb, v_g_final]
    g_flat, _ = _flat_pack([small_sum[:offs[6]]])
    wf, _ = _flat_pack(small_w)
    mf, _ = _flat_pack(small_m)
    vf, _ = _flat_pack(small_v)
    d_f, m_f, v_f = _adam_direct(g_flat, wf, mf, vf, name="adam_small")
    for k, nm in enumerate(small_names):
        sl = lambda f: f.reshape(-1)[offs[k]:offs[k + 1]].reshape(small_w[k].shape)
        res[nm] = [sl(g_flat), sl(d_f), sl(m_f), sl(v_f)]

    names = ["meta_tokens", "g_norm", "w_in", "g_q", "g_kv", "w_uq", "w_ukv", "g_out_mla", "g_out_sb", "w_o", "g_final"]
    out = lambda n, j: jnp.swapaxes(res[n][j], 1, 2) if n == "w_in" else res[n][j]
    return (loss, grad_x, *[out(n, 0) for n in names], *[out(n, 1) for n in names],
            *[out(n, 2) for n in names], *[out(n, 3) for n in names])
```

```python
import math

import numpy as np
import jax
import jax.numpy as jnp
from jax import lax
from jax.experimental import pallas as pl
from jax.experimental.pallas import tpu as pltpu

CD = jnp.bfloat16
F32 = jnp.float32

LANES = 128
N_DEV = 8
EPS = 1e-6
ROPE = 64
ROPE_THETA = 10000.0
NEG = -1e30
VMEM_LIMIT = 48 * 2**20
VMEM_LIMIT_WIDE = 56 * 2**20

ADAM_LR, ADAM_B1, ADAM_B2, ADAM_EPS, ADAM_WD, ADAM_STEP = 0.001, 0.9, 0.999, 1e-08, 0.01, 10
ADAM_BC1 = 1.0 - ADAM_B1**ADAM_STEP
ADAM_BC2 = 1.0 - ADAM_B2**ADAM_STEP

MESH = pl.DeviceIdType.MESH


def _cp(*sem):
    return pltpu.CompilerParams(dimension_semantics=tuple(sem) if sem else None, vmem_limit_bytes=VMEM_LIMIT)


def _pick(dim, prefs):
    for p in prefs:
        if dim % p == 0:
            return p
    raise ValueError(f"no tile for {dim} in {prefs}")


def _dot(a, b):
    return jnp.dot(a, b, preferred_element_type=F32)


def _dot_nt(a, b):
    return lax.dot_general(a, b, (((1,), (1,)), ((), ())), preferred_element_type=F32)


def _dot_tn(a, b):
    return lax.dot_general(a, b, (((0,), (0,)), ((), ())), preferred_element_type=F32)


def _mm(a, b, *, ta=False, tb=False, out_dtype, res=None, after=None, name):
    M, K = (a.shape[1], a.shape[0]) if ta else a.shape
    N = b.shape[0] if tb else b.shape[1]
    tm = _pick(M, (1056, 1024, 512, 256, 128))
    tk = _pick(K, (4224, 2112, 1056, 1024, 512, 384, 256, 128) if ta else (4096, 2048, 1024, 512, 384, 256, 128)[a.dtype == F32:])
    if K > tk > 2112:
        tk = _pick(K, (2048, 1024, 512, 384, 256, 128))
    tn = _pick(N, (512, 384, 256, 128) if tk > 2112 else (1024, 512, 384, 256, 128))
    nk = K // tk
    a_dims = (((0,), (1 if tb else 0,)), ((), ())) if ta else (((1,), (1 if tb else 0,)), ((), ()))

    def body(*refs):
        if after is not None:
            refs = refs[:-3] + refs[-2:]
        if res is None:
            a_ref, b_ref, o_ref, acc = refs
        else:
            a_ref, b_ref, r_ref, o_ref, acc = refs
        k = pl.program_id(2)

        @pl.when(k == 0)
        def _():
            acc[...] = jnp.zeros_like(acc)

        acc[...] += lax.dot_general(a_ref[...].astype(CD), b_ref[...].astype(CD), a_dims, preferred_element_type=F32)

        @pl.when(k == nk - 1)
        def _():
            r = acc[...]
            if res is not None:
                r = r + r_ref[...]
            o_ref[...] = r.astype(out_dtype)

    a_spec = pl.BlockSpec((tk, tm), lambda i, j, k: (k, i)) if ta else pl.BlockSpec((tm, tk), lambda i, j, k: (i, k))
    b_spec = pl.BlockSpec((tn, tk), lambda i, j, k: (j, k)) if tb else pl.BlockSpec((tk, tn), lambda i, j, k: (k, j))
    o_spec = pl.BlockSpec((tm, tn), lambda i, j, k: (i, j))
    in_specs, args = [a_spec, b_spec], [a, b]
    if res is not None:
        in_specs.append(o_spec)
        args.append(res)
    if after is not None:
        in_specs.append(pl.BlockSpec(memory_space=pl.ANY))
        args.append(after)
    return pl.pallas_call(
        body, grid=(M // tm, N // tn, nk), in_specs=in_specs, out_specs=o_spec,
        out_shape=jax.ShapeDtypeStruct((M, N), out_dtype), scratch_shapes=[pltpu.VMEM((tm, tn), F32)],
        compiler_params=_cp("parallel", "parallel", "arbitrary"), name=name,
    )(*args)


def _rms_fwd(x, g, *, cb, width, tr, name):
    rows = x.shape[0]

    def body(x_ref, g_ref, o_ref):
        xv = x_ref[...].astype(F32)
        r = lax.rsqrt(jnp.mean(xv * xv, axis=-1, keepdims=True) + EPS)
        o_ref[...] = (xv * r * g_ref[...]).astype(o_ref.dtype)

    return pl.pallas_call(
        body, grid=(rows // tr,),
        in_specs=[pl.BlockSpec((tr, width), lambda i: (i, cb)), pl.BlockSpec((1, width), lambda i: (0, 0))],
        out_specs=pl.BlockSpec((tr, width), lambda i: (i, 0)),
        out_shape=jax.ShapeDtypeStruct((rows, width), CD), compiler_params=_cp("parallel"), name=name,
    )(x, g)


def _rms_bwd(x, g, dy, *, cb, width, tr, res=None, out_dtype, name):
    rows = x.shape[0]

    def body(*refs):
        if res is None:
            x_ref, g_ref, dy_ref, dx_ref, dg_ref = refs
        else:
            x_ref, g_ref, dy_ref, r_ref, dx_ref, dg_ref = refs

        @pl.when(pl.program_id(0) == 0)
        def _():
            dg_ref[...] = jnp.zeros_like(dg_ref)

        xv = x_ref[...].astype(F32)
        r = lax.rsqrt(jnp.mean(xv * xv, axis=-1, keepdims=True) + EPS)
        xh = xv * r
        dyv = dy_ref[...].astype(F32)
        dxh = dyv * g_ref[...]
        dx = r * (dxh - xh * jnp.mean(dxh * xh, axis=-1, keepdims=True))
        if res is not None:
            dx = dx + r_ref[...]
        dx_ref[...] = dx.astype(out_dtype)
        dg_ref[...] += jnp.sum(dyv * xh, axis=0, keepdims=True)

    blk = pl.BlockSpec((tr, width), lambda i: (i, 0))
    in_specs = [pl.BlockSpec((tr, width), lambda i: (i, cb)), pl.BlockSpec((1, width), lambda i: (0, 0)), blk]
    args = [x, g, dy]
    if res is not None:
        in_specs.append(blk)
        args.append(res)
    return pl.pallas_call(
        body, grid=(rows // tr,), in_specs=in_specs,
        out_specs=[blk, pl.BlockSpec((1, width), lambda i: (0, 0))],
        out_shape=[jax.ShapeDtypeStruct((rows, width), out_dtype), jax.ShapeDtypeStruct((1, width), F32)],
        compiler_params=_cp("arbitrary"), name=name,
    )(*args)


def _sigmoid(z):
    return 1.0 / (1.0 + jnp.exp(-z))


def _gate_fwd(o_mla, o_sb, proj, g_mla, g_sb, *, zb_mla, zb_sb, tr, name):
    rows, W = o_mla.shape

    def body(om_ref, os_ref, zm_ref, zs_ref, gm_ref, gs_ref, y_ref):
        for half, (o_ref, z_ref, g_ref) in enumerate(((om_ref, zm_ref, gm_ref), (os_ref, zs_ref, gs_ref))):
            o = o_ref[...].astype(F32)
            r = lax.rsqrt(jnp.mean(o * o, axis=-1, keepdims=True) + EPS)
            z = z_ref[...].astype(F32)
            y_ref[:, half * W:(half + 1) * W] = ((o * r * g_ref[...]) * (z * _sigmoid(z))).astype(y_ref.dtype)

    blk = pl.BlockSpec((tr, W), lambda i: (i, 0))
    gsp = pl.BlockSpec((1, W), lambda i: (0, 0))
    return pl.pallas_call(
        body, grid=(rows // tr,),
        in_specs=[blk, blk, pl.BlockSpec((tr, W), lambda i: (i, zb_mla)), pl.BlockSpec((tr, W), lambda i: (i, zb_sb)), gsp, gsp],
        out_specs=pl.BlockSpec((tr, 2 * W), lambda i: (i, 0)),
        out_shape=jax.ShapeDtypeStruct((rows, 2 * W), CD), compiler_params=_cp("parallel"), name=name,
    )(o_mla, o_sb, proj, proj, g_mla, g_sb)


def _gate_bwd(dy, o_mla, o_sb, proj, g_mla, g_sb, *, zb_mla, zb_sb, tr, name):
    rows, W = o_mla.shape

    def body(dy_ref, om_ref, os_ref, zm_ref, zs_ref, gm_ref, gs_ref, dom_ref, dos_ref, dzm_ref, dzs_ref, dgm_ref, dgs_ref):
        @pl.when(pl.program_id(0) == 0)
        def _():
            dgm_ref[...] = jnp.zeros_like(dgm_ref)
            dgs_ref[...] = jnp.zeros_like(dgs_ref)

        halves = ((om_ref, zm_ref, gm_ref, dom_ref, dzm_ref, dgm_ref), (os_ref, zs_ref, gs_ref, dos_ref, dzs_ref, dgs_ref))
        for half, (o_ref, z_ref, g_ref, do_ref, dz_ref, dg_ref) in enumerate(halves):
            dyv = dy_ref[:, half * W:(half + 1) * W].astype(F32)
            o = o_ref[...].astype(F32)
            r = lax.rsqrt(jnp.mean(o * o, axis=-1, keepdims=True) + EPS)
            oh = o * r
            g = g_ref[...]
            z = z_ref[...].astype(F32)
            sg = _sigmoid(z)
            dn = dyv * (z * sg)
            dz_ref[...] = (dyv * (oh * g) * (sg * (1.0 + z * (1.0 - sg)))).astype(dz_ref.dtype)
            dxh = dn * g
            do_ref[...] = (r * (dxh - oh * jnp.mean(dxh * oh, axis=-1, keepdims=True))).astype(do_ref.dtype)
            dg_ref[...] += jnp.sum(dn * oh, axis=0, keepdims=True)

    blk = pl.BlockSpec((tr, W), lambda i: (i, 0))
    gsp = pl.BlockSpec((1, W), lambda i: (0, 0))
    act = jax.ShapeDtypeStruct((rows, W), CD)
    gsh = jax.ShapeDtypeStruct((1, W), F32)
    return pl.pallas_call(
        body, grid=(rows // tr,),
        in_specs=[pl.BlockSpec((tr, 2 * W), lambda i: (i, 0)), blk, blk,
                  pl.BlockSpec((tr, W), lambda i: (i, zb_mla)), pl.BlockSpec((tr, W), lambda i: (i, zb_sb)), gsp, gsp],
        out_specs=[blk, blk, blk, blk, gsp, gsp], out_shape=[act, act, act, act, gsh, gsh],
        compiler_params=_cp("arbitrary"), name=name,
    )(dy, o_mla, o_sb, proj, proj, g_mla, g_sb)


QKW = 2 * LANES


def _rope_fwd(q_raw, kv, proj, cs, sn, *, H, krb, tr, name):
    rows = q_raw.shape[0]

    def body(q_ref, kn_ref, kr_ref, c_ref, s_ref, qo_ref, ko_ref):
        c, s = c_ref[...], s_ref[...]
        xk = kr_ref[...].astype(F32)
        kr = (xk * c + pltpu.roll(xk, ROPE, 1) * s).astype(ko_ref.dtype)
        for h in range(H):
            nope, rope = slice(h * QKW, h * QKW + LANES), slice(h * QKW + LANES, (h + 1) * QKW)
            qo_ref[:, nope] = q_ref[:, nope].astype(qo_ref.dtype)
            xh = q_ref[:, rope]
            qo_ref[:, rope] = (xh * c + pltpu.roll(xh, ROPE, 1) * s).astype(qo_ref.dtype)
            ko_ref[:, nope] = kn_ref[:, _head(h)]
            ko_ref[:, rope] = kr

    tab = pl.BlockSpec((tr, LANES), lambda i: (i, 0))
    wide = pl.BlockSpec((tr, H * QKW), lambda i: (i, 0))
    sh = jax.ShapeDtypeStruct((rows, H * QKW), CD)
    return pl.pallas_call(
        body, grid=(rows // tr,),
        in_specs=[wide, pl.BlockSpec((tr, H * LANES), lambda i: (i, 0)), pl.BlockSpec((tr, LANES), lambda i: (i, krb)), tab, tab],
        out_specs=[wide, wide], out_shape=[sh, sh], compiler_params=_cp("parallel"), name=name,
    )(q_raw, kv, proj, cs, sn)


def _rope_bwd(dq, dkcat, dv, cs, sn, *, H, tr, name):
    rows = dq.shape[0]
    HW = H * LANES

    def body(dq_ref, dk_ref, dv_ref, c_ref, s_ref, dqo_ref, dkv_ref, dkr_ref):
        c, s = c_ref[...], s_ref[...]
        dkr = jnp.zeros((tr, LANES), F32)
        for h in range(H):
            nope, rope = slice(h * QKW, h * QKW + LANES), slice(h * QKW + LANES, (h + 1) * QKW)
            dqo_ref[:, nope] = dq_ref[:, nope]
            d = dq_ref[:, rope].astype(F32)
            dqo_ref[:, rope] = (d * c + pltpu.roll(d * s, ROPE, 1)).astype(dqo_ref.dtype)
            dkv_ref[:, _head(h)] = dk_ref[:, nope].astype(dkv_ref.dtype)
            dkr = dkr + dk_ref[:, rope]
        dkv_ref[:, HW:] = dv_ref[...].astype(dkv_ref.dtype)
        dkr_ref[...] = (dkr * c + pltpu.roll(dkr * s, ROPE, 1)).astype(dkr_ref.dtype)

    tab = pl.BlockSpec((tr, LANES), lambda i: (i, 0))
    wide = pl.BlockSpec((tr, H * QKW), lambda i: (i, 0))
    return pl.pallas_call(
        body, grid=(rows // tr,), in_specs=[wide, wide, pl.BlockSpec((tr, HW), lambda i: (i, 0)), tab, tab],
        out_specs=[wide, wide, tab],
        out_shape=[jax.ShapeDtypeStruct((rows, H * QKW), CD), jax.ShapeDtypeStruct((rows, 2 * HW), CD), jax.ShapeDtypeStruct((rows, LANES), CD)],
        compiler_params=_cp("parallel"), name=name,
    )(dq, dkcat, dv, cs, sn)


def _loss_head(h, g, tgt, *, lo, hi, tr, name):
    rows, D = h.shape

    def body(h_ref, g_ref, t_ref, dh_ref, dg_ref, loss_ref):
        i = pl.program_id(0)

        @pl.when(i == 0)
        def _():
            dg_ref[...] = jnp.zeros_like(dg_ref)
            loss_ref[...] = jnp.zeros_like(loss_ref)

        xv = h_ref[...]
        r = lax.rsqrt(jnp.mean(xv * xv, axis=-1, keepdims=True) + EPS)
        xh = xv * r
        g_ = g_ref[...]
        rowid = i * tr + lax.broadcasted_iota(jnp.int32, (tr, 1), 0)
        valid = jnp.logical_and(rowid >= lo, rowid < hi)
        err = jnp.where(valid, xh * g_ - t_ref[...], 0.0)
        loss_ref[...] += jnp.sum(jnp.sum(err * err, axis=-1, keepdims=True), axis=0, keepdims=True) * (0.5 / D)
        dy = err * (1.0 / D)
        dxh = dy * g_
        dh_ref[...] = r * (dxh - xh * jnp.mean(dxh * xh, axis=-1, keepdims=True))
        dg_ref[...] += jnp.sum(dy * xh, axis=0, keepdims=True)

    blk = pl.BlockSpec((tr, D), lambda i: (i, 0))
    gsp = pl.BlockSpec((1, D), lambda i: (0, 0))
    return pl.pallas_call(
        body, grid=(rows // tr,), in_specs=[blk, gsp, blk],
        out_specs=[blk, gsp, pl.BlockSpec((1, 1), lambda i: (0, 0))],
        out_shape=[jax.ShapeDtypeStruct((rows, D), F32), jax.ShapeDtypeStruct((1, D), F32), jax.ShapeDtypeStruct((1, 1), F32)],
        compiler_params=_cp("arbitrary"), name=name,
    )(h, g, tgt)


HP = 2
HPW = HP * LANES
HP_FWD = 4


def _head(hh):
    return slice(hh * LANES, (hh + 1) * LANES)


def _qk(hh):
    return slice(hh * QKW, (hh + 1) * QKW)


def _mla_fwd(q, kcat, kv, *, H, bq, scale, name):
    rows = q.shape[0]
    HP, HPW = HP_FWD, HP_FWD * LANES

    def body(q_ref, k_ref, v_ref, o_ref, lse_ref):
        i = pl.program_id(1)
        causal = lax.broadcasted_iota(jnp.int32, (bq, bq), 1) <= lax.broadcasted_iota(jnp.int32, (bq, bq), 0)

        def block(j, carry, diag):
            off = pl.multiple_of(j * bq, bq)
            out = []
            for hh in range(HP):
                m, l, acc = carry[hh]
                s = _dot_nt(q_ref[:, _qk(hh)], k_ref[pl.ds(off, bq), _qk(hh)]) * scale
                if diag:
                    s = jnp.where(causal, s, NEG)
                m_new = jnp.maximum(m, jnp.max(s, axis=-1, keepdims=True))
                p = jnp.exp(s - m_new)
                alpha = jnp.exp(m - m_new)
                l = alpha * l + jnp.sum(p, axis=-1, keepdims=True)
                acc = alpha * acc + _dot(p.astype(CD), v_ref[pl.ds(off, bq), _head(hh)])
                out.append((m_new, l, acc))
            return tuple(out)

        init = tuple((jnp.full((bq, 1), NEG, F32), jnp.zeros((bq, 1), F32), jnp.zeros((bq, LANES), F32)) for _ in range(HP))
        carry = lax.fori_loop(0, i, lambda j, cr: block(j, cr, False), init)
        carry = block(i, carry, True)
        for hh in range(HP):
            m, l, acc = carry[hh]
            o_ref[:, _head(hh)] = (acc / l).astype(o_ref.dtype)
            lse_ref[hh] = m + jnp.log(l)

    return pl.pallas_call(
        body, grid=(H // HP, rows // bq),
        in_specs=[pl.BlockSpec((bq, HP * QKW), lambda h, i: (i, h)), pl.BlockSpec((rows, HP * QKW), lambda h, i: (0, h)),
                  pl.BlockSpec((rows, HPW), lambda h, i: (0, H // HP + h))],
        out_specs=[pl.BlockSpec((bq, HPW), lambda h, i: (i, h)), pl.BlockSpec((HP, bq, 1), lambda h, i: (h, i, 0))],
        out_shape=[jax.ShapeDtypeStruct((rows, H * LANES), CD), jax.ShapeDtypeStruct((H, rows, 1), F32)],
        compiler_params=_cp("arbitrary", "arbitrary"), name=name,
    )(q, kcat, kv)


def _mla_bwd(q, kcat, kv, o, do, lse, *, H, bq, scale, name):
    rows = q.shape[0]

    def body(q_ref, k_ref, v_ref, o_ref, do_ref, lse_ref, dq_ref, dk_ref, dv_ref):
        i = pl.program_id(1)

        @pl.when(i == 0)
        def _():
            dk_ref[...] = jnp.zeros_like(dk_ref)
            dv_ref[...] = jnp.zeros_like(dv_ref)

        causal = lax.broadcasted_iota(jnp.int32, (bq, bq), 1) <= lax.broadcasted_iota(jnp.int32, (bq, bq), 0)
        delta = [jnp.sum(do_ref[:, _head(hh)].astype(F32) * o_ref[:, _head(hh)].astype(F32), axis=-1, keepdims=True) for hh in range(HP)]

        def block(j, carry, diag):
            off = pl.multiple_of(j * bq, bq)
            out = []
            for hh in range(HP):
                q_, do_ = q_ref[:, _qk(hh)], do_ref[:, _head(hh)]
                kj, vj = k_ref[pl.ds(off, bq), _qk(hh)], v_ref[pl.ds(off, bq), _head(hh)]
                p = jnp.exp(_dot_nt(q_, kj) * scale - lse_ref[hh])
                if diag:
                    p = jnp.where(causal, p, 0.0)
                ds = (p * (_dot_nt(do_, vj) - delta[hh]) * scale).astype(CD)
                dk_ref[pl.ds(off, bq), _qk(hh)] += _dot_tn(ds, q_)
                dv_ref[pl.ds(off, bq), _head(hh)] += _dot_tn(p.astype(CD), do_)
                out.append(carry[hh] + _dot(ds, kj))
            return tuple(out)

        init = tuple(jnp.zeros((bq, QKW), F32) for _ in range(HP))
        carry = lax.fori_loop(0, i, lambda j, cr: block(j, cr, False), init)
        carry = block(i, carry, True)
        for hh in range(HP):
            dq_ref[:, _qk(hh)] = carry[hh].astype(dq_ref.dtype)

    qb = pl.BlockSpec((bq, HP * QKW), lambda h, i: (i, h))
    kb = pl.BlockSpec((rows, HP * QKW), lambda h, i: (0, h))
    ob = pl.BlockSpec((bq, HPW), lambda h, i: (i, h))
    return pl.pallas_call(
        body, grid=(H // HP, rows // bq),
        in_specs=[qb, kb, pl.BlockSpec((rows, HPW), lambda h, i: (0, H // HP + h)), ob, ob, pl.BlockSpec((HP, bq, 1), lambda h, i: (h, i, 0))],
        out_specs=[qb, kb, pl.BlockSpec((rows, HPW), lambda h, i: (0, h))],
        out_shape=[jax.ShapeDtypeStruct((rows, H * QKW), CD), jax.ShapeDtypeStruct((rows, H * QKW), F32), jax.ShapeDtypeStruct((rows, H * LANES), F32)],
        compiler_params=pltpu.CompilerParams(dimension_semantics=("arbitrary", "arbitrary"), vmem_limit_bytes=VMEM_LIMIT_WIDE), name=name,
    )(q, kcat, kv, o, do, lse)


def _log_sigmoids(z):
    lb = jnp.minimum(z, 0.0) - jnp.log(1.0 + jnp.exp(-jnp.abs(z)))
    return lb, lb - z


def _split(x):
    hi = x.astype(CD)
    return hi, (x - hi.astype(F32)).astype(CD)


def _rounded(x):
    hi, lo = _split(x)
    return hi.astype(F32) + lo.astype(F32)


def _tri(ck, rel, stack=1):
    j, s = np.arange(ck)[:, None], np.arange(ck)[None, :]
    return jnp.asarray(np.tile({"gt": j > s, "le": j <= s, "lt": j < s}[rel], (stack, 1)), CD)


def _cumsum_mm(hi, lo, tri2):
    return _dot(jnp.concatenate([hi, lo], axis=1), tri2)


def _sb_fwd(proj, *, H, qb0, kb0, vb0, bq, ck, scale, name):
    rows = proj.shape[0]
    nsub = bq // ck
    HP, HPW = HP_FWD, HP_FWD * LANES

    def body(q_ref, k_ref, v_ref, u_ref, o_ref, t_ref):
        i = pl.program_id(1)
        u = u_ref[...]
        strict = lax.broadcasted_iota(jnp.int32, (bq, bq), 1) < lax.broadcasted_iota(jnp.int32, (bq, bq), 0)

        def block(j, carry, diag):
            off = pl.multiple_of(j * bq, bq)
            out = []
            for hh in range(HP):
                c, acc = carry[hh]
                z = _dot_nt(q_ref[:, _head(hh)], k_ref[pl.ds(off, bq), _head(hh)]) * scale
                lb, lom = _log_sigmoids(z)
                if diag:
                    lom = jnp.where(strict, lom, 0.0)
                hi, lo = _split(lom)
                parts = [None] * nsub
                for s in reversed(range(nsub)):
                    ss = slice(s * ck, (s + 1) * ck)
                    rin = _cumsum_mm(hi[:, ss], lo[:, ss], u)
                    parts[s] = jnp.exp(lb[:, ss] + rin + c)
                    c = c + rin[:, :1] + _rounded(lom[:, s * ck:s * ck + 1])
                a = parts[0] if nsub == 1 else jnp.concatenate(parts, axis=1)
                if diag:
                    a = jnp.where(strict, a, 0.0)
                out.append((c, acc + _dot(a.astype(CD), v_ref[pl.ds(off, bq), _head(hh)])))
            return tuple(out)

        init = tuple((jnp.zeros((bq, 1), F32), jnp.zeros((bq, LANES), F32)) for _ in range(HP))
        carry = block(i, init, True)
        carry = lax.fori_loop(0, i, lambda jj, cr: block(i - 1 - jj, cr, False), carry)
        for hh in range(HP):
            o_ref[:, _head(hh)] = carry[hh][1].astype(o_ref.dtype)
            t_ref[hh] = carry[hh][0]

    kb = lambda off: pl.BlockSpec((rows, HPW), lambda h, i: (0, off // HP + h))
    return pl.pallas_call(
        body, grid=(H // HP, rows // bq),
        in_specs=[pl.BlockSpec((bq, HPW), lambda h, i: (i, qb0 // HP + h)), kb(kb0), kb(vb0), pl.BlockSpec((2 * ck, ck), lambda h, i: (0, 0))],
        out_specs=[pl.BlockSpec((bq, HPW), lambda h, i: (i, h)), pl.BlockSpec((HP, bq, 1), lambda h, i: (h, i, 0))],
        out_shape=[jax.ShapeDtypeStruct((rows, H * LANES), CD), jax.ShapeDtypeStruct((H, rows, 1), F32)],
        compiler_params=_cp("arbitrary", "arbitrary"), name=name,
    )(proj, proj, proj, _tri(ck, "gt", 2))


def _sb_bwd(proj, do, tot, *, H, qb0, kb0, vb0, bq, ck, scale, name):
    rows = proj.shape[0]
    nsub = bq // ck

    def body(q_ref, k_ref, v_ref, do_ref, t_ref, ule_ref, ult_ref, dq_ref, dk_ref, dv_ref):
        i = pl.program_id(1)

        @pl.when(i == 0)
        def _():
            dk_ref[...] = jnp.zeros_like(dk_ref)
            dv_ref[...] = jnp.zeros_like(dv_ref)

        ule, ult = ule_ref[...], ult_ref[...]
        strict = lax.broadcasted_iota(jnp.int32, (bq, bq), 1) < lax.broadcasted_iota(jnp.int32, (bq, bq), 0)

        def block(j, carry, diag):
            off = pl.multiple_of(j * bq, bq)
            out = []
            for hh in range(HP):
                pc, gc, dq = carry[hh]
                q_, do_, tot_ = q_ref[:, _head(hh)], do_ref[:, _head(hh)], t_ref[hh]
                kj, vj = k_ref[pl.ds(off, bq), _head(hh)], v_ref[pl.ds(off, bq), _head(hh)]
                z = _dot_nt(q_, kj) * scale
                lb, lom = _log_sigmoids(z)
                if diag:
                    lom = jnp.where(strict, lom, 0.0)
                hi, lo = _split(lom)
                parts = []
                for s in range(nsub):
                    ss = slice(s * ck, (s + 1) * ck)
                    pin = _cumsum_mm(hi[:, ss], lo[:, ss], ule)
                    parts.append(jnp.exp(lb[:, ss] + (tot_ - (pc + pin))))
                    pc = pc + pin[:, ck - 1:ck]
                a = parts[0] if nsub == 1 else jnp.concatenate(parts, axis=1)
                if diag:
                    a = jnp.where(strict, a, 0.0)
                g = a * _dot_nt(do_, vj)
                gb = g.astype(CD)
                parts = []
                for s in range(nsub):
                    ss = slice(s * ck, (s + 1) * ck)
                    gin = _dot(gb[:, ss], ult)
                    parts.append(gc + gin)
                    gc = gc + gin[:, ck - 1:ck] + g[:, (s + 1) * ck - 1:(s + 1) * ck].astype(CD).astype(F32)
                big_g = parts[0] if nsub == 1 else jnp.concatenate(parts, axis=1)
                sg = jnp.exp(lb)
                dz = (g * (1.0 - sg) - big_g * sg) * scale
                if diag:
                    dz = jnp.where(strict, dz, 0.0)
                dz = dz.astype(CD)
                dk_ref[pl.ds(off, bq), _head(hh)] += _dot_tn(dz, q_)
                dv_ref[pl.ds(off, bq), _head(hh)] += _dot_tn(a.astype(CD), do_)
                out.append((pc, gc, dq + _dot(dz, kj)))
            return tuple(out)

        init = tuple((jnp.zeros((bq, 1), F32), jnp.zeros((bq, 1), F32), jnp.zeros((bq, LANES), F32)) for _ in range(HP))
        carry = lax.fori_loop(0, i, lambda j, cr: block(j, cr, False), init)
        carry = block(i, carry, True)
        for hh in range(HP):
            dq_ref[:, _head(hh)] = carry[hh][2].astype(dq_ref.dtype)

    kb = lambda off: pl.BlockSpec((rows, HPW), lambda h, i: (0, off // HP + h))
    ob = pl.BlockSpec((bq, HPW), lambda h, i: (i, h))
    tri = pl.BlockSpec((ck, ck), lambda h, i: (0, 0))
    acc = jax.ShapeDtypeStruct((rows, H * LANES), F32)
    return pl.pallas_call(
        body, grid=(H // HP, rows // bq),
        in_specs=[pl.BlockSpec((bq, HPW), lambda h, i: (i, qb0 // HP + h)), kb(kb0), kb(vb0), ob,
                  pl.BlockSpec((HP, bq, 1), lambda h, i: (h, i, 0)), pl.BlockSpec((2 * ck, ck), lambda h, i: (0, 0)), tri],
        out_specs=[ob, kb(0), kb(0)],
        out_shape=[jax.ShapeDtypeStruct((rows, H * LANES), CD), acc, acc],
        compiler_params=_cp("arbitrary", "arbitrary"), name=name,
    )(proj, proj, proj, do, tot, _tri(ck, "le", 2), _tri(ck, "lt"))


def _dev_index(p):
    return 4 * p[0] + 2 * p[1] + p[2]


def _all_gather(arrays, *, name, after=None):
    A = len(arrays)
    extra = [] if after is None else [after]

    def body(*refs):
        ins, outs = refs[:A], refs[A + len(extra):2 * A + len(extra)]
        send_sems, recv_sems, local_sems = refs[2 * A + len(extra):]
        x, y, c = lax.axis_index("x"), lax.axis_index("y"), lax.axis_index("c")
        me, sibling = (x, y, c), (x, y, 1 - c)
        chips = [(1 - x, y), (x, 1 - y), (1 - x, 1 - y)]

        def copy(a, k, block, to, src=None):
            dst = outs[a].at[_dev_index(block)]
            return pltpu.make_async_remote_copy(
                src_ref=dst if src is None else src, dst_ref=dst, send_sem=send_sems.at[a * 7 + k],
                recv_sem=recv_sems.at[a * 7 + k], device_id=to, device_id_type=MESH)

        mine = [pltpu.make_async_copy(ins[a], outs[a].at[_dev_index(me)], local_sems.at[a]) for a in range(A)]
        for cp in mine:
            cp.start()
        first = []
        for a in range(A):
            first.append(copy(a, 0, me, sibling, src=ins[a]))
            first += [copy(a, 1 + j, me, (*chip, c), src=ins[a]) for j, chip in enumerate(chips)]
        for cp in first:
            cp.start()
        passed = []
        for j, chip in enumerate(chips):
            for a in range(A):
                copy(a, 1 + j, (*chip, c), me).wait_recv()
                fwd = copy(a, 4 + j, (*chip, c), sibling)
                fwd.start()
                passed.append(fwd)
        for a in range(A):
            copy(a, 0, sibling, me).wait_recv()
            for j, chip in enumerate(chips):
                copy(a, 4 + j, (*chip, 1 - c), me).wait_recv()
        for cp in first + passed:
            cp.wait_send()
        for cp in mine:
            cp.wait()

    any_spec = pl.BlockSpec(memory_space=pl.ANY)
    return pl.pallas_call(
        body, in_specs=[any_spec] * (A + len(extra)), out_specs=[any_spec] * A,
        out_shape=[jax.ShapeDtypeStruct((N_DEV,) + a.shape, a.dtype) for a in arrays],
        scratch_shapes=[pltpu.SemaphoreType.DMA((7 * A,)), pltpu.SemaphoreType.DMA((7 * A,)), pltpu.SemaphoreType.DMA((A,))],
        name=name,
    )(*arrays, *extra)


_HBM = pl.BlockSpec(memory_space=pltpu.HBM)
_SEM = pl.BlockSpec(memory_space=pltpu.SEMAPHORE)
_EFFECT = pltpu.SideEffectType.DATAFLOW_SIDE_EFFECTING
_PEER_ORDER = (0, 1, 3, 2, 4, 5, 6)


def _peers():
    x, y, c = lax.axis_index("x"), lax.axis_index("y"), lax.axis_index("c")
    return _dev_index((x, y, c)), [((1 - x) if r & 4 else x, (1 - y) if r & 2 else y, (1 - c) if r & 1 else c) for r in range(1, N_DEV)]


def _xchg_copy(src, land, sends, recvs, a, k, me_i, peers, scatter, arriving):
    peer_i = _dev_index(peers[k])
    return pltpu.make_async_remote_copy(
        src_ref=src.at[peer_i] if scatter else src, dst_ref=land.at[peer_i if arriving else me_i],
        send_sem=sends.at[a * 7 + k], recv_sem=recvs.at[a * 7 + k], device_id=peers[k], device_id_type=MESH)


def _xchg_start(arrays, *, scatter, name, after=None):
    A = len(arrays)
    lands = [lax.empty(a.shape if scatter else (N_DEV,) + a.shape, a.dtype) for a in arrays]
    extra = [] if after is None else [after]

    def body(*refs):
        srcs, land = refs[:A], refs[A:2 * A]
        sends, recvs, token = refs[2 * A + len(extra)], refs[2 * A + len(extra) + 1], refs[-1]
        me_i, peers = _peers()
        for k in _PEER_ORDER:
            for a in range(A):
                _xchg_copy(srcs[a], land[a], sends, recvs, a, k, me_i, peers, scatter, False).start()
        token[...] = jnp.zeros_like(token)

    hbm = lambda a: pltpu.HBM(a.shape, a.dtype)
    outs = pl.pallas_call(
        body, name=name, in_specs=[_HBM] * (2 * A) + [pl.BlockSpec(memory_space=pl.ANY)] * len(extra),
        out_shape=(pltpu.SemaphoreType.DMA((7 * A,)), pltpu.SemaphoreType.DMA((7 * A,)), *[hbm(a) for a in arrays],
                   *[hbm(a) for a in lands], jax.ShapeDtypeStruct((8, LANES), F32)),
        out_specs=(_SEM, _SEM, *([_HBM] * (2 * A)), pl.BlockSpec(memory_space=pltpu.VMEM)),
        input_output_aliases={n: 2 + n for n in range(2 * A)},
        compiler_params=pltpu.CompilerParams(has_side_effects=_EFFECT),
    )(*[pltpu.with_memory_space_constraint(a, pltpu.HBM) for a in arrays],
      *[pltpu.with_memory_space_constraint(a, pltpu.HBM) for a in lands], *extra)
    return outs[0], outs[1], list(outs[2:2 + A]), list(outs[2 + A:2 + 2 * A]), outs[-1]


def _xchg_wait(handle, after, *, scatter, name):
    sends, recvs, srcs, lands, _ = handle
    A = len(srcs)

    def body(*refs):
        src, land = refs[:A], refs[A:2 * A]
        send_sems, recv_sems = refs[2 * A], refs[2 * A + 1]
        me_i, peers = _peers()
        for k in _PEER_ORDER:
            for a in range(A):
                cp = _xchg_copy(src[a], land[a], send_sems, recv_sems, a, k, me_i, peers, scatter, True)
                cp.wait_send()
                cp.wait_recv()

    outs = pl.pallas_call(
        body, name=name, in_specs=[_HBM] * (2 * A) + [_SEM, _SEM, pl.BlockSpec(memory_space=pl.ANY)],
        out_shape=tuple(pltpu.HBM(a.shape, a.dtype) for a in srcs + lands), out_specs=tuple([_HBM] * (2 * A)),
        input_output_aliases={n: n for n in range(2 * A)},
        compiler_params=pltpu.CompilerParams(has_side_effects=_EFFECT),
    )(*srcs, *lands, sends, recvs, after)
    my = _dev_index((lax.axis_index("x"), lax.axis_index("y"), lax.axis_index("c")))
    got = []
    for src, land in zip(outs[:A], outs[A:]):
        own = lax.dynamic_index_in_dim(src, my, 0, keepdims=True) if scatter else src[None]
        got.append(lax.dynamic_update_index_in_dim(land, own, my, 0))
    return got


def _adamw(g, w, m, v):
    m = ADAM_B1 * m + (1.0 - ADAM_B1) * g
    v = ADAM_B2 * v + (1.0 - ADAM_B2) * (g * g)
    delta = -ADAM_LR * ((m / ADAM_BC1) / (jnp.sqrt(v / ADAM_BC2) + ADAM_EPS) + ADAM_WD * w)
    return delta, m, v


def _adam_pieces(pieces, w, m, v, bufs, *, layer, name):
    depth, R, C = w.shape
    if R % 16 == 0:
        tr, tc = _pick(R, [t for t in (512, 256, 128, 64, 32, 16) if t * C * 4 <= 2**20]), C
    else:
        tr, tc = R, _pick(C, [t for t in (512, 256, 128) if R * t * 4 <= 2**21])

    def body(p_ref, w_ref, m_ref, v_ref, b0, b1, b2, b3, g_ref, d_ref, mo_ref, vo_ref):
        g = p_ref[0].astype(F32)
        for k in range(1, N_DEV):
            g = g + p_ref[k].astype(F32)
        g_ref[0] = g
        d_ref[0], mo_ref[0], vo_ref[0] = _adamw(g, w_ref[0], m_ref[0], v_ref[0])

    blk = pl.BlockSpec((1, tr, tc), lambda i, j: (layer, i, j))
    anyspec = pl.BlockSpec(memory_space=pl.ANY)
    return pl.pallas_call(
        body, grid=(R // tr, C // tc),
        in_specs=[pl.BlockSpec((N_DEV, tr, tc), lambda i, j: (0, i, j)), blk, blk, blk] + [anyspec] * 4,
        out_specs=[blk] * 4, out_shape=[jax.ShapeDtypeStruct((depth, R, C), F32)] * 4,
        input_output_aliases={4: 0, 5: 1, 6: 2, 7: 3}, compiler_params=_cp("parallel", "parallel"), name=name,
    )(pieces, w, m, v, *bufs)


def _adam_direct(g, w, m, v, *, name):
    sh = jax.ShapeDtypeStruct(w.shape, F32)

    def body(g_ref, w_ref, m_ref, v_ref, d_ref, mo_ref, vo_ref):
        d_ref[...], mo_ref[...], vo_ref[...] = _adamw(g_ref[...], w_ref[...], m_ref[...], v_ref[...])

    return pl.pallas_call(body, out_shape=[sh] * 3, compiler_params=_cp(), name=name)(g, w, m, v)


def _sum_devices(parts, *, name):
    def body(p_ref, o_ref):
        s = p_ref[0]
        for k in range(1, N_DEV):
            s = s + p_ref[k]
        o_ref[...] = s

    return pl.pallas_call(body, out_shape=jax.ShapeDtypeStruct(parts.shape[1:], F32), compiler_params=_cp(), name=name)(parts)


def _rope_cols_pad(w):
    z = jnp.zeros(w.shape[:-1] + (ROPE // 2,), w.dtype)
    return jnp.concatenate([w[..., :ROPE // 2], z, w[..., ROPE // 2:], z], axis=-1)


def _rope_cols_unpad(w):
    return jnp.concatenate([w[..., :ROPE // 2], w[..., ROPE:ROPE + ROPE // 2]], axis=-1)


class _Dims:
    def __init__(self, D, S, NM, QL, KVL, WM, WS):
        self.D, self.S, self.NM, self.QL, self.KVL, self.WM, self.WS = D, S, NM, QL, KVL, WM, WS
        self.H = WM // LANES
        self.HW = self.H * LANES
        self.L = NM + S
        self.bq = 384 if self.L > 1024 else 128
        self.Lp = -(-self.L // self.bq) * self.bq
        self.ck_sb = LANES
        self.tr = 128
        self.sizes = (QL, KVL, ROPE, WM, WS, WS, WS, WS)
        self.d_in = sum(self.sizes)
        front = QL + KVL + LANES
        self.front = -(-front // WM) * WM
        self.pad = self.front - front
        self.dn = self.front + WM + 4 * WS
        assert WM == WS and QL % KVL == 0 and KVL % LANES == 0 and self.d_in % N_DEV == 0 and D % N_DEV == 0
        self.mla_scale = 1.0 / math.sqrt(LANES + ROPE)
        self.sb_scale = 1.0 / math.sqrt(LANES)


def _w_in_nice(d, wt):
    parts = jnp.split(wt, np.cumsum(d.sizes)[:-1].tolist(), axis=0)
    z = jnp.zeros((ROPE // 2, wt.shape[1]), wt.dtype)
    k_r = [parts[2][:ROPE // 2], z, parts[2][ROPE // 2:], z]
    return jnp.concatenate([parts[0], parts[1]] + k_r + [jnp.zeros((d.pad, wt.shape[1]), wt.dtype)] + parts[3:], axis=0)


def _w_in_orig(d, wt):
    a, b = d.QL + d.KVL, d.front
    return jnp.concatenate([wt[:a + ROPE // 2], wt[a + ROPE:a + ROPE + ROPE // 2], wt[b:]], axis=0)


def _w_uq_nice(d, w):
    w3 = w.reshape(d.QL, d.H, LANES + ROPE)
    return jnp.concatenate([w3[..., :LANES], _rope_cols_pad(w3[..., LANES:])], axis=-1).reshape(d.QL, d.H * QKW)


def _w_uq_orig(d, w):
    w3 = w.reshape(d.QL, d.H, QKW)
    return jnp.concatenate([w3[..., :LANES], _rope_cols_unpad(w3[..., LANES:])], axis=-1).reshape(d.QL, d.H * (LANES + ROPE))


def _w_ukv_nice(d, w):
    return w.reshape(d.KVL, d.H, 2, LANES).transpose(0, 2, 1, 3).reshape(d.KVL, 2 * d.HW)


def _w_ukv_orig(d, w):
    return w.reshape(d.KVL, 2, d.H, LANES).transpose(0, 2, 1, 3).reshape(d.KVL, 2 * d.HW)


def _cols_from_shards(g):
    return g.transpose(1, 0, 2).reshape(g.shape[1], -1)


def _cols_to_shards(w):
    return w.reshape(w.shape[0], N_DEV, -1).transpose(1, 0, 2)


def _rope_tables(rows):
    inv_freq = ROPE_THETA ** (-jnp.arange(0, ROPE, 2, dtype=F32) / ROPE)
    ang = jnp.arange(rows, dtype=jnp.int32).astype(F32)[:, None] * inv_freq[None, :]
    cos, sin, z = jnp.cos(ang), jnp.sin(ang), jnp.zeros_like(ang)
    return jnp.concatenate([cos, z, cos, z], axis=1), jnp.concatenate([-sin, z, sin, z], axis=1)


def _flat_pack(vecs):
    flat = jnp.concatenate([v.reshape(-1).astype(F32) for v in vecs])
    n = flat.shape[0]
    rows = -(-n // (8 * 1024)) * 8
    offs = np.cumsum([0] + [int(np.prod(v.shape)) for v in vecs])
    return jnp.pad(flat, (0, rows * 1024 - n)).reshape(rows, 1024), offs


def kernel(x, meta_tokens, g_norm, w_in, g_q, g_kv, w_uq, w_ukv, g_out_mla, g_out_sb, w_o, g_final, loss_target, m_meta_tokens, m_g_norm, m_w_in, m_g_q, m_g_kv, m_w_uq, m_w_ukv, m_g_out_mla, m_g_out_sb, m_w_o, m_g_final, v_meta_tokens, v_g_norm, v_w_in, v_g_q, v_g_kv, v_w_uq, v_w_ukv, v_g_out_mla, v_g_out_sb, v_w_o, v_g_final):
    depth = g_norm.shape[0]
    d = _Dims(D=x.shape[-1], S=x.shape[1], NM=meta_tokens.shape[0], QL=g_q.shape[1], KVL=g_kv.shape[1],
              WM=g_out_mla.shape[1], WS=g_out_sb.shape[1])
    D, Lp, H, tr = d.D, d.Lp, d.H, d.tr
    my = _dev_index((lax.axis_index("x"), lax.axis_index("y"), lax.axis_index("c")))

    w_in_t, m_w_in_t, v_w_in_t = (jnp.swapaxes(a, 1, 2) for a in (w_in, m_w_in, v_w_in))
    shards = lambda l: [w_in_t[l].astype(CD), w_uq[l].astype(CD), w_ukv[l].astype(CD), w_o[l].astype(CD)]
    g_in0, g_meta = _all_gather([shards(0)[0], meta_tokens], name="gather_layer0")
    gathers, token = {}, g_meta
    for l in range(depth):
        gathers[l] = _xchg_start(shards(l)[1:] if l == 0 else shards(l), scatter=False, name=f"gather_start_{l}", after=token)
        token = gathers[l][4]
    started = token[0, 0]

    def relayout(g_uq, g_ukv, g_o):
        return _w_uq_nice(d, _cols_from_shards(g_uq)), _w_ukv_nice(d, _cols_from_shards(g_ukv)), g_o.reshape(D, D)

    wn, wuq, wukv, wo = [None] * depth, [None] * depth, [None] * depth, [None] * depth
    meta_full = _cols_from_shards(g_meta)

    cs, sn = _rope_tables(Lp)
    h = jnp.concatenate([meta_full, x[0], jnp.zeros((Lp - d.L, D), F32)], axis=0)
    tgt = jnp.concatenate([jnp.zeros((d.NM, D), F32), loss_target[0], jnp.zeros((Lp - d.L, D), F32)], axis=0)

    b128 = lambda col: col // LANES
    krb = b128(d.QL + d.KVL)
    zb_mla, zb_sb = d.front // d.WM, (d.front + d.WM + 3 * d.WS) // d.WS
    qb0, kb0, vb0 = b128(d.front + d.WM), b128(d.front + d.WM + d.WS), b128(d.front + d.WM + 2 * d.WS)
    row = lambda a, i: a[i][None, :]

    saved = []
    for i in range(depth):
        if i == 0:
            wn[i] = _w_in_nice(d, g_in0.reshape(d.d_in, D))
            u = _rms_fwd(h, row(g_norm, i) + started, cb=0, width=D, tr=tr, name=f"rms_in_{i}")
            proj = _mm(u, wn[i], tb=True, out_dtype=CD, name=f"proj_{i}")
            wuq[i], wukv[i], wo[i] = relayout(*_xchg_wait(gathers[i], proj, scatter=False, name=f"gather_wait_{i}"))
        else:
            g_in, *g_rest = _xchg_wait(gathers[i], h, scatter=False, name=f"gather_wait_{i}")
            wn[i] = _w_in_nice(d, g_in.reshape(d.d_in, D))
            wuq[i], wukv[i], wo[i] = relayout(*g_rest)
            u = _rms_fwd(h, row(g_norm, i), cb=0, width=D, tr=tr, name=f"rms_in_{i}")
            proj = _mm(u, wn[i], tb=True, out_dtype=CD, name=f"proj_{i}")
        cqn = _rms_fwd(proj, row(g_q, i), cb=0, width=d.QL, tr=tr, name=f"rms_q_{i}")
        ckvn = _rms_fwd(proj, row(g_kv, i), cb=d.QL // d.KVL, width=d.KVL, tr=tr, name=f"rms_kv_{i}")
        q_raw = _mm(cqn, wuq[i], out_dtype=F32, name=f"up_q_{i}")
        kvn = _mm(ckvn, wukv[i], out_dtype=CD, name=f"up_kv_{i}")
        q, kcat = _rope_fwd(q_raw, kvn, proj, cs, sn, H=H, krb=krb, tr=tr, name=f"rope_{i}")
        o_mla, lse = _mla_fwd(q, kcat, kvn, H=H, bq=d.bq, scale=d.mla_scale, name=f"mla_fwd_{i}")
        o_sb, tot = _sb_fwd(proj, H=H, qb0=qb0, kb0=kb0, vb0=vb0, bq=d.bq, ck=d.ck_sb, scale=d.sb_scale, name=f"sb_fwd_{i}")
        y = _gate_fwd(o_mla, o_sb, proj, row(g_out_mla, i), row(g_out_sb, i), zb_mla=zb_mla, zb_sb=zb_sb, tr=tr, name=f"gate_fwd_{i}")
        h_next = _mm(y, wo[i], out_dtype=F32, res=h, name=f"out_proj_{i}")
        saved.append((h, u, proj, cqn, ckvn, q, kvn, kcat, o_mla, lse, o_sb, tot, y))
        h = h_next

    dh, dg_final, loss_part = _loss_head(h, g_final[None, :], tgt, lo=d.NM, hi=d.L, tr=tr, name="loss_head")

    dg_norm, dg_q, dg_kv, dg_om, dg_os = [None] * depth, [None] * depth, [None] * depth, [None] * depth, [None] * depth
    big_w = {"w_in": (w_in_t, m_w_in_t, v_w_in_t), "w_uq": (w_uq, m_w_uq, v_w_uq), "w_ukv": (w_ukv, m_w_ukv, v_w_ukv), "w_o": (w_o, m_w_o, v_w_o)}
    res = {n: [lax.empty(w.shape, F32) for _ in range(4)] for n, (w, _, _) in big_w.items()}
    exchanges = [[] for _ in range(depth)]

    def send(l, tag, pieces, after=None):
        handle = _xchg_start(list(pieces.values()), scatter=True, name=f"exch_start_{l}{tag}", after=after)
        exchanges[l].append((handle, list(pieces), tag))
        return handle[4]

    def finish(l, after):
        got = None
        for handle, sent, tag in exchanges[l]:
            got = _xchg_wait(handle, after, scatter=True, name=f"exch_wait_{l}{tag}")
            for n, p in zip(sent, got):
                w, m, v = big_w[n]
                res[n] = _adam_pieces(p, w, m, v, res[n], layer=l, name=f"adam_{n}_{l}")
        return got[0]

    cat = lambda parts: jnp.concatenate(parts, axis=0)
    zpad = jnp.zeros((Lp, d.pad), CD)
    for i in reversed(range(depth)):
        h_in, u, proj, cqn, ckvn, q, kvn, kcat, o_mla, lse, o_sb, tot, y = saved[i]
        dy = _mm(dh, wo[i], tb=True, out_dtype=CD, name=f"d_y_{i}")
        dwo = _mm(y, dh, ta=True, out_dtype=CD, name=f"d_wo_{i}")
        do_mla, do_sb, dz_mla, dz_sb, dg_om[i], dg_os[i] = _gate_bwd(
            dy, o_mla, o_sb, proj, row(g_out_mla, i), row(g_out_sb, i), zb_mla=zb_mla, zb_sb=zb_sb, tr=tr, name=f"gate_bwd_{i}")
        dq_mla, dkcat, dv = _mla_bwd(q, kcat, kvn, o_mla, do_mla, lse, H=H, bq=d.bq, scale=d.mla_scale, name=f"mla_bwd_{i}")
        dq_sb, dk_sb, dv_sb = _sb_bwd(proj, do_sb, tot, H=H, qb0=qb0, kb0=kb0, vb0=vb0, bq=d.bq, ck=d.ck_sb, scale=d.sb_scale, name=f"sb_bwd_{i}")
        dq, dkv, dk_r = _rope_bwd(dq_mla, dkcat, dv, cs, sn, H=H, tr=tr, name=f"rope_bwd_{i}")
        dcqn = _mm(dq, wuq[i], tb=True, out_dtype=CD, name=f"d_cq_{i}")
        dwuq = _mm(cqn, dq, ta=True, out_dtype=CD, name=f"d_wuq_{i}")
        dckvn = _mm(dkv, wukv[i], tb=True, out_dtype=CD, name=f"d_ckv_{i}")
        dwukv = _mm(ckvn, dkv, ta=True, out_dtype=CD, name=f"d_wukv_{i}")
        dc_q, dg_q[i] = _rms_bwd(proj, row(g_q, i), dcqn, cb=0, width=d.QL, tr=tr, out_dtype=CD, name=f"rms_q_bwd_{i}")
        dc_kv, dg_kv[i] = _rms_bwd(proj, row(g_kv, i), dckvn, cb=d.QL // d.KVL, width=d.KVL, tr=tr, out_dtype=CD, name=f"rms_kv_bwd_{i}")
        rest = {"w_uq": _cols_to_shards(_w_uq_orig(d, dwuq)), "w_ukv": _cols_to_shards(_w_ukv_orig(d, dwukv)),
                "w_o": dwo.reshape(N_DEV, D // N_DEV, D)}
        placed = None
        if i == 0:
            small_a, offs_a = _flat_pack(([cat(dg_norm[1:])] if depth > 1 else []) + [cat(dg_q), cat(dg_kv), cat(dg_om), cat(dg_os), dg_final, loss_part])
            (small_a_all,) = _all_gather([small_a], name="gather_small_a")
            placed = send(0, "a", rest, after=small_a_all)
        dproj = jnp.concatenate([dc_q, dc_kv, dk_r, zpad, dz_mla, dq_sb, dk_sb.astype(CD), dv_sb.astype(CD), dz_sb], axis=1)
        dwn = _mm(dproj, u, ta=True, out_dtype=CD, after=placed, name=f"d_wn_{i}")
        pieces = {"w_in": _w_in_orig(d, dwn).reshape(N_DEV, d.d_in // N_DEV, D)}
        if i > 0:
            pieces.update(rest)
        du = _mm(dproj, wn[i], out_dtype=CD, after=send(i, "b", pieces), name=f"d_u_{i}")
        dh, dg_norm[i] = _rms_bwd(h_in, row(g_norm, i), du, cb=0, width=D, tr=tr, res=dh, out_dtype=F32, name=f"rms_in_bwd_{i}")
    last_start = exchanges[0][-1][0][4]
    for l in range(depth - 1, 0, -1):
        finish(l, last_start)
    landed = finish(0, res["w_in"][0] if depth > 1 else dh)

    small_b, offs_b = _flat_pack([dg_norm[0], dh[:d.NM]])
    (small_b_all,) = _all_gather([small_b], name="gather_small_b", after=landed)
    sum_a = _sum_devices(small_a_all, name="sum_small_a").reshape(-1)
    sum_b = _sum_devices(small_b_all, name="sum_small_b").reshape(-1)
    n_a = int(offs_a[-2])
    small_sum = jnp.concatenate([sum_b[:D], sum_a[:n_a], sum_b[D:offs_b[2]], sum_a[n_a:n_a + 1]])
    offs = np.cumsum([0, depth * D, depth * d.QL, depth * d.KVL, depth * d.WM, depth * d.WS, D, d.NM * D, 1])
    seg = lambda k, shape: small_sum[offs[k]:offs[k + 1]].reshape(shape)
    loss = small_sum[offs[7]]
    grad_x = dh[d.NM:d.L][None]

    g_meta_full = seg(6, (d.NM, D))
    ncol = meta_tokens.shape[1]
    g_meta_mine = lax.dynamic_slice(g_meta_full, (0, my * ncol), (d.NM, ncol))
    res["meta_tokens"] = [g_meta_mine] + list(_adam_direct(g_meta_mine, meta_tokens, m_meta_tokens, v_meta_tokens, name="adam_meta"))

    small_names = ["g_norm", "g_q", "g_kv", "g_out_mla", "g_out_sb", "g_final"]
    small_w = [g_norm, g_q, g_kv, g_out_mla, g_out_sb, g_final]
    small_m = [m_g_norm, m_g_q, m_g_kv, m_g_out_mla, m_g_out_sb, m_g_final]
    small_v = [v_g_norm, v_g_q, v_g_kv, v_g_out_mla, v_g_out_sb, v_g_final]
    g_flat, _ = _flat_pack([small_sum[:offs[6]]])
    wf, _ = _flat_pack(small_w)
    mf, _ = _flat_pack(small_m)
    vf, _ = _flat_pack(small_v)
    d_f, m_f, v_f = _adam_direct(g_flat, wf, mf, vf, name="adam_small")
    for k, nm in enumerate(small_names):
        sl = lambda f: f.reshape(-1)[offs[k]:offs[k + 1]].reshape(small_w[k].shape)
        res[nm] = [sl(g_flat), sl(d_f), sl(m_f), sl(v_f)]

    names = ["meta_tokens", "g_norm", "w_in", "g_q", "g_kv", "w_uq", "w_ukv", "g_out_mla", "g_out_sb", "w_o", "g_final"]
    out = lambda n, j: jnp.swapaxes(res[n][j], 1, 2) if n == "w_in" else res[n][j]
    return (loss, grad_x, *[out(n, 0) for n in names], *[out(n, 1) for n in names],
            *[out(n, 2) for n in names], *[out(n, 3) for n in names])
```
